```python
import math, functools
import jax, jax.numpy as jnp
from jax import lax
import numpy as np

D_MODEL = 1024
BATCH = 4
SEQ = 4096
DEPTH = 1
DEC_BATCH = 128
DEC_SEQ = 4
PAST_LEN = 2048
PAGE_SIZE = 128

A_GROUPS = ((128, 1), (512, 4), (2048, 16))
A_HEADS_PER_GROUP = 4
A_HEADS = 12
A_HEAD_DIM = 128
A_WIDTH = A_HEADS * A_HEAD_DIM
A_OUT = A_HEADS_PER_GROUP * A_HEAD_DIM
A_BLOCK = 128
G_HEADS = 8
G_DK = 128
G_DV = 128
G_QK = G_HEADS * G_DK
G_V = G_HEADS * G_DV
G_CONV_CH = 2 * G_QK + G_V
G_CONV = 4
G_CHUNK = 64
N_BRANCH = 2
D_FF = 2816
NORM_EPS = 1e-6
IN_SIZES = (A_WIDTH, A_WIDTH, A_WIDTH, G_CONV_CH, G_V, G_HEADS, G_HEADS, N_BRANCH * D_MODEL)
IN_WIDTH = 3 * A_WIDTH + G_CONV_CH + G_V + 2 * G_HEADS + N_BRANCH * D_MODEL

kernel_name = 'hybrid_dilated_attn_gated_deltanet_step'


def _rms(x, w):
    xf = x.astype(jnp.float32)
    y = xf * lax.rsqrt(jnp.mean(xf * xf, axis=-1, keepdims=True) + NORM_EPS)
    return (y * w.astype(jnp.float32)).astype(x.dtype)


def _swiglu(x, w_gu, w_down):
    gate, up = jnp.split(x @ w_gu, 2, axis=-1)
    return (jax.nn.silu(gate) * up) @ w_down


def _split_cols(u, sizes):
    idx, acc = [], 0
    for s in sizes[:-1]:
        acc += s
        idx.append(acc)
    return jnp.split(u, idx, axis=-1)


def _alibi_slopes():
    return jnp.exp2(-8.0 * jnp.arange(1, A_HEADS + 1, dtype=jnp.float32) / A_HEADS)


def _dilated_prompt(q, k, v, window, dil, slopes):
    B, T, H, E = q.shape
    n = T // dil
    nb = -(-n // A_BLOCK)
    npad = nb * A_BLOCK
    wsub = window // dil

    def to_sub(a):
        a = a.reshape(B, n, dil, H, E).transpose(0, 2, 1, 3, 4)
        return jnp.pad(a, ((0, 0), (0, 0), (A_BLOCK, npad - n), (0, 0), (0, 0)))

    def band(a):
        a = to_sub(a)
        prev = a[:, :, :npad].reshape(B, dil, nb, A_BLOCK, H, E)
        cur = a[:, :, A_BLOCK:].reshape(B, dil, nb, A_BLOCK, H, E)
        return jnp.concatenate([prev, cur], axis=3)

    qs = to_sub(q)[:, :, A_BLOCK:].reshape(B, dil, nb, A_BLOCK, H, E)
    kb, vb = band(k), band(v)
    s = jnp.einsum('brnqhe,brnkhe->brnhqk', qs, kb, preferred_element_type=jnp.float32) * (E ** -0.5)
    qi = jnp.arange(A_BLOCK)[:, None]
    kj = jnp.arange(2 * A_BLOCK)[None, :]
    delta = A_BLOCK + qi - kj
    kpos = jnp.arange(nb)[:, None, None] * A_BLOCK + kj[None] - A_BLOCK
    valid = (delta >= 0) & (delta <= wsub) & (kpos >= 0)
    bias = -slopes.astype(jnp.float32)[:, None, None] * (dil * delta).astype(jnp.float32)
    s = jnp.where(valid[:, None], s + bias, -jnp.inf)
    m = jnp.max(s, axis=-1, keepdims=True)
    p = jnp.exp(s - m)
    den = jnp.sum(p, axis=-1)
    o = jnp.einsum('brnhqk,brnkhe->brnqhe', p, vb.astype(jnp.float32)) / jnp.swapaxes(den, -1, -2)[..., None]
    lse = jnp.swapaxes(m[..., 0] + jnp.log(den), -1, -2)
    o = o.reshape(B, dil, npad, H, E)[:, :, :n].transpose(0, 2, 1, 3, 4).reshape(B, T, H, E)
    lse = lse.reshape(B, dil, npad, H)[:, :, :n].transpose(0, 2, 1, 3).reshape(B, T, H)
    return o, lse


def _dilated_sample(q, k, v, kv_buf, window, dil, slopes):
    Bd, S, H, E = q.shape
    wb = kv_buf.shape[1]
    kc = jnp.concatenate([kv_buf[:, :, 0].astype(k.dtype), k], axis=1)
    vc = jnp.concatenate([kv_buf[:, :, 1].astype(v.dtype), v], axis=1)
    j = jnp.arange(window // dil + 1)
    idx = wb + jnp.arange(S)[:, None] - dil * j[None, :]
    valid = idx >= 0
    idx = jnp.maximum(idx, 0)
    kg = kc[:, idx]
    vg = vc[:, idx]
    s = jnp.einsum('bshe,bsjhe->bhsj', q, kg, preferred_element_type=jnp.float32) * (E ** -0.5)
    s = jnp.where(valid, s - slopes.astype(jnp.float32)[:, None, None] * (dil * j).astype(jnp.float32), -jnp.inf)
    m = jnp.max(s, axis=-1, keepdims=True)
    p = jnp.exp(s - m)
    den = jnp.sum(p, axis=-1)
    o = jnp.einsum('bhsj,bsjhe->bshe', p, vg.astype(jnp.float32)) / jnp.swapaxes(den, 1, 2)[..., None]
    lse = jnp.swapaxes(m[..., 0] + jnp.log(den), 1, 2)
    return o, lse


def _merge_groups(outs, lses, dtype):
    o = jnp.stack(outs, axis=0)
    w = jax.nn.softmax(jnp.stack(lses, axis=0), axis=0)
    o = jnp.sum(w[..., None] * o, axis=0)
    B, T = o.shape[:2]
    return o.reshape(B, T, A_OUT).astype(dtype)


def _group_heads(gi):
    return slice(gi * A_HEADS_PER_GROUP, (gi + 1) * A_HEADS_PER_GROUP)


def _attn_prompt(q, k, v, slopes):
    T = q.shape[1]
    outs, lses, rows = [], [], []
    for gi, (win, dil) in enumerate(A_GROUPS):
        hs = _group_heads(gi)
        kg, vg = k[:, :, hs], v[:, :, hs]
        o, l = _dilated_prompt(q[:, :, hs], kg, vg, win, dil, slopes[hs])
        keep = min(win, T)
        rows.append(jnp.stack([kg[:, T - keep:], vg[:, T - keep:]], axis=2))
        outs.append(o)
        lses.append(l)
    return _merge_groups(outs, lses, q.dtype), rows


def _attn_sample(q, k, v, bufs, slopes):
    outs, lses, rows = [], [], []
    for gi, (win, dil) in enumerate(A_GROUPS):
        hs = _group_heads(gi)
        kg, vg = k[:, :, hs], v[:, :, hs]
        o, l = _dilated_sample(q[:, :, hs], kg, vg, bufs[gi], win, dil, slopes[hs])
        rows.append(jnp.stack([kg, vg], axis=2))
        outs.append(o)
        lses.append(l)
    return _merge_groups(outs, lses, q.dtype), rows


def _short_conv(u, buf, w):
    T = u.shape[1]
    full = jnp.concatenate([buf, u], axis=1)
    y = full[:, :T] * w[0]
    for j in range(1, G_CONV):
        y = y + full[:, j:j + T] * w[j]
    return y, full[:, T:]


def _l2norm(x):
    return x * lax.rsqrt(jnp.sum(x * x, axis=-1, keepdims=True) + NORM_EPS)


def _gated_delta_chunked(q, k, v, g, beta, s0):
    B, T, H, _ = q.shape
    C = min(G_CHUNK, T)
    nc = -(-T // C)
    pad = nc * C - T

    def chunks(a):
        a = jnp.pad(a, ((0, 0), (0, pad)) + ((0, 0),) * (a.ndim - 2))
        a = a.reshape((B, nc, C) + a.shape[2:])
        return jnp.moveaxis(a, (1, 3), (0, 2))

    qc, kc, vc, gc, bc = (chunks(a) for a in (q, k, v, g, beta))
    gc = jnp.cumsum(gc, axis=-1)
    incl = jnp.tril(jnp.ones((C, C), bool))
    strict = jnp.tril(jnp.ones((C, C), bool), -1)
    decay = jnp.exp(jnp.where(incl, gc[..., :, None] - gc[..., None, :], -jnp.inf))
    kb = kc * bc[..., None]
    a_low = jnp.where(strict, jnp.einsum('nbhik,nbhjk->nbhij', kb, kc) * decay, 0.0)
    t_mat = a_low + jnp.eye(C, dtype=a_low.dtype)
    solve = functools.partial(lax.linalg.triangular_solve, left_side=True, lower=True, unit_diagonal=True)
    u = solve(t_mat, vc * bc[..., None])
    w = solve(t_mat, kb * jnp.exp(gc)[..., None])
    qk = jnp.einsum('nbhik,nbhjk->nbhij', qc, kc) * decay
    qg = qc * jnp.exp(gc)[..., None]
    kd = kc * jnp.exp(gc[..., -1:] - gc)[..., None]
    g_end = jnp.exp(gc[..., -1])

    def step(s, xs):
        u_i, w_i, qk_i, qg_i, kd_i, ge_i = xs
        v_new = u_i - jnp.einsum('bhck,bhkv->bhcv', w_i, s)
        o_i = jnp.einsum('bhck,bhkv->bhcv', qg_i, s) + jnp.einsum('bhcj,bhjv->bhcv', qk_i, v_new)
        s = s * ge_i[..., None, None] + jnp.einsum('bhck,bhcv->bhkv', kd_i, v_new)
        return s, o_i

    s_end, o = lax.scan(step, s0, (u, w, qk, qg, kd, g_end))
    o = jnp.transpose(o, (1, 0, 3, 2, 4)).reshape(B, nc * C, H, -1)[:, :T]
    return o, s_end


def _gated_deltanet(qkv, z, b_logit, a_logit, conv_buf, s0, conv_w, a_log, dt_bias, norm_w):
    B, T, _ = qkv.shape
    y, conv_new = _short_conv(qkv, conv_buf.astype(qkv.dtype), conv_w)
    y = jax.nn.silu(y.astype(jnp.float32))
    q, k, v = jnp.split(y, [G_QK, 2 * G_QK], axis=-1)
    q = _l2norm(q.reshape(B, T, G_HEADS, G_DK)) * (G_DK ** -0.5)
    k = _l2norm(k.reshape(B, T, G_HEADS, G_DK))
    v = v.reshape(B, T, G_HEADS, G_DV)
    beta = jax.nn.sigmoid(b_logit.astype(jnp.float32))
    g = -jnp.exp(a_log.astype(jnp.float32)) * jax.nn.softplus(a_logit.astype(jnp.float32) + dt_bias.astype(jnp.float32))
    o, s_new = _gated_delta_chunked(q, k, v, g, beta, s0.astype(jnp.float32))
    o = o * lax.rsqrt(jnp.mean(o * o, axis=-1, keepdims=True) + NORM_EPS) * norm_w.astype(jnp.float32)
    o = o * jax.nn.silu(z.astype(jnp.float32).reshape(B, T, G_HEADS, G_DV))
    return o.reshape(B, T, G_V).astype(qkv.dtype), conv_new, s_new


def _layer(x, attn_a, conv_buf, s0, norm_ffn1, w_ffn1_gu, w_ffn1_down, norm_mix, w_in, conv_w,
           gdn_a_log, gdn_dt_bias, gdn_norm, w_proj_a, w_proj_b, w_out, norm_ffn2, w_ffn2_gu, w_ffn2_down):
    B, T, _ = x.shape
    x = x + 0.5 * _swiglu(_rms(x, norm_ffn1), w_ffn1_gu, w_ffn1_down)
    h = _rms(x, norm_mix)
    qa, ka, va, qkv_b, z_b, b_b, a_b, gate = _split_cols(h @ w_in, IN_SIZES)
    heads = lambda t: t.reshape(B, T, A_HEADS, A_HEAD_DIM)
    o_a, kv_rows = attn_a(heads(qa), heads(ka), heads(va))
    o_b, conv_new, s_new = _gated_deltanet(qkv_b, z_b, b_b, a_b, conv_buf, s0, conv_w,
                                           gdn_a_log, gdn_dt_bias, gdn_norm)
    gt = jax.nn.sigmoid(gate.astype(jnp.float32))
    merged = (gt[..., :D_MODEL] * (o_a @ w_proj_a).astype(jnp.float32)
              + gt[..., D_MODEL:] * (o_b @ w_proj_b).astype(jnp.float32))
    x = x + merged.astype(x.dtype) @ w_out
    x = x + 0.5 * _swiglu(_rms(x, norm_ffn2), w_ffn2_gu, w_ffn2_down)
    return x, kv_rows, conv_new, s_new


def setup_inputs(seed: int = 0) -> dict:
    key = jax.random.key(seed)
    ks = jax.random.split(key, 24)
    f32 = jnp.float32
    nrm = lambda k, shape, scale: jax.random.normal(k, shape, f32) * scale
    gain = lambda k, shape: 1.0 + 0.02 * jax.random.normal(k, shape, f32)
    wins = [min(w, PAST_LEN) for (w, _) in A_GROUPS]
    kv_shape = lambda wb: (DEPTH, DEC_BATCH, wb, 2, A_HEADS_PER_GROUP, A_HEAD_DIM)
    dt = jnp.exp(jax.random.uniform(ks[14], (DEPTH, G_HEADS), f32, math.log(1e-3), math.log(0.1)))
    return {
        'x_prompt': nrm(ks[0], (BATCH, SEQ, D_MODEL), 1.0),
        'x_sample': nrm(ks[1], (DEC_BATCH, DEC_SEQ, D_MODEL), 1.0),
        'cache_kv_w128': nrm(ks[2], kv_shape(wins[0]), 1.0),
        'cache_kv_w512': nrm(ks[3], kv_shape(wins[1]), 1.0),
        'cache_kv_w2048': nrm(ks[4], kv_shape(wins[2]), 1.0),
        'state_conv': nrm(ks[5], (DEPTH, DEC_BATCH, G_CONV - 1, G_CONV_CH), 1.0),
        'state_ssm': nrm(ks[6], (DEPTH, DEC_BATCH, G_HEADS, G_DK, G_DV), 0.1),
        'norm_ffn1': gain(ks[7], (DEPTH, D_MODEL)),
        'w_ffn1_gu': nrm(ks[8], (DEPTH, D_MODEL, 2 * D_FF), D_MODEL ** -0.5),
        'w_ffn1_down': nrm(ks[9], (DEPTH, D_FF, D_MODEL), D_FF ** -0.5),
        'norm_mix': gain(ks[10], (DEPTH, D_MODEL)),
        'w_in': nrm(ks[11], (DEPTH, D_MODEL, IN_WIDTH), D_MODEL ** -0.5),
        'conv_w': nrm(ks[12], (DEPTH, G_CONV, G_CONV_CH), G_CONV ** -0.5),
        'gdn_a_log': jnp.log(jax.random.uniform(ks[13], (DEPTH, G_HEADS), f32, 1.0, 16.0)),
        'gdn_dt_bias': dt + jnp.log(-jnp.expm1(-dt)),
        'gdn_norm': gain(ks[15], (DEPTH, G_DV)),
        'w_proj_a': nrm(ks[16], (DEPTH, A_OUT, D_MODEL), A_OUT ** -0.5),
        'w_proj_b': nrm(ks[17], (DEPTH, G_V, D_MODEL), G_V ** -0.5),
        'w_out': nrm(ks[18], (DEPTH, D_MODEL, D_MODEL), D_MODEL ** -0.5),
        'norm_ffn2': gain(ks[19], (DEPTH, D_MODEL)),
        'w_ffn2_gu': nrm(ks[20], (DEPTH, D_MODEL, 2 * D_FF), D_MODEL ** -0.5),
        'w_ffn2_down': nrm(ks[21], (DEPTH, D_FF, D_MODEL), D_FF ** -0.5),
        'norm_out': gain(ks[22], (D_MODEL,)),
    }


def reference(x_prompt, x_sample, cache_kv_w128, cache_kv_w512, cache_kv_w2048, state_conv, state_ssm,
              norm_ffn1, w_ffn1_gu, w_ffn1_down, norm_mix, w_in, conv_w, gdn_a_log, gdn_dt_bias, gdn_norm,
              w_proj_a, w_proj_b, w_out, norm_ffn2, w_ffn2_gu, w_ffn2_down, norm_out):
    slopes = _alibi_slopes()
    xp, xs = x_prompt, x_sample
    bp = xp.shape[0]
    p_rows = [[], [], []]
    s_rows = [[], [], []]
    p_conv, p_ssm, s_conv, s_ssm = [], [], [], []
    for l in range(DEPTH):
        lw = (norm_ffn1[l], w_ffn1_gu[l], w_ffn1_down[l], norm_mix[l], w_in[l], conv_w[l],
              gdn_a_log[l], gdn_dt_bias[l], gdn_norm[l], w_proj_a[l], w_proj_b[l], w_out[l],
              norm_ffn2[l], w_ffn2_gu[l], w_ffn2_down[l])
        conv0 = jnp.zeros((bp, G_CONV - 1, G_CONV_CH), xp.dtype)
        ssm0 = jnp.zeros((bp, G_HEADS, G_DK, G_DV), jnp.float32)
        attn_p = functools.partial(_attn_prompt, slopes=slopes)
        xp, rows, cb, sb = _layer(xp, attn_p, conv0, ssm0, *lw)
        for gi in range(len(A_GROUPS)):
            p_rows[gi].append(rows[gi])
        p_conv.append(cb)
        p_ssm.append(sb)
        bufs = (cache_kv_w128[l], cache_kv_w512[l], cache_kv_w2048[l])
        attn_s = functools.partial(_attn_sample, bufs=bufs, slopes=slopes)
        xs, rows, cb, sb = _layer(xs, attn_s, state_conv[l], state_ssm[l], *lw)
        for gi in range(len(A_GROUPS)):
            s_rows[gi].append(rows[gi])
        s_conv.append(cb)
        s_ssm.append(sb)
    y_prompt = _rms(xp, norm_out)
    y_sample = _rms(xs, norm_out)
    return (y_prompt, y_sample,
            jnp.stack(p_rows[0]), jnp.stack(p_rows[1]), jnp.stack(p_rows[2]), jnp.stack(p_conv), jnp.stack(p_ssm),
            jnp.stack(s_rows[0]), jnp.stack(s_rows[1]), jnp.stack(s_rows[2]), jnp.stack(s_conv), jnp.stack(s_ssm))
```

```python
import functools

import jax
import jax.numpy as jnp
from jax import lax
from jax.experimental import pallas as pl
from jax.experimental.pallas import tpu as pltpu

F32 = jnp.float32
BF16 = jnp.bfloat16
HIGHEST = lax.Precision.HIGHEST

D_MODEL = 1024
D_FF = 2816
NORM_EPS = 1e-6
A_GROUPS = ((128, 1), (512, 4), (2048, 16))
A_HPG = 4
A_E = 128
A_BLOCK = 128
A_GW = A_HPG * A_E
A_ST = 2048
G_HEADS = 8
G_DK = 128
G_DV = 128
G_QK = G_HEADS * G_DK
G_CONV_CH = 3 * G_QK
G_CONV = 4
G_CHUNK = 64

QKVB_OFF = 0
Z_OFF = 3072
GATE_OFF = 4096
KV_OFF = 6144
QA_OFF = 9216
U_WIDTH = 10752

VMEM_LIMIT = 56 * 1024 * 1024
LANE = 128
SUBLANE = 8


def _cparams(sem):
    return pltpu.CompilerParams(dimension_semantics=sem, vmem_limit_bytes=VMEM_LIMIT)


def _rms_rows(x, w):
    return x * lax.rsqrt(jnp.mean(x * x, axis=-1, keepdims=True) + NORM_EPS) * w


def _ffn_kernel(x_ref, nw_ref, wg_ref, wu_ref, wd_ref, *rest, n_ff, final_norm):
    if final_norm:
        onw_ref, o_ref, hb_ref, acc_ref = rest
    else:
        o_ref, hb_ref, acc_ref = rest
    j = pl.program_id(1)

    @pl.when(j == 0)
    def _():
        hb_ref[...] = _rms_rows(x_ref[...], nw_ref[...]).astype(BF16)
        acc_ref[...] = jnp.zeros_like(acc_ref)

    hb = hb_ref[...]
    g = jnp.dot(hb, wg_ref[...], preferred_element_type=F32)
    u = jnp.dot(hb, wu_ref[...], preferred_element_type=F32)
    a = (g * jax.nn.sigmoid(g) * u).astype(BF16)
    acc_ref[...] += jnp.dot(a, wd_ref[...], preferred_element_type=F32)

    @pl.when(j == n_ff - 1)
    def _():
        y = x_ref[...] + 0.5 * acc_ref[...]
        if final_norm:
            y = _rms_rows(y, onw_ref[...])
        o_ref[...] = y


def _ffn(x, norm_w, w_gu, w_down, out_norm_w=None, *, tm, tf=256):
    n, d = x.shape
    n_ff = D_FF // tf
    final_norm = out_norm_w is not None
    in_specs = [
        pl.BlockSpec((tm, d), lambda i, j: (i, 0)),
        pl.BlockSpec((1, d), lambda i, j: (0, 0)),
        pl.BlockSpec((d, tf), lambda i, j: (0, j)),
        pl.BlockSpec((d, tf), lambda i, j: (0, j + n_ff)),
        pl.BlockSpec((tf, d), lambda i, j: (j, 0)),
    ]
    args = [x, norm_w.reshape(1, d), w_gu, w_gu, w_down]
    if final_norm:
        in_specs.append(pl.BlockSpec((1, d), lambda i, j: (0, 0)))
        args.append(out_norm_w.reshape(1, d))
    return pl.pallas_call(
        functools.partial(_ffn_kernel, n_ff=n_ff, final_norm=final_norm),
        grid=(n // tm, n_ff),
        in_specs=in_specs,
        out_specs=pl.BlockSpec((tm, d), lambda i, j: (i, 0)),
        out_shape=jax.ShapeDtypeStruct((n, d), F32),
        scratch_shapes=[pltpu.VMEM((tm, d), BF16), pltpu.VMEM((tm, d), F32)],
        compiler_params=_cparams(("parallel", "arbitrary")),
        name="ffn_final" if final_norm else "ffn",
    )(*args)


def _inproj_kernel(x_ref, nw_ref, w_ref, wbh_ref, wbl_ref, u_ref, ba_ref, hb_ref):
    j = pl.program_id(1)

    @pl.when(j == 0)
    def _():
        h = _rms_rows(x_ref[...], nw_ref[...])
        hb = h.astype(BF16)
        hl = (h - hb.astype(F32)).astype(BF16)
        hb_ref[...] = hb
        ba_ref[...] = (jnp.dot(hb, wbh_ref[...], preferred_element_type=F32)
                       + jnp.dot(hb, wbl_ref[...], preferred_element_type=F32)
                       + jnp.dot(hl, wbh_ref[...], preferred_element_type=F32))

    u_ref[...] = jnp.dot(hb_ref[...], w_ref[...], preferred_element_type=F32)


def _inproj(x, norm_w, w_main, w_ba_hi, w_ba_lo, *, tm, tn=512):
    n, d = x.shape
    return pl.pallas_call(
        _inproj_kernel,
        grid=(n // tm, U_WIDTH // tn),
        in_specs=[
            pl.BlockSpec((tm, d), lambda i, j: (i, 0)),
            pl.BlockSpec((1, d), lambda i, j: (0, 0)),
            pl.BlockSpec((d, tn), lambda i, j: (0, j)),
            pl.BlockSpec((d, LANE), lambda i, j: (0, 0)),
            pl.BlockSpec((d, LANE), lambda i, j: (0, 0)),
        ],
        out_specs=[
            pl.BlockSpec((tm, tn), lambda i, j: (i, j)),
            pl.BlockSpec((tm, LANE), lambda i, j: (i, 0)),
        ],
        out_shape=[
            jax.ShapeDtypeStruct((n, U_WIDTH), F32),
            jax.ShapeDtypeStruct((n, LANE), F32),
        ],
        scratch_shapes=[pltpu.VMEM((tm, d), BF16)],
        compiler_params=_cparams(("parallel", "arbitrary")),
        name="inproj",
    )(x, norm_w.reshape(1, d), w_main, w_ba_hi, w_ba_lo)


def _rows(start, size, stride):
    return pl.ds(start, size) if stride == 1 else pl.ds(start, size, stride=stride)


def _attn_prompt_kernel(slopes_ref, *refs):
    n_g = len(A_GROUPS)
    ins, o_ref, scr = refs[:5 * n_g], refs[5 * n_g], refs[5 * n_g + 1:]
    st = pl.program_id(1)
    hh = pl.program_id(2)
    scale = A_E ** -0.5
    qi = lax.broadcasted_iota(jnp.int32, (A_BLOCK, 2 * A_BLOCK), 0)
    kj = lax.broadcasted_iota(jnp.int32, (A_BLOCK, 2 * A_BLOCK), 1)
    delta = A_BLOCK + qi - kj
    band = jnp.logical_and(delta >= 0, delta <= A_BLOCK)
    band_first = jnp.logical_and(band, jnp.logical_or(kj >= A_BLOCK, st > 0))
    dn_nt = (((1,), (1,)), ((), ()))
    for g, (win, dil) in enumerate(A_GROUPS):
        q_ref, kc_ref, vc_ref, kp_ref, vp_ref = ins[5 * g:5 * g + 5]
        kf_ref, vf_ref, og_ref, lg_ref = scr[4 * g:4 * g + 4]
        kf_ref[0:win, :] = kp_ref[...]
        kf_ref[win:win + A_ST, :] = kc_ref[...]
        vf_ref[0:win, :] = vp_ref[...]
        vf_ref[win:win + A_ST, :] = vc_ref[...]
        slope = slopes_ref[g * A_HPG + hh]
        bias = -slope * (dil * delta).astype(F32)
        bias_mid = jnp.where(band, bias, -jnp.inf)
        bias_first = jnp.where(band_first, bias, -jnp.inf)
        for r in range(dil):
            for j in range(A_ST // (A_BLOCK * dil)):
                row0 = r + j * A_BLOCK * dil
                q = q_ref[_rows(row0, A_BLOCK, dil), :].astype(BF16)
                k2 = kf_ref[_rows(row0, 2 * A_BLOCK, dil), :].astype(BF16)
                v2 = vf_ref[_rows(row0, 2 * A_BLOCK, dil), :].astype(BF16)
                s = lax.dot_general(q, k2, dn_nt, preferred_element_type=F32) * scale
                s = s + (bias_first if j == 0 else bias_mid)
                m = jnp.max(s, axis=-1, keepdims=True)
                p = jnp.exp(s - m)
                den = jnp.sum(p, axis=-1, keepdims=True)
                acc = jnp.dot(p.astype(BF16), v2, preferred_element_type=F32)
                og_ref[_rows(row0, A_BLOCK, dil), :] = acc / den
                lg_ref[_rows(row0, A_BLOCK, dil), :] = jnp.broadcast_to(m + jnp.log(den), (A_BLOCK, A_E))
    chunk = 2 * A_BLOCK
    for c in range(A_ST // chunk):
        rs = slice(c * chunk, (c + 1) * chunk)
        ls = [scr[4 * g + 3][rs, :] for g in range(n_g)]
        mx = functools.reduce(jnp.maximum, ls)
        ws = [jnp.exp(l - mx) for l in ls]
        num = sum(w * scr[4 * g + 2][rs, :] for g, w in enumerate(ws))
        o_ref[rs, :] = num / sum(ws)


def _attn_prompt(slopes, u, *, batch, seq):
    assert seq % A_ST == 0
    u3 = u.reshape(batch, seq, U_WIDTH)
    in_specs = [pl.BlockSpec(memory_space=pltpu.SMEM)]
    args = [slopes]
    scratch = []
    for g, (win, dil) in enumerate(A_GROUPS):
        assert win == A_BLOCK * dil and A_ST % win == 0
        qcb = (QA_OFF + g * A_GW) // A_E
        kcb = (KV_OFF + g * 2 * A_GW) // A_E
        vcb = kcb + A_HPG
        per = A_ST // win

        def cur(cb):
            return pl.BlockSpec((None, A_ST, A_E), lambda b, st, hh, cb=cb: (b, st, cb + hh))

        def prv(cb, win=win, per=per):
            return pl.BlockSpec((None, win, A_E),
                                lambda b, st, hh, cb=cb: (b, jnp.maximum(st * per - 1, 0), cb + hh))

        in_specs += [cur(qcb), cur(kcb), cur(vcb), prv(kcb), prv(vcb)]
        args += [u3] * 5
        scratch += [pltpu.VMEM((win + A_ST, A_E), F32), pltpu.VMEM((win + A_ST, A_E), F32),
                    pltpu.VMEM((A_ST, A_E), F32), pltpu.VMEM((A_ST, A_E), F32)]
    return pl.pallas_call(
        _attn_prompt_kernel,
        grid=(batch, seq // A_ST, A_HPG),
        in_specs=in_specs,
        out_specs=pl.BlockSpec((None, A_ST, A_E), lambda b, st, hh: (b, st, hh)),
        out_shape=jax.ShapeDtypeStruct((batch, seq, A_GW), F32),
        scratch_shapes=scratch,
        compiler_params=_cparams(("parallel", "parallel", "arbitrary")),
        name="attn_prompt",
    )(*args)


def _pick_rows(rows):
    sub = lax.broadcasted_iota(jnp.int32, (SUBLANE, A_E), 0)
    out = jnp.zeros((SUBLANE, A_E), F32)
    for i, r in enumerate(rows):
        out = jnp.where(sub == i, jnp.broadcast_to(r, (SUBLANE, A_E)), out)
    return out


def _attn_sample_kernel(slopes_ref, q_ref, kv_ref, c0_ref, c1_ref, c2_ref, o_ref, *, dec_seq, nseq):
    scale = A_E ** -0.5
    caches = (c0_ref, c1_ref, c2_ref)
    n_keys = A_BLOCK * SUBLANE
    sub = lax.broadcasted_iota(jnp.int32, (SUBLANE, n_keys), 0)
    lane = lax.broadcasted_iota(jnp.int32, (SUBLANE, n_keys), 1)
    tile_row = lane & (SUBLANE - 1)
    m_idx = lane >> 3
    own_k = tile_row == sub
    sub1 = lax.broadcasted_iota(jnp.int32, (SUBLANE, 1), 0)
    dn_nt = (((1,), (1,)), ((), ()))
    out_tiles = [jnp.zeros((nseq * dec_seq, A_E), F32) for _ in range(A_HPG)]
    out_sub = lax.broadcasted_iota(jnp.int32, (nseq * dec_seq, A_E), 0)
    for bb in range(nseq):
        for s in range(dec_seq):
            row = bb * dec_seq + s
            outs, lses = [], []
            for g, (win, dil) in enumerate(A_GROUPS):
                slope8 = jnp.zeros((SUBLANE, 1), F32)
                for h in range(A_HPG):
                    slope8 = jnp.where(sub1 == h, slopes_ref[g * A_HPG + h], slope8)
                q8 = _pick_rows([q_ref[row:row + 1, g * A_GW + h * A_E:g * A_GW + (h + 1) * A_E]
                                 for h in range(A_HPG)])
                res = 0 if dil == 1 else s
                xb = caches[g][bb, :, res, :, :].reshape(n_keys, A_E).astype(BF16)
                sc = lax.dot_general(q8.astype(BF16), xb, dn_nt, preferred_element_type=F32) * scale
                if dil == 1:
                    dist = (win + s - m_idx).astype(F32)
                    valid = jnp.logical_and(own_k, m_idx >= s)
                    new_rows = [(bb * dec_seq + t, float(s - t)) for t in range(s + 1)]
                else:
                    dist = (dil * (win // dil - m_idx)).astype(F32)
                    valid = own_k
                    new_rows = [(row, 0.0)]
                sc = jnp.where(valid, sc - slope8 * dist, -jnp.inf)
                m = jnp.max(sc, axis=-1, keepdims=True)
                koff = g * 2 * A_GW
                new = []
                for nrow, ndist in new_rows:
                    k8 = _pick_rows([kv_ref[nrow:nrow + 1, koff + h * A_E:koff + (h + 1) * A_E]
                                     for h in range(A_HPG)])
                    v8 = _pick_rows([kv_ref[nrow:nrow + 1, koff + A_GW + h * A_E:koff + A_GW + (h + 1) * A_E]
                                     for h in range(A_HPG)])
                    s_n = jnp.sum(k8 * q8, axis=-1, keepdims=True) * scale - slope8 * ndist
                    m = jnp.maximum(m, s_n)
                    new.append((s_n, v8))
                p = jnp.exp(sc - m)
                den = jnp.sum(p, axis=-1, keepdims=True)
                p_v = pltpu.roll(p, A_HPG, 1)
                acc = jnp.dot(p_v.astype(BF16), xb, preferred_element_type=F32)
                for s_n, v8 in new:
                    p_n = jnp.exp(s_n - m)
                    den = den + p_n
                    acc = acc + p_n * v8
                outs.append(acc / den)
                lses.append(m + jnp.log(den))
            mx = functools.reduce(jnp.maximum, lses)
            ws = [jnp.exp(l - mx) for l in lses]
            o8 = sum(w * o for w, o in zip(ws, outs)) / sum(ws)
            for h in range(A_HPG):
                out_tiles[h] = jnp.where(out_sub == row,
                                         jnp.broadcast_to(o8[h:h + 1, :], (nseq * dec_seq, A_E)), out_tiles[h])
    for h in range(A_HPG):
        o_ref[:, h * A_E:(h + 1) * A_E] = out_tiles[h]


def _attn_sample(slopes, u, caches, *, batch, dec_seq):
    nseq = SUBLANE // dec_seq
    assert nseq * dec_seq == SUBLANE and batch % nseq == 0
    views, specs = [], []
    for (win, dil), cache in zip(A_GROUPS, caches):
        wb = cache.shape[1]
        assert wb == win and wb // dil == A_BLOCK and (dil == 1 or dec_seq <= dil)
        views.append(cache.reshape(batch, wb // dil, dil, SUBLANE, A_E))
        res = 1 if dil == 1 else dec_seq
        specs.append(pl.BlockSpec((nseq, A_BLOCK, res, SUBLANE, A_E), lambda i: (i, 0, 0, 0, 0)))
    n_q = len(A_GROUPS) * A_GW
    rows = nseq * dec_seq
    return pl.pallas_call(
        functools.partial(_attn_sample_kernel, dec_seq=dec_seq, nseq=nseq),
        grid=(batch // nseq,),
        in_specs=[
            pl.BlockSpec(memory_space=pltpu.SMEM),
            pl.BlockSpec((rows, n_q), lambda i: (i, QA_OFF // n_q)),
            pl.BlockSpec((rows, 2 * n_q), lambda i: (i, KV_OFF // (2 * n_q))),
        ] + specs,
        out_specs=pl.BlockSpec((rows, A_GW), lambda i: (i, 0)),
        out_shape=jax.ShapeDtypeStruct((batch * dec_seq, A_GW), F32),
        compiler_params=_cparams(("parallel",)),
        name="attn_sample",
    )(slopes, u, u, *views)


def _mm(a, b):
    return jnp.dot(a, b, precision=HIGHEST, preferred_element_type=F32)


def _gdn_kernel(qkv_ref, z_ref, ba_ref, cw_ref, alog_ref, dtb_ref, nw_ref, conv0_ref, s0_ref,
                o_ref, sout_ref, cb_ref, s_ref, *, tb, cc, nc):
    n = pl.program_id(1)

    @pl.when(n == 0)
    def _():
        s_ref[...] = s0_ref[...]
        cb_ref[0:SUBLANE, :] = conv0_ref[...]

    x = qkv_ref[...]
    cb_ref[SUBLANE:SUBLANE + tb, :] = x
    if tb < cc:
        cb_ref[SUBLANE + tb:SUBLANE + cc, :] = jnp.zeros((cc - tb, G_CONV_CH), F32)
    cw = cw_ref[...]
    first = SUBLANE - (G_CONV - 1)
    y = cb_ref[first:first + cc, :] * cw[0:1, :]
    for j in range(1, G_CONV):
        y = y + cb_ref[first + j:first + j + cc, :] * cw[j:j + 1, :]
    if nc > 1:
        cb_ref[0:SUBLANE, :] = x[tb - SUBLANE:tb, :]
    y = y * jax.nn.sigmoid(y)

    rows = lax.broadcasted_iota(jnp.int32, (cc, 1), 0)
    live = rows < tb
    ba = ba_ref[...]
    if tb < cc:
        ba = jnp.concatenate([ba, jnp.zeros((cc - tb, LANE), F32)], axis=0)
    beta_t = jnp.where(live, jax.nn.sigmoid(ba), 0.0)
    xs = ba + dtb_ref[...]
    softplus = jnp.maximum(xs, 0.0) + jnp.log(1.0 + jnp.exp(-jnp.abs(xs)))
    g_t = jnp.where(live, -jnp.exp(alog_ref[...]) * softplus, 0.0)

    ii = lax.broadcasted_iota(jnp.int32, (cc, cc), 0)
    jj = lax.broadcasted_iota(jnp.int32, (cc, cc), 1)
    incl = ii >= jj
    strict = ii > jj
    eye = ii == jj
    ones = jnp.ones((cc, cc), F32)
    gc = _mm(incl.astype(F32), g_t)
    gc_last = gc[cc - 1:cc, :]
    e_gc = jnp.exp(gc)
    e_rest = jnp.exp(gc_last - gc)
    e_end = jnp.exp(gc_last)

    dn_nt = (((1,), (1,)), ((), ()))
    dn_tn = (((0,), (0,)), ((), ()))
    for h in range(G_HEADS):
        sl = slice(h * G_DK, (h + 1) * G_DK)
        q = y[:, sl]
        k = y[:, G_QK + h * G_DK:G_QK + (h + 1) * G_DK]
        v = y[:, 2 * G_QK + h * G_DV:2 * G_QK + (h + 1) * G_DV]
        q = q * lax.rsqrt(jnp.sum(q * q, axis=-1, keepdims=True) + NORM_EPS) * (G_DK ** -0.5)
        k = k * lax.rsqrt(jnp.sum(k * k, axis=-1, keepdims=True) + NORM_EPS)
        if tb < cc:
            q = jnp.where(live, q, 0.0)
            k = jnp.where(live, k, 0.0)
            v = jnp.where(live, v, 0.0)
        beta = beta_t[:, h:h + 1]
        gc_h = gc[:, G_HEADS + h:G_HEADS + h + 1]
        gc_row = _mm(ones, jnp.where(eye, gc_h, 0.0))
        decay = jnp.exp(jnp.where(incl, gc_h - gc_row, -jnp.inf))
        kb = k * beta
        a_low = jnp.where(strict, lax.dot_general(kb, k, dn_nt, precision=HIGHEST,
                                                  preferred_element_type=F32) * decay, 0.0)
        rhs = jnp.concatenate([v * beta, kb * e_gc[:, G_HEADS + h:G_HEADS + h + 1]], axis=1)
        rhs = rhs - _mm(a_low, rhs)
        pw = a_low
        span = 2
        while span < cc:
            pw = _mm(pw, pw)
            rhs = rhs + _mm(pw, rhs)
            span *= 2
        u_h = rhs[:, :G_DV]
        w_h = rhs[:, G_DV:]
        qk = lax.dot_general(q, k, dn_nt, precision=HIGHEST, preferred_element_type=F32) * decay
        qg = q * e_gc[:, G_HEADS + h:G_HEADS + h + 1]
        kd = k * e_rest[:, G_HEADS + h:G_HEADS + h + 1]
        s_h = s_ref[h]
        ws = _mm(jnp.concatenate([w_h, qg], axis=0), s_h)
        v_new = u_h - ws[:cc]
        o_h = ws[cc:] + _mm(qk, v_new)
        s_ref[h] = (s_h * e_end[:, G_HEADS + h:G_HEADS + h + 1]
                    + lax.dot_general(kd, v_new, dn_tn, precision=HIGHEST, preferred_element_type=F32))
        o_h = o_h * lax.rsqrt(jnp.mean(o_h * o_h, axis=-1, keepdims=True) + NORM_EPS) * nw_ref[...]
        zz = z_ref[:, sl]
        o_ref[:, sl] = o_h[:tb] * (zz * jax.nn.sigmoid(zz))

    @pl.when(n == nc - 1)
    def _():
        sout_ref[...] = s_ref[...]


def _gdn(u, ba, conv0, s0, conv_w, alog_t, dtb_t, norm_w, *, batch, seq):
    cc = G_CHUNK if seq >= G_CHUNK else SUBLANE
    tb = min(seq, cc)
    nc = seq // tb
    assert nc * tb == seq and (nc == 1 or tb == cc)
    uv = u.reshape(batch, seq, U_WIDTH)
    bav = ba.reshape(batch, seq, LANE)
    conv0p = jnp.concatenate(
        [jnp.zeros((batch, SUBLANE - (G_CONV - 1), G_CONV_CH), F32), conv0.astype(F32)], axis=1)
    const2 = lambda b, n: (0, 0)
    return pl.pallas_call(
        functools.partial(_gdn_kernel, tb=tb, cc=cc, nc=nc),
        grid=(batch, nc),
        in_specs=[
            pl.BlockSpec((None, tb, G_CONV_CH), lambda b, n: (b, n, QKVB_OFF // G_CONV_CH)),
            pl.BlockSpec((None, tb, G_QK), lambda b, n: (b, n, Z_OFF // G_QK)),
            pl.BlockSpec((None, tb, LANE), lambda b, n: (b, n, 0)),
            pl.BlockSpec((G_CONV, G_CONV_CH), const2),
            pl.BlockSpec((1, LANE), const2),
            pl.BlockSpec((1, LANE), const2),
            pl.BlockSpec((1, G_DV), const2),
            pl.BlockSpec((None, SUBLANE, G_CONV_CH), lambda b, n: (b, 0, 0)),
            pl.BlockSpec((None, G_HEADS, G_DK, G_DV), lambda b, n: (b, 0, 0, 0)),
        ],
        out_specs=[
            pl.BlockSpec((None, tb, G_QK), lambda b, n: (b, n, 0)),
            pl.BlockSpec((None, G_HEADS, G_DK, G_DV), lambda b, n: (b, 0, 0, 0)),
        ],
        out_shape=[
            jax.ShapeDtypeStruct((batch, seq, G_QK), F32),
            jax.ShapeDtypeStruct((batch, G_HEADS, G_DK, G_DV), F32),
        ],
        scratch_shapes=[
            pltpu.VMEM((SUBLANE + cc, G_CONV_CH), F32),
            pltpu.VMEM((G_HEADS, G_DK, G_DV), F32),
        ],
        compiler_params=_cparams(("parallel", "arbitrary")),
        name="gdn",
    )(uv, uv, bav, conv_w, alog_t, dtb_t, norm_w.reshape(1, G_DV), conv0p, s0)


def _mix_kernel(x_ref, oa_ref, ob_ref, ga_ref, gb_ref, wa_ref, wb_ref, wo_ref, o_ref):
    pa = jnp.dot(oa_ref[...].astype(BF16), wa_ref[...], preferred_element_type=F32)
    pb = jnp.dot(ob_ref[...].astype(BF16), wb_ref[...], preferred_element_type=F32)
    merged = jax.nn.sigmoid(ga_ref[...]) * pa + jax.nn.sigmoid(gb_ref[...]) * pb
    o_ref[...] = x_ref[...] + jnp.dot(merged.astype(BF16), wo_ref[...], preferred_element_type=F32)


def _mix(x, oa, ob, u, wa, wb, wo, *, tm):
    n, d = x.shape
    const = lambda i: (0, 0)
    return pl.pallas_call(
        _mix_kernel,
        grid=(n // tm,),
        in_specs=[
            pl.BlockSpec((tm, d), lambda i: (i, 0)),
            pl.BlockSpec((tm, A_GW), lambda i: (i, 0)),
            pl.BlockSpec((tm, G_QK), lambda i: (i, 0)),
            pl.BlockSpec((tm, d), lambda i: (i, GATE_OFF // d)),
            pl.BlockSpec((tm, d), lambda i: (i, GATE_OFF // d + 1)),
            pl.BlockSpec((A_GW, d), const),
            pl.BlockSpec((G_QK, d), const),
            pl.BlockSpec((d, d), const),
        ],
        out_specs=pl.BlockSpec((tm, d), lambda i: (i, 0)),
        out_shape=jax.ShapeDtypeStruct((n, d), F32),
        compiler_params=_cparams(("parallel",)),
        name="mix",
    )(x, oa, ob, u, u, wa, wb, wo)


def _prep_weights(w_in, gdn_a_log, gdn_dt_bias):
    aw = len(A_GROUPS) * A_GW
    qa, ka, va = w_in[:, 0:aw], w_in[:, aw:2 * aw], w_in[:, 2 * aw:3 * aw]
    off = 3 * aw
    qkvb = w_in[:, off:off + G_CONV_CH]
    off += G_CONV_CH
    zb = w_in[:, off:off + G_QK]
    off += G_QK
    ba = w_in[:, off:off + 2 * G_HEADS]
    off += 2 * G_HEADS
    gate = w_in[:, off:]
    kv = []
    for g in range(len(A_GROUPS)):
        kv += [ka[:, g * A_GW:(g + 1) * A_GW], va[:, g * A_GW:(g + 1) * A_GW]]
    w_main = jnp.concatenate([qkvb, zb, gate] + kv + [qa], axis=1).astype(BF16)
    assert w_main.shape[1] == U_WIDTH
    ba_pad = jnp.pad(ba, ((0, 0), (0, LANE - 2 * G_HEADS)))
    ba_hi = ba_pad.astype(BF16)
    ba_lo = (ba_pad - ba_hi.astype(F32)).astype(BF16)
    pad = (G_HEADS, LANE - 2 * G_HEADS)
    alog_t = jnp.pad(gdn_a_log.astype(F32), pad).reshape(1, LANE)
    dtb_t = jnp.pad(gdn_dt_bias.astype(F32), pad).reshape(1, LANE)
    return w_main, ba_hi, ba_lo, alog_t, dtb_t


def _kv_rows(u3, group, keep):
    b, t, _ = u3.shape
    off = KV_OFF + group * 2 * A_GW
    return u3[:, t - keep:, off:off + 2 * A_GW].reshape(1, b, keep, 2, A_HPG, A_E)


def _layer(x, batch, seq, conv0, s0, attn_fn, p, *, tm):
    tm = min(tm, x.shape[0])
    x1 = _ffn(x, p["norm_ffn1"], p["w_ffn1_gu"], p["w_ffn1_down"], tm=tm)
    u, ba = _inproj(x1, p["norm_mix"], p["w_main"], p["ba_hi"], p["ba_lo"], tm=tm)
    oa = attn_fn(u)
    ob, s_new = _gdn(u, ba, conv0, s0, p["conv_w"], p["alog_t"], p["dtb_t"], p["gdn_norm"],
                     batch=batch, seq=seq)
    x2 = _mix(x1, oa.reshape(batch * seq, A_GW), ob.reshape(batch * seq, G_QK), u,
              p["w_proj_a"], p["w_proj_b"], p["w_out"], tm=min(tm, 512))
    y = _ffn(x2, p["norm_ffn2"], p["w_ffn2_gu"], p["w_ffn2_down"], p["norm_out"], tm=tm)
    u3 = u.reshape(batch, seq, U_WIDTH)
    rows = [_kv_rows(u3, g, min(win, seq)) for g, (win, _) in enumerate(A_GROUPS)]
    conv_new = u3[:, seq - (G_CONV - 1):, QKVB_OFF:QKVB_OFF + G_CONV_CH][None]
    return y.reshape(batch, seq, D_MODEL), rows, conv_new, s_new[None]


def kernel(x_prompt, x_sample, cache_kv_w128, cache_kv_w512, cache_kv_w2048, state_conv, state_ssm,
           norm_ffn1, w_ffn1_gu, w_ffn1_down, norm_mix, w_in, conv_w, gdn_a_log, gdn_dt_bias, gdn_norm,
           w_proj_a, w_proj_b, w_out, norm_ffn2, w_ffn2_gu, w_ffn2_down, norm_out):
    assert w_in.shape[0] == 1, "single layer"
    n_heads = len(A_GROUPS) * A_HPG
    slopes = jnp.exp2(-8.0 * jnp.arange(1, n_heads + 1, dtype=F32) / n_heads)
    w_main, ba_hi, ba_lo, alog_t, dtb_t = _prep_weights(w_in[0], gdn_a_log[0], gdn_dt_bias[0])
    p = dict(
        norm_ffn1=norm_ffn1[0], w_ffn1_gu=w_ffn1_gu[0].astype(BF16), w_ffn1_down=w_ffn1_down[0].astype(BF16),
        norm_mix=norm_mix[0], w_main=w_main, ba_hi=ba_hi, ba_lo=ba_lo, conv_w=conv_w[0],
        alog_t=alog_t, dtb_t=dtb_t, gdn_norm=gdn_norm[0],
        w_proj_a=w_proj_a[0].astype(BF16), w_proj_b=w_proj_b[0].astype(BF16), w_out=w_out[0].astype(BF16),
        norm_ffn2=norm_ffn2[0], w_ffn2_gu=w_ffn2_gu[0].astype(BF16), w_ffn2_down=w_ffn2_down[0].astype(BF16),
        norm_out=norm_out,
    )
    bp, tp, d = x_prompt.shape
    bs, ts, _ = x_sample.shape

    conv0_p = jnp.zeros((bp, G_CONV - 1, G_CONV_CH), F32)
    ssm0_p = jnp.zeros((bp, G_HEADS, G_DK, G_DV), F32)
    yp, rows_p, conv_p, ssm_p = _layer(
        x_prompt.reshape(bp * tp, d), bp, tp, conv0_p, ssm0_p,
        functools.partial(_attn_prompt, slopes, batch=bp, seq=tp), p, tm=1024)

    caches = (cache_kv_w128[0], cache_kv_w512[0], cache_kv_w2048[0])
    ys, rows_s, conv_s, ssm_s = _layer(
        x_sample.reshape(bs * ts, d), bs, ts, state_conv[0], state_ssm[0],
        lambda u: _attn_sample(slopes, u, caches, batch=bs, dec_seq=ts), p, tm=512)

    return (yp, ys, rows_p[0], rows_p[1], rows_p[2], conv_p, ssm_p,
            rows_s[0], rows_s[1], rows_s[2], conv_s, ssm_s)
```

```python
import functools

import jax
import jax.numpy as jnp
from jax import lax
from jax.experimental import pallas as pl
from jax.experimental.pallas import tpu as pltpu

F32 = jnp.float32
BF16 = jnp.bfloat16
HIGHEST = lax.Precision.HIGHEST

D_MODEL = 1024
D_FF = 2816
NORM_EPS = 1e-6
A_GROUPS = ((128, 1), (512, 4), (2048, 16))
A_HPG = 4
A_E = 128
A_BLOCK = 128
A_GW = A_HPG * A_E
A_ST = 2048
G_HEADS = 8
G_DK = 128
G_DV = 128
G_QK = G_HEADS * G_DK
G_CONV_CH = 3 * G_QK
G_CONV = 4
G_CHUNK = 64

QKVB_OFF = 0
Z_OFF = 3072
GATE_OFF = 4096
KV_OFF = 6144
QA_OFF = 9216
U_WIDTH = 10752

VMEM_LIMIT = 56 * 1024 * 1024
LANE = 128
SUBLANE = 8


def _cparams(sem):
    return pltpu.CompilerParams(dimension_semantics=sem, vmem_limit_bytes=VMEM_LIMIT)


def _rms_rows(x, w):
    return x * lax.rsqrt(jnp.mean(x * x, axis=-1, keepdims=True) + NORM_EPS) * w


def _ffn_kernel(x_ref, nw_ref, wg_ref, wu_ref, wd_ref, *rest, n_ff, final_norm):
    if final_norm:
        onw_ref, o_ref, hb_ref, acc_ref = rest
    else:
        o_ref, hb_ref, acc_ref = rest
    j = pl.program_id(1)

    @pl.when(j == 0)
    def _():
        hb_ref[...] = _rms_rows(x_ref[...], nw_ref[...]).astype(BF16)
        acc_ref[...] = jnp.zeros_like(acc_ref)

    hb = hb_ref[...]
    g = jnp.dot(hb, wg_ref[...], preferred_element_type=F32)
    u = jnp.dot(hb, wu_ref[...], preferred_element_type=F32)
    a = (g * jax.nn.sigmoid(g) * u).astype(BF16)
    acc_ref[...] += jnp.dot(a, wd_ref[...], preferred_element_type=F32)

    @pl.when(j == n_ff - 1)
    def _():
        y = x_ref[...] + 0.5 * acc_ref[...]
        if final_norm:
            y = _rms_rows(y, onw_ref[...])
        o_ref[...] = y


def _ffn(x, norm_w, w_gu, w_down, out_norm_w=None, *, tm, tf=256):
    n, d = x.shape
    n_ff = D_FF // tf
    final_norm = out_norm_w is not None
    in_specs = [
        pl.BlockSpec((tm, d), lambda i, j: (i, 0)),
        pl.BlockSpec((1, d), lambda i, j: (0, 0)),
        pl.BlockSpec((d, tf), lambda i, j: (0, j)),
        pl.BlockSpec((d, tf), lambda i, j: (0, j + n_ff)),
        pl.BlockSpec((tf, d), lambda i, j: (j, 0)),
    ]
    args = [x, norm_w.reshape(1, d), w_gu, w_gu, w_down]
    if final_norm:
        in_specs.append(pl.BlockSpec((1, d), lambda i, j: (0, 0)))
        args.append(out_norm_w.reshape(1, d))
    return pl.pallas_call(
        functools.partial(_ffn_kernel, n_ff=n_ff, final_norm=final_norm),
        grid=(n // tm, n_ff),
        in_specs=in_specs,
        out_specs=pl.BlockSpec((tm, d), lambda i, j: (i, 0)),
        out_shape=jax.ShapeDtypeStruct((n, d), F32),
        scratch_shapes=[pltpu.VMEM((tm, d), BF16), pltpu.VMEM((tm, d), F32)],
        compiler_params=_cparams(("parallel", "arbitrary")),
        name="ffn_final" if final_norm else "ffn",
    )(*args)


def _inproj_kernel(x_ref, nw_ref, w_ref, wbh_ref, wbl_ref, u_ref, ba_ref, hb_ref):
    j = pl.program_id(1)

    @pl.when(j == 0)
    def _():
        h = _rms_rows(x_ref[...], nw_ref[...])
        hb = h.astype(BF16)
        hl = (h - hb.astype(F32)).astype(BF16)
        hb_ref[...] = hb
        ba_ref[...] = (jnp.dot(hb, wbh_ref[...], preferred_element_type=F32)
                       + jnp.dot(hb, wbl_ref[...], preferred_element_type=F32)
                       + jnp.dot(hl, wbh_ref[...], preferred_element_type=F32))

    u_ref[...] = jnp.dot(hb_ref[...], w_ref[...], preferred_element_type=F32)


def _inproj(x, norm_w, w_main, w_ba_hi, w_ba_lo, *, tm, tn=512):
    n, d = x.shape
    return pl.pallas_call(
        _inproj_kernel,
        grid=(n // tm, U_WIDTH // tn),
        in_specs=[
            pl.BlockSpec((tm, d), lambda i, j: (i, 0)),
            pl.BlockSpec((1, d), lambda i, j: (0, 0)),
            pl.BlockSpec((d, tn), lambda i, j: (0, j)),
            pl.BlockSpec((d, LANE), lambda i, j: (0, 0)),
            pl.BlockSpec((d, LANE), lambda i, j: (0, 0)),
        ],
        out_specs=[
            pl.BlockSpec((tm, tn), lambda i, j: (i, j)),
            pl.BlockSpec((tm, LANE), lambda i, j: (i, 0)),
        ],
        out_shape=[
            jax.ShapeDtypeStruct((n, U_WIDTH), F32),
            jax.ShapeDtypeStruct((n, LANE), F32),
        ],
        scratch_shapes=[pltpu.VMEM((tm, d), BF16)],
        compiler_params=_cparams(("parallel", "arbitrary")),
        name="inproj",
    )(x, norm_w.reshape(1, d), w_main, w_ba_hi, w_ba_lo)


def _rows(start, size, stride):
    return pl.ds(start, size) if stride == 1 else pl.ds(start, size, stride=stride)


def _attn_prompt_kernel(slopes_ref, *refs):
    n_g = len(A_GROUPS)
    ins, o_ref, scr = refs[:5 * n_g], refs[5 * n_g], refs[5 * n_g + 1:]
    st = pl.program_id(1)
    hh = pl.program_id(2)
    scale = A_E ** -0.5
    qi = lax.broadcasted_iota(jnp.int32, (A_BLOCK, 2 * A_BLOCK), 0)
    kj = lax.broadcasted_iota(jnp.int32, (A_BLOCK, 2 * A_BLOCK), 1)
    delta = A_BLOCK + qi - kj
    band = jnp.logical_and(delta >= 0, delta <= A_BLOCK)
    band_first = jnp.logical_and(band, jnp.logical_or(kj >= A_BLOCK, st > 0))
    dn_nt = (((1,), (1,)), ((), ()))
    for g, (win, dil) in enumerate(A_GROUPS):
        q_ref, kc_ref, vc_ref, kp_ref, vp_ref = ins[5 * g:5 * g + 5]
        kf_ref, vf_ref, og_ref, lg_ref = scr[4 * g:4 * g + 4]
        kf_ref[0:win, :] = kp_ref[...]
        kf_ref[win:win + A_ST, :] = kc_ref[...]
        vf_ref[0:win, :] = vp_ref[...]
        vf_ref[win:win + A_ST, :] = vc_ref[...]
        slope = slopes_ref[g * A_HPG + hh]
        bias = -slope * (dil * delta).astype(F32)
        bias_mid = jnp.where(band, bias, -jnp.inf)
        bias_first = jnp.where(band_first, bias, -jnp.inf)
        for r in range(dil):
            for j in range(A_ST // (A_BLOCK * dil)):
                row0 = r + j * A_BLOCK * dil
                q = q_ref[_rows(row0, A_BLOCK, dil), :].astype(BF16)
                k2 = kf_ref[_rows(row0, 2 * A_BLOCK, dil), :].astype(BF16)
                v2 = vf_ref[_rows(row0, 2 * A_BLOCK, dil), :].astype(BF16)
                s = lax.dot_general(q, k2, dn_nt, preferred_element_type=F32) * scale
                s = s + (bias_first if j == 0 else bias_mid)
                m = jnp.max(s, axis=-1, keepdims=True)
                p = jnp.exp(s - m)
                den = jnp.sum(p, axis=-1, keepdims=True)
                acc = jnp.dot(p.astype(BF16), v2, preferred_element_type=F32)
                og_ref[_rows(row0, A_BLOCK, dil), :] = acc / den
                lg_ref[_rows(row0, A_BLOCK, dil), :] = jnp.broadcast_to(m + jnp.log(den), (A_BLOCK, A_E))
    chunk = 2 * A_BLOCK
    for c in range(A_ST // chunk):
        rs = slice(c * chunk, (c + 1) * chunk)
        ls = [scr[4 * g + 3][rs, :] for g in range(n_g)]
        mx = functools.reduce(jnp.maximum, ls)
        ws = [jnp.exp(l - mx) for l in ls]
        num = sum(w * scr[4 * g + 2][rs, :] for g, w in enumerate(ws))
        o_ref[rs, :] = num / sum(ws)


def _attn_prompt(slopes, u, *, batch, seq):
    assert seq % A_ST == 0
    u3 = u.reshape(batch, seq, U_WIDTH)
    in_specs = [pl.BlockSpec(memory_space=pltpu.SMEM)]
    args = [slopes]
    scratch = []
    for g, (win, dil) in enumerate(A_GROUPS):
        assert win == A_BLOCK * dil and A_ST % win == 0
        qcb = (QA_OFF + g * A_GW) // A_E
        kcb = (KV_OFF + g * 2 * A_GW) // A_E
        vcb = kcb + A_HPG
        per = A_ST // win

        def cur(cb):
            return pl.BlockSpec((None, A_ST, A_E), lambda b, st, hh, cb=cb: (b, st, cb + hh))

        def prv(cb, win=win, per=per):
            return pl.BlockSpec((None, win, A_E),
                                lambda b, st, hh, cb=cb: (b, jnp.maximum(st * per - 1, 0), cb + hh))

        in_specs += [cur(qcb), cur(kcb), cur(vcb), prv(kcb), prv(vcb)]
        args += [u3] * 5
        scratch += [pltpu.VMEM((win + A_ST, A_E), F32), pltpu.VMEM((win + A_ST, A_E), F32),
                    pltpu.VMEM((A_ST, A_E), F32), pltpu.VMEM((A_ST, A_E), F32)]
    return pl.pallas_call(
        _attn_prompt_kernel,
        grid=(batch, seq // A_ST, A_HPG),
        in_specs=in_specs,
        out_specs=pl.BlockSpec((None, A_ST, A_E), lambda b, st, hh: (b, st, hh)),
        out_shape=jax.ShapeDtypeStruct((batch, seq, A_GW), F32),
        scratch_shapes=scratch,
        compiler_params=_cparams(("parallel", "parallel", "arbitrary")),
        name="attn_prompt",
    )(*args)


def _pick_rows(rows):
    sub = lax.broadcasted_iota(jnp.int32, (SUBLANE, A_E), 0)
    out = jnp.zeros((SUBLANE, A_E), F32)
    for i, r in enumerate(rows):
        out = jnp.where(sub == i, jnp.broadcast_to(r, (SUBLANE, A_E)), out)
    return out


def _attn_sample_kernel(slopes_ref, q_ref, kv_ref, c0_ref, c1_ref, c2_ref, o_ref, *, dec_seq, nseq):
    scale = A_E ** -0.5
    caches = (c0_ref, c1_ref, c2_ref)
    n_keys = A_BLOCK * SUBLANE
    sub = lax.broadcasted_iota(jnp.int32, (SUBLANE, n_keys), 0)
    lane = lax.broadcasted_iota(jnp.int32, (SUBLANE, n_keys), 1)
    tile_row = lane & (SUBLANE - 1)
    m_idx = lane >> 3
    own_k = tile_row == sub
    sub1 = lax.broadcasted_iota(jnp.int32, (SUBLANE, 1), 0)
    dn_nt = (((1,), (1,)), ((), ()))
    slope8s = []
    for g in range(len(A_GROUPS)):
        slope8 = jnp.zeros((SUBLANE, 1), F32)
        for h in range(A_HPG):
            slope8 = jnp.where(sub1 == h, slopes_ref[g * A_HPG + h], slope8)
        slope8s.append(slope8)

    def keys(bb, s, g):
        res = 0 if A_GROUPS[g][1] == 1 else s
        return caches[g][bb, :, res, :, :].reshape(n_keys, A_E).astype(BF16)

    units = [(bb, s, g) for bb in range(nseq) for s in range(dec_seq) for g in range(len(A_GROUPS))]
    q8s = [_pick_rows([q_ref[bb * dec_seq + s:bb * dec_seq + s + 1, g * A_GW + h * A_E:g * A_GW + (h + 1) * A_E]
                       for h in range(A_HPG)]) for bb, s, g in units]
    scs = [lax.dot_general(q8.astype(BF16), keys(*u), dn_nt, preferred_element_type=F32) * scale
           for q8, u in zip(q8s, units)]
    ms, dens, pvs, news = [], [], [], []
    for (bb, s, g), q8, sc in zip(units, q8s, scs):
        win, dil = A_GROUPS[g]
        row = bb * dec_seq + s
        slope8 = slope8s[g]
        if dil == 1:
            dist = (win + s - m_idx).astype(F32)
            valid = jnp.logical_and(own_k, m_idx >= s)
            new_rows = [(bb * dec_seq + t, float(s - t)) for t in range(s + 1)]
        else:
            dist = (dil * (win // dil - m_idx)).astype(F32)
            valid = own_k
            new_rows = [(row, 0.0)]
        sc = jnp.where(valid, sc - slope8 * dist, -jnp.inf)
        m = jnp.max(sc, axis=-1, keepdims=True)
        koff = g * 2 * A_GW
        new = []
        for nrow, ndist in new_rows:
            k8 = _pick_rows([kv_ref[nrow:nrow + 1, koff + h * A_E:koff + (h + 1) * A_E]
                             for h in range(A_HPG)])
            v8 = _pick_rows([kv_ref[nrow:nrow + 1, koff + A_GW + h * A_E:koff + A_GW + (h + 1) * A_E]
                             for h in range(A_HPG)])
            s_n = jnp.sum(k8 * q8, axis=-1, keepdims=True) * scale - slope8 * ndist
            m = jnp.maximum(m, s_n)
            new.append((s_n, v8))
        p = jnp.exp(sc - m)
        ms.append(m)
        dens.append(jnp.sum(p, axis=-1, keepdims=True))
        pvs.append(pltpu.roll(p, A_HPG, 1).astype(BF16))
        news.append(new)
    accs = [jnp.dot(pv, keys(*u), preferred_element_type=F32) for pv, u in zip(pvs, units)]

    out_tiles = [jnp.zeros((nseq * dec_seq, A_E), F32) for _ in range(A_HPG)]
    out_sub = lax.broadcasted_iota(jnp.int32, (nseq * dec_seq, A_E), 0)
    n_g = len(A_GROUPS)
    for ui in range(0, len(units), n_g):
        bb, s, _ = units[ui]
        outs, lses = [], []
        for m, den, acc, new in zip(ms[ui:ui + n_g], dens[ui:ui + n_g], accs[ui:ui + n_g], news[ui:ui + n_g]):
            for s_n, v8 in new:
                p_n = jnp.exp(s_n - m)
                den = den + p_n
                acc = acc + p_n * v8
            outs.append(acc / den)
            lses.append(m + jnp.log(den))
        mx = functools.reduce(jnp.maximum, lses)
        ws = [jnp.exp(l - mx) for l in lses]
        o8 = sum(w * o for w, o in zip(ws, outs)) / sum(ws)
        for h in range(A_HPG):
            out_tiles[h] = jnp.where(out_sub == bb * dec_seq + s,
                                     jnp.broadcast_to(o8[h:h + 1, :], (nseq * dec_seq, A_E)), out_tiles[h])
    for h in range(A_HPG):
        o_ref[:, h * A_E:(h + 1) * A_E] = out_tiles[h]


def _attn_sample(slopes, u, caches, *, batch, dec_seq):
    nseq = SUBLANE // dec_seq
    assert nseq * dec_seq == SUBLANE and batch % nseq == 0
    views, specs = [], []
    for (win, dil), cache in zip(A_GROUPS, caches):
        wb = cache.shape[1]
        assert wb == win and wb // dil == A_BLOCK and (dil == 1 or dec_seq <= dil)
        views.append(cache.reshape(batch, wb // dil, dil, SUBLANE, A_E))
        res = 1 if dil == 1 else dec_seq
        specs.append(pl.BlockSpec((nseq, A_BLOCK, res, SUBLANE, A_E), lambda i: (i, 0, 0, 0, 0)))
    n_q = len(A_GROUPS) * A_GW
    rows = nseq * dec_seq
    return pl.pallas_call(
        functools.partial(_attn_sample_kernel, dec_seq=dec_seq, nseq=nseq),
        grid=(batch // nseq,),
        in_specs=[
            pl.BlockSpec(memory_space=pltpu.SMEM),
            pl.BlockSpec((rows, n_q), lambda i: (i, QA_OFF // n_q)),
            pl.BlockSpec((rows, 2 * n_q), lambda i: (i, KV_OFF // (2 * n_q))),
        ] + specs,
        out_specs=pl.BlockSpec((rows, A_GW), lambda i: (i, 0)),
        out_shape=jax.ShapeDtypeStruct((batch * dec_seq, A_GW), F32),
        compiler_params=_cparams(("parallel",)),
        name="attn_sample",
    )(slopes, u, u, *views)


def _hi_lo(x):
    hi = pltpu.bitcast(pltpu.bitcast(x, jnp.uint32) & jnp.uint32(0xFFFF0000), F32)
    return hi, x - hi


def _pdot(a, b, mode, dn=(((1,), (0,)), ((), ()))):
    def dot(x, y):
        return lax.dot_general(x.astype(BF16), y.astype(BF16), dn, preferred_element_type=F32)

    if mode == "1":
        return dot(a, b)
    bh, bl = _hi_lo(b)
    if mode == "b":
        return dot(a, bh) + dot(a, bl)
    ah, al = _hi_lo(a)
    return dot(ah, bh) + dot(ah, bl) + dot(al, bh)


_GDN_PREC = ("1", "1", "b", "1", "1", "1", "1")


def _gdn_kernel(qkv_ref, z_ref, ba_ref, cw_ref, alog_ref, dtb_ref, nw_ref, conv0_ref, s0_ref,
                o_ref, sout_ref, cb_ref, s_ref, *, tb, cc, nc, bpb):
    n = pl.program_id(1)

    @pl.when(n == 0)
    def _():
        s_ref[...] = s0_ref[...]
        cb_ref[:, 0:SUBLANE, :] = conv0_ref[...]

    rows = lax.broadcasted_iota(jnp.int32, (cc, 1), 0)
    live = rows < tb
    ii = lax.broadcasted_iota(jnp.int32, (cc, cc), 0)
    jj = lax.broadcasted_iota(jnp.int32, (cc, cc), 1)
    incl = ii >= jj
    strict = ii > jj
    tri = incl.astype(F32)
    cw = cw_ref[...]
    first = SUBLANE - (G_CONV - 1)
    dn_nt = (((1,), (1,)), ((), ()))
    dn_tn = (((0,), (0,)), ((), ()))
    p_a, p_sq, p_app, p_qk, p_ws, p_o, p_s = _GDN_PREC

    qs, ks, kbs, decays, rhss, e_gcs, e_rests, e_ends = [], [], [], [], [], [], [], []
    idx = range(bpb * G_HEADS)
    for bi in range(bpb):
        x = qkv_ref[bi]
        cb_ref[bi, SUBLANE:SUBLANE + tb, :] = x
        if tb < cc:
            cb_ref[bi, SUBLANE + tb:SUBLANE + cc, :] = jnp.zeros((cc - tb, G_CONV_CH), F32)
        y = cb_ref[bi, SUBLANE:SUBLANE + cc, :] * cw[G_CONV - 1:G_CONV, :]
        for j in range(G_CONV - 2, -1, -1):
            y = y + cb_ref[bi, first + j:first + j + cc, :] * cw[j:j + 1, :]
        if nc > 1:
            cb_ref[bi, 0:SUBLANE, :] = x[tb - SUBLANE:tb, :]
        y = y * jax.nn.sigmoid(y)

        ba = ba_ref[bi]
        if tb < cc:
            ba = jnp.concatenate([ba, jnp.zeros((cc - tb, LANE), F32)], axis=0)
        beta_t = jnp.where(live, jax.nn.sigmoid(ba), 0.0)
        xs = ba + dtb_ref[...]
        softplus = jnp.maximum(xs, 0.0) + jnp.log(1.0 + jnp.exp(-jnp.abs(xs)))
        g_t = jnp.where(live, -jnp.exp(alog_ref[...]) * softplus, 0.0)
        gc = jnp.dot(tri, g_t, precision=HIGHEST, preferred_element_type=F32)
        gc_t = jnp.concatenate([gc, jnp.zeros((LANE - cc, LANE), F32)], axis=0).T
        gc_last = gc[cc - 1:cc, :]
        e_gc = jnp.exp(gc)
        e_rest = jnp.exp(gc_last - gc)
        e_end = jnp.exp(gc_last)
        for h in range(G_HEADS):
            sl = slice(h * G_DK, (h + 1) * G_DK)
            gl = slice(G_HEADS + h, G_HEADS + h + 1)
            q = y[:, sl]
            k = y[:, G_QK + h * G_DK:G_QK + (h + 1) * G_DK]
            v = y[:, 2 * G_QK + h * G_DV:2 * G_QK + (h + 1) * G_DV]
            q = q * lax.rsqrt(jnp.sum(q * q, axis=-1, keepdims=True) + NORM_EPS) * (G_DK ** -0.5)
            k = k * lax.rsqrt(jnp.sum(k * k, axis=-1, keepdims=True) + NORM_EPS)
            if tb < cc:
                q = jnp.where(live, q, 0.0)
                k = jnp.where(live, k, 0.0)
                v = jnp.where(live, v, 0.0)
            beta = beta_t[:, h:h + 1]
            kb = k * beta
            qs.append(q)
            ks.append(k)
            kbs.append(kb)
            decays.append(jnp.exp(jnp.where(
                incl, gc[:, gl] - gc_t[G_HEADS + h:G_HEADS + h + 1, 0:cc], -jnp.inf)))
            rhss.append(jnp.concatenate([v * beta, kb * e_gc[:, gl]], axis=1))
            e_gcs.append(e_gc[:, gl])
            e_rests.append(e_rest[:, gl])
            e_ends.append(e_end[:, gl])

    a_low = [jnp.where(strict, _pdot(kbs[i], ks[i], p_a, dn_nt) * decays[i], 0.0) for i in idx]
    qk = [_pdot(qs[i], ks[i], p_qk, dn_nt) * decays[i] for i in idx]
    rhss = [rhss[i] - _pdot(a_low[i], rhss[i], p_app) for i in idx]
    pw = a_low
    span = 2
    while span < cc:
        pw = [_pdot(pw[i], pw[i], p_sq) for i in idx]
        rhss = [rhss[i] + _pdot(pw[i], rhss[i], p_app) for i in idx]
        span *= 2
    s_old = [s_ref[i // G_HEADS, i % G_HEADS] for i in idx]
    ws = [_pdot(jnp.concatenate([rhss[i][:, G_DV:], qs[i] * e_gcs[i]], axis=0), s_old[i], p_ws)
          for i in idx]
    v_new = [rhss[i][:, :G_DV] - ws[i][:cc] for i in idx]
    for i in idx:
        s_ref[i // G_HEADS, i % G_HEADS] = (
            s_old[i] * e_ends[i] + _pdot(ks[i] * e_rests[i], v_new[i], p_s, dn_tn))
    o = [ws[i][cc:] + _pdot(qk[i], v_new[i], p_o) for i in idx]
    for i in idx:
        bi, h = i // G_HEADS, i % G_HEADS
        sl = slice(h * G_DV, (h + 1) * G_DV)
        o_h = o[i] * lax.rsqrt(jnp.mean(o[i] * o[i], axis=-1, keepdims=True) + NORM_EPS) * nw_ref[...]
        zz = z_ref[bi, :, sl]
        o_ref[bi, :, sl] = o_h[:tb] * (zz * jax.nn.sigmoid(zz))

    @pl.when(n == nc - 1)
    def _():
        sout_ref[...] = s_ref[...]


def _gdn(u, ba, conv0, s0, conv_w, alog_t, dtb_t, norm_w, *, batch, seq, bpb):
    cc = G_CHUNK if seq >= G_CHUNK else SUBLANE
    tb = min(seq, cc)
    nc = seq // tb
    assert nc * tb == seq and (nc == 1 or tb == cc) and batch % bpb == 0
    uv = u.reshape(batch, seq, U_WIDTH)
    bav = ba.reshape(batch, seq, LANE)
    conv0p = jnp.concatenate(
        [jnp.zeros((batch, SUBLANE - (G_CONV - 1), G_CONV_CH), F32), conv0.astype(F32)], axis=1)
    const2 = lambda b, n: (0, 0)
    return pl.pallas_call(
        functools.partial(_gdn_kernel, tb=tb, cc=cc, nc=nc, bpb=bpb),
        grid=(batch // bpb, nc),
        in_specs=[
            pl.BlockSpec((bpb, tb, G_CONV_CH), lambda b, n: (b, n, QKVB_OFF // G_CONV_CH)),
            pl.BlockSpec((bpb, tb, G_QK), lambda b, n: (b, n, Z_OFF // G_QK)),
            pl.BlockSpec((bpb, tb, LANE), lambda b, n: (b, n, 0)),
            pl.BlockSpec((G_CONV, G_CONV_CH), const2),
            pl.BlockSpec((1, LANE), const2),
            pl.BlockSpec((1, LANE), const2),
            pl.BlockSpec((1, G_DV), const2),
            pl.BlockSpec((bpb, SUBLANE, G_CONV_CH), lambda b, n: (b, 0, 0)),
            pl.BlockSpec((bpb, G_HEADS, G_DK, G_DV), lambda b, n: (b, 0, 0, 0)),
        ],
        out_specs=[
            pl.BlockSpec((bpb, tb, G_QK), lambda b, n: (b, n, 0)),
            pl.BlockSpec((bpb, G_HEADS, G_DK, G_DV), lambda b, n: (b, 0, 0, 0)),
        ],
        out_shape=[
            jax.ShapeDtypeStruct((batch, seq, G_QK), F32),
            jax.ShapeDtypeStruct((batch, G_HEADS, G_DK, G_DV), F32),
        ],
        scratch_shapes=[
            pltpu.VMEM((bpb, SUBLANE + cc, G_CONV_CH), F32),
            pltpu.VMEM((bpb, G_HEADS, G_DK, G_DV), F32),
        ],
        compiler_params=_cparams(("parallel", "arbitrary")),
        name="gdn",
    )(uv, uv, bav, conv_w, alog_t, dtb_t, norm_w.reshape(1, G_DV), conv0p, s0)


def _mix_kernel(x_ref, oa_ref, ob_ref, ga_ref, gb_ref, wa_ref, wb_ref, wo_ref, o_ref):
    pa = jnp.dot(oa_ref[...].astype(BF16), wa_ref[...], preferred_element_type=F32)
    pb = jnp.dot(ob_ref[...].astype(BF16), wb_ref[...], preferred_element_type=F32)
    merged = jax.nn.sigmoid(ga_ref[...]) * pa + jax.nn.sigmoid(gb_ref[...]) * pb
    o_ref[...] = x_ref[...] + jnp.dot(merged.astype(BF16), wo_ref[...], preferred_element_type=F32)


def _mix(x, oa, ob, u, wa, wb, wo, *, tm):
    n, d = x.shape
    const = lambda i: (0, 0)
    return pl.pallas_call(
        _mix_kernel,
        grid=(n // tm,),
        in_specs=[
            pl.BlockSpec((tm, d), lambda i: (i, 0)),
            pl.BlockSpec((tm, A_GW), lambda i: (i, 0)),
            pl.BlockSpec((tm, G_QK), lambda i: (i, 0)),
            pl.BlockSpec((tm, d), lambda i: (i, GATE_OFF // d)),
            pl.BlockSpec((tm, d), lambda i: (i, GATE_OFF // d + 1)),
            pl.BlockSpec((A_GW, d), const),
            pl.BlockSpec((G_QK, d), const),
            pl.BlockSpec((d, d), const),
        ],
        out_specs=pl.BlockSpec((tm, d), lambda i: (i, 0)),
        out_shape=jax.ShapeDtypeStruct((n, d), F32),
        compiler_params=_cparams(("parallel",)),
        name="mix",
    )(x, oa, ob, u, u, wa, wb, wo)


def _prep_weights(w_in, gdn_a_log, gdn_dt_bias):
    aw = len(A_GROUPS) * A_GW
    qa, ka, va = w_in[:, 0:aw], w_in[:, aw:2 * aw], w_in[:, 2 * aw:3 * aw]
    off = 3 * aw
    qkvb = w_in[:, off:off + G_CONV_CH]
    off += G_CONV_CH
    zb = w_in[:, off:off + G_QK]
    off += G_QK
    ba = w_in[:, off:off + 2 * G_HEADS]
    off += 2 * G_HEADS
    gate = w_in[:, off:]
    kv = []
    for g in range(len(A_GROUPS)):
        kv += [ka[:, g * A_GW:(g + 1) * A_GW], va[:, g * A_GW:(g + 1) * A_GW]]
    w_main = jnp.concatenate([qkvb, zb, gate] + kv + [qa], axis=1).astype(BF16)
    assert w_main.shape[1] == U_WIDTH
    ba_pad = jnp.pad(ba, ((0, 0), (0, LANE - 2 * G_HEADS)))
    ba_hi = ba_pad.astype(BF16)
    ba_lo = (ba_pad - ba_hi.astype(F32)).astype(BF16)
    pad = (G_HEADS, LANE - 2 * G_HEADS)
    alog_t = jnp.pad(gdn_a_log.astype(F32), pad).reshape(1, LANE)
    dtb_t = jnp.pad(gdn_dt_bias.astype(F32), pad).reshape(1, LANE)
    return w_main, ba_hi, ba_lo, alog_t, dtb_t


def _kv_rows(u3, group, keep):
    b, t, _ = u3.shape
    off = KV_OFF + group * 2 * A_GW
    return u3[:, t - keep:, off:off + 2 * A_GW].reshape(1, b, keep, 2, A_HPG, A_E)


def _layer(x, batch, seq, conv0, s0, attn_fn, p, *, tm, gdn_bpb):
    tm = min(tm, x.shape[0])
    gdn_bpb = min(gdn_bpb, batch)
    x1 = _ffn(x, p["norm_ffn1"], p["w_ffn1_gu"], p["w_ffn1_down"], tm=tm)
    u, ba = _inproj(x1, p["norm_mix"], p["w_main"], p["ba_hi"], p["ba_lo"], tm=tm)
    oa = attn_fn(u)
    ob, s_new = _gdn(u, ba, conv0, s0, p["conv_w"], p["alog_t"], p["dtb_t"], p["gdn_norm"],
                     batch=batch, seq=seq, bpb=gdn_bpb)
    x2 = _mix(x1, oa.reshape(batch * seq, A_GW), ob.reshape(batch * seq, G_QK), u,
              p["w_proj_a"], p["w_proj_b"], p["w_out"], tm=min(tm, 512))
    y = _ffn(x2, p["norm_ffn2"], p["w_ffn2_gu"], p["w_ffn2_down"], p["norm_out"], tm=tm)
    u3 = u.reshape(batch, seq, U_WIDTH)
    rows = [_kv_rows(u3, g, min(win, seq)) for g, (win, _) in enumerate(A_GROUPS)]
    conv_new = u3[:, seq - (G_CONV - 1):, QKVB_OFF:QKVB_OFF + G_CONV_CH][None]
    return y.reshape(batch, seq, D_MODEL), rows, conv_new, s_new[None]


def kernel(x_prompt, x_sample, cache_kv_w128, cache_kv_w512, cache_kv_w2048, state_conv, state_ssm,
           norm_ffn1, w_ffn1_gu, w_ffn1_down, norm_mix, w_in, conv_w, gdn_a_log, gdn_dt_bias, gdn_norm,
           w_proj_a, w_proj_b, w_out, norm_ffn2, w_ffn2_gu, w_ffn2_down, norm_out):
    assert w_in.shape[0] == 1, "single layer"
    n_heads = len(A_GROUPS) * A_HPG
    slopes = jnp.exp2(-8.0 * jnp.arange(1, n_heads + 1, dtype=F32) / n_heads)
    w_main, ba_hi, ba_lo, alog_t, dtb_t = _prep_weights(w_in[0], gdn_a_log[0], gdn_dt_bias[0])
    p = dict(
        norm_ffn1=norm_ffn1[0], w_ffn1_gu=w_ffn1_gu[0].astype(BF16), w_ffn1_down=w_ffn1_down[0].astype(BF16),
        norm_mix=norm_mix[0], w_main=w_main, ba_hi=ba_hi, ba_lo=ba_lo, conv_w=conv_w[0],
        alog_t=alog_t, dtb_t=dtb_t, gdn_norm=gdn_norm[0],
        w_proj_a=w_proj_a[0].astype(BF16), w_proj_b=w_proj_b[0].astype(BF16), w_out=w_out[0].astype(BF16),
        norm_ffn2=norm_ffn2[0], w_ffn2_gu=w_ffn2_gu[0].astype(BF16), w_ffn2_down=w_ffn2_down[0].astype(BF16),
        norm_out=norm_out,
    )
    bp, tp, d = x_prompt.shape
    bs, ts, _ = x_sample.shape

    conv0_p = jnp.zeros((bp, G_CONV - 1, G_CONV_CH), F32)
    ssm0_p = jnp.zeros((bp, G_HEADS, G_DK, G_DV), F32)
    yp, rows_p, conv_p, ssm_p = _layer(
        x_prompt.reshape(bp * tp, d), bp, tp, conv0_p, ssm0_p,
        functools.partial(_attn_prompt, slopes, batch=bp, seq=tp), p, tm=1024, gdn_bpb=2)

    caches = (cache_kv_w128[0], cache_kv_w512[0], cache_kv_w2048[0])
    ys, rows_s, conv_s, ssm_s = _layer(
        x_sample.reshape(bs * ts, d), bs, ts, state_conv[0], state_ssm[0],
        lambda u: _attn_sample(slopes, u, caches, batch=bs, dec_seq=ts), p, tm=512, gdn_bpb=2)

    return (yp, ys, rows_p[0], rows_p[1], rows_p[2], conv_p, ssm_p,
            rows_s[0], rows_s[1], rows_s[2], conv_s, ssm_s)
```

```python
import functools

import jax
import jax.numpy as jnp
from jax import lax
from jax.experimental import pallas as pl
from jax.experimental.pallas import tpu as pltpu

F32 = jnp.float32
BF16 = jnp.bfloat16
HIGHEST = lax.Precision.HIGHEST

D_MODEL = 1024
D_FF = 2816
NORM_EPS = 1e-6
A_GROUPS = ((128, 1), (512, 4), (2048, 16))
A_HPG = 4
A_E = 128
A_BLOCK = 128
A_GW = A_HPG * A_E
A_ST = 2048
G_HEADS = 8
G_DK = 128
G_DV = 128
G_QK = G_HEADS * G_DK
G_CONV_CH = 3 * G_QK
G_CONV = 4
G_CHUNK = 64

QKVB_OFF = 0
Z_OFF = 3072
GATE_OFF = 4096
UB_WIDTH = 6144
KV_OFF = 0
QA_OFF = 3072
UA_WIDTH = 4608
U_WIDTH = UB_WIDTH + UA_WIDTH
W_CB = 512

VMEM_LIMIT = 56 * 1024 * 1024
LANE = 128
SUBLANE = 8


def _cparams(sem):
    return pltpu.CompilerParams(dimension_semantics=sem, vmem_limit_bytes=VMEM_LIMIT)


def _rms_rows(x, w):
    return x * lax.rsqrt(jnp.mean(x * x, axis=-1, keepdims=True) + NORM_EPS) * w


def _ffn_kernel(x_ref, nw_ref, wg_ref, wu_ref, wd_ref, *rest, n_ff, final_norm):
    if final_norm:
        onw_ref, o_ref, hb_ref = rest
    else:
        o_ref, hb_ref = rest
    j = pl.program_id(1)

    @pl.when(j == 0)
    def _():
        hb_ref[...] = _rms_rows(x_ref[...], nw_ref[...]).astype(BF16)
        o_ref[...] = jnp.zeros_like(o_ref)

    hb = hb_ref[...]
    g = jnp.dot(hb, wg_ref[...], preferred_element_type=F32)
    u = jnp.dot(hb, wu_ref[...], preferred_element_type=F32)
    a = (g * jax.nn.sigmoid(g) * u).astype(BF16)
    o_ref[...] += jnp.dot(a, wd_ref[...], preferred_element_type=F32)

    @pl.when(j == n_ff - 1)
    def _():
        y = x_ref[...] + 0.5 * o_ref[...]
        if final_norm:
            y = _rms_rows(y, onw_ref[...])
        o_ref[...] = y


def _ffn(x, norm_w, w_gu, w_down, out_norm_w=None, *, tm, tf=256):
    n, d = x.shape
    n_ff = D_FF // tf
    final_norm = out_norm_w is not None
    in_specs = [
        pl.BlockSpec((tm, d), lambda i, j: (i, 0)),
        pl.BlockSpec((1, d), lambda i, j: (0, 0)),
        pl.BlockSpec((d, tf), lambda i, j: (0, j)),
        pl.BlockSpec((d, tf), lambda i, j: (0, j + n_ff)),
        pl.BlockSpec((tf, d), lambda i, j: (j, 0)),
    ]
    args = [x, norm_w.reshape(1, d), w_gu, w_gu, w_down]
    if final_norm:
        in_specs.append(pl.BlockSpec((1, d), lambda i, j: (0, 0)))
        args.append(out_norm_w.reshape(1, d))
    return pl.pallas_call(
        functools.partial(_ffn_kernel, n_ff=n_ff, final_norm=final_norm),
        grid=(n // tm, n_ff),
        in_specs=in_specs,
        out_specs=pl.BlockSpec((tm, d), lambda i, j: (i, 0)),
        out_shape=jax.ShapeDtypeStruct((n, d), F32),
        scratch_shapes=[pltpu.VMEM((tm, d), BF16)],
        compiler_params=_cparams(("parallel", "arbitrary")),
        name="ffn_final" if final_norm else "ffn",
    )(*args)


_DN_NT = (((1,), (1,)), ((), ()))


def _inproj_kernel(x_ref, nw_ref, w_ref, wbh_ref, wbl_ref, ub_ref, ua_ref, ba_ref, *rest, nb, nq, tail):
    if tail:
        tail_ref, hb_ref = rest
    else:
        (hb_ref,) = rest
    j = pl.program_id(1)

    @pl.when(j == 0)
    def _():
        h = _rms_rows(x_ref[...], nw_ref[...])
        hb = h.astype(BF16)
        hl = (h - hb.astype(F32)).astype(BF16)
        hb_ref[...] = hb
        ba_ref[...] = (lax.dot_general(hb, wbh_ref[...], _DN_NT, preferred_element_type=F32)
                       + lax.dot_general(hb, wbl_ref[...], _DN_NT, preferred_element_type=F32)
                       + lax.dot_general(hl, wbh_ref[...], _DN_NT, preferred_element_type=F32))

    def project(rows):
        return lax.dot_general(hb_ref[rows, :], w_ref[...], _DN_NT, preferred_element_type=F32)

    @pl.when(j < nb)
    def _():
        ub_ref[...] = project(slice(None)).astype(ub_ref.dtype)

    @pl.when(j >= nb)
    def _():
        ua_ref[...] = project(slice(None))

    if tail:
        @pl.when(j < nq)
        def _():
            tm = hb_ref.shape[0]
            tail_ref[...] = project(slice(tm - 2 * SUBLANE, tm))[SUBLANE:, :]


def _inproj(x, norm_w, w_main, w_ba_hi, w_ba_lo, *, tm, b_dtype, tail, tn=512):
    n, d = x.shape
    nb, nq = UB_WIDTH // tn, G_CONV_CH // tn
    assert nb * tn == UB_WIDTH and nq * tn == G_CONV_CH and U_WIDTH % tn == 0 and QKVB_OFF == 0
    out_specs = [
        pl.BlockSpec((tm, tn), lambda i, j: (i, jnp.minimum(j, nb - 1))),
        pl.BlockSpec((tm, tn), lambda i, j: (i, jnp.maximum(j - nb, 0))),
        pl.BlockSpec((tm, LANE), lambda i, j: (i, 0)),
    ]
    out_shape = [
        jax.ShapeDtypeStruct((n, UB_WIDTH), b_dtype),
        jax.ShapeDtypeStruct((n, UA_WIDTH), F32),
        jax.ShapeDtypeStruct((n, LANE), F32),
    ]
    if tail:
        out_specs.append(pl.BlockSpec((SUBLANE, tn), lambda i, j: (i, jnp.minimum(j, nq - 1))))
        out_shape.append(jax.ShapeDtypeStruct((n // tm * SUBLANE, G_CONV_CH), F32))
    return pl.pallas_call(
        functools.partial(_inproj_kernel, nb=nb, nq=nq, tail=tail),
        grid=(n // tm, U_WIDTH // tn),
        in_specs=[
            pl.BlockSpec((tm, d), lambda i, j: (i, 0), pipeline_mode=pl.Buffered(1)),
            pl.BlockSpec((1, d), lambda i, j: (0, 0)),
            pl.BlockSpec((tn, d), lambda i, j: (j, 0)),
            pl.BlockSpec((LANE, d), lambda i, j: (0, 0)),
            pl.BlockSpec((LANE, d), lambda i, j: (0, 0)),
        ],
        out_specs=out_specs,
        out_shape=out_shape,
        scratch_shapes=[pltpu.VMEM((tm, d), BF16)],
        compiler_params=_cparams(("parallel", "arbitrary")),
        name="inproj",
    )(x, norm_w.reshape(1, d), w_main, w_ba_hi, w_ba_lo)


def _rows(start, size, stride):
    return pl.ds(start, size) if stride == 1 else pl.ds(start, size, stride=stride)


def _attn_prompt_kernel(slopes_ref, *refs):
    n_g = len(A_GROUPS)
    ins, o_ref, scr = refs[:5 * n_g], refs[5 * n_g], refs[5 * n_g + 1:]
    st = pl.program_id(1)
    hh = pl.program_id(2)
    scale = A_E ** -0.5
    qi = lax.broadcasted_iota(jnp.int32, (A_BLOCK, 2 * A_BLOCK), 0)
    kj = lax.broadcasted_iota(jnp.int32, (A_BLOCK, 2 * A_BLOCK), 1)
    delta = A_BLOCK + qi - kj
    band = jnp.logical_and(delta >= 0, delta <= A_BLOCK)
    band_first = jnp.logical_and(band, jnp.logical_or(kj >= A_BLOCK, st > 0))
    dn_nt = (((1,), (1,)), ((), ()))
    for g, (win, dil) in enumerate(A_GROUPS):
        q_ref, kc_ref, vc_ref, kp_ref, vp_ref = ins[5 * g:5 * g + 5]
        kf_ref, vf_ref, og_ref, lg_ref = scr[4 * g:4 * g + 4]
        kf_ref[0:win, :] = kp_ref[...]
        kf_ref[win:win + A_ST, :] = kc_ref[...]
        vf_ref[0:win, :] = vp_ref[...]
        vf_ref[win:win + A_ST, :] = vc_ref[...]
        slope = slopes_ref[g * A_HPG + hh]
        bias = -slope * (dil * delta).astype(F32)
        bias_mid = jnp.where(band, bias, -jnp.inf)
        bias_first = jnp.where(band_first, bias, -jnp.inf)
        for r in range(dil):
            for j in range(A_ST // (A_BLOCK * dil)):
                row0 = r + j * A_BLOCK * dil
                q = q_ref[_rows(row0, A_BLOCK, dil), :].astype(BF16)
                k2 = kf_ref[_rows(row0, 2 * A_BLOCK, dil), :].astype(BF16)
                v2 = vf_ref[_rows(row0, 2 * A_BLOCK, dil), :].astype(BF16)
                s = lax.dot_general(q, k2, dn_nt, preferred_element_type=F32) * scale
                s = s + (bias_first if j == 0 else bias_mid)
                m = jnp.max(s, axis=-1, keepdims=True)
                p = jnp.exp(s - m)
                den = jnp.sum(p, axis=-1, keepdims=True)
                acc = jnp.dot(p.astype(BF16), v2, preferred_element_type=F32)
                og_ref[_rows(row0, A_BLOCK, dil), :] = acc / den
                lg_ref[_rows(row0, A_BLOCK, dil), :] = jnp.broadcast_to(m + jnp.log(den), (A_BLOCK, A_E))
    chunk = 2 * A_BLOCK
    for c in range(A_ST // chunk):
        rs = slice(c * chunk, (c + 1) * chunk)
        ls = [scr[4 * g + 3][rs, :] for g in range(n_g)]
        mx = functools.reduce(jnp.maximum, ls)
        ws = [jnp.exp(l - mx) for l in ls]
        num = sum(w * scr[4 * g + 2][rs, :] for g, w in enumerate(ws))
        o_ref[rs, :] = num / sum(ws)


def _attn_prompt(slopes, u, *, batch, seq):
    assert seq % A_ST == 0
    u3 = u.reshape(batch, seq, UA_WIDTH)
    in_specs = [pl.BlockSpec(memory_space=pltpu.SMEM)]
    args = [slopes]
    scratch = []
    for g, (win, dil) in enumerate(A_GROUPS):
        assert win == A_BLOCK * dil and A_ST % win == 0
        qcb = (QA_OFF + g * A_GW) // A_E
        kcb = (KV_OFF + g * 2 * A_GW) // A_E
        vcb = kcb + A_HPG
        per = A_ST // win

        def cur(cb):
            return pl.BlockSpec((None, A_ST, A_E), lambda b, st, hh, cb=cb: (b, st, cb + hh))

        def prv(cb, win=win, per=per):
            return pl.BlockSpec((None, win, A_E),
                                lambda b, st, hh, cb=cb: (b, jnp.maximum(st * per - 1, 0), cb + hh))

        in_specs += [cur(qcb), cur(kcb), cur(vcb), prv(kcb), prv(vcb)]
        args += [u3] * 5
        scratch += [pltpu.VMEM((win + A_ST, A_E), F32), pltpu.VMEM((win + A_ST, A_E), F32),
                    pltpu.VMEM((A_ST, A_E), F32), pltpu.VMEM((A_ST, A_E), F32)]
    return pl.pallas_call(
        _attn_prompt_kernel,
        grid=(batch, seq // A_ST, A_HPG),
        in_specs=in_specs,
        out_specs=pl.BlockSpec((None, A_ST, A_E), lambda b, st, hh: (b, st, hh)),
        out_shape=jax.ShapeDtypeStruct((batch, seq, A_GW), F32),
        scratch_shapes=scratch,
        compiler_params=_cparams(("parallel", "parallel", "arbitrary")),
        name="attn_prompt",
    )(*args)


def _pick_rows(rows):
    sub = lax.broadcasted_iota(jnp.int32, (SUBLANE, A_E), 0)
    out = jnp.zeros((SUBLANE, A_E), F32)
    for i, r in enumerate(rows):
        out = jnp.where(sub == i, jnp.broadcast_to(r, (SUBLANE, A_E)), out)
    return out


def _attn_sample_kernel(slopes_ref, q_ref, kv_ref, c0_ref, c1_ref, c2_ref, o_ref, *, dec_seq, nseq):
    scale = A_E ** -0.5
    caches = (c0_ref, c1_ref, c2_ref)
    n_keys = A_BLOCK * SUBLANE
    sub = lax.broadcasted_iota(jnp.int32, (SUBLANE, n_keys), 0)
    lane = lax.broadcasted_iota(jnp.int32, (SUBLANE, n_keys), 1)
    tile_row = lane & (SUBLANE - 1)
    m_idx = lane >> 3
    own_k = tile_row == sub
    sub1 = lax.broadcasted_iota(jnp.int32, (SUBLANE, 1), 0)
    dn_nt = (((1,), (1,)), ((), ()))
    slope8s = []
    for g in range(len(A_GROUPS)):
        slope8 = jnp.zeros((SUBLANE, 1), F32)
        for h in range(A_HPG):
            slope8 = jnp.where(sub1 == h, slopes_ref[g * A_HPG + h], slope8)
        slope8s.append(slope8)

    def keys(bb, s, g):
        res = 0 if A_GROUPS[g][1] == 1 else s
        return caches[g][bb, :, res, :, :].reshape(n_keys, A_E).astype(BF16)

    units = [(bb, s, g) for bb in range(nseq) for s in range(dec_seq) for g in range(len(A_GROUPS))]
    q8s = [_pick_rows([q_ref[bb * dec_seq + s:bb * dec_seq + s + 1, g * A_GW + h * A_E:g * A_GW + (h + 1) * A_E]
                       for h in range(A_HPG)]) for bb, s, g in units]
    scs = [lax.dot_general(q8.astype(BF16), keys(*u), dn_nt, preferred_element_type=F32) * scale
           for q8, u in zip(q8s, units)]
    ms, dens, pvs, news = [], [], [], []
    for (bb, s, g), q8, sc in zip(units, q8s, scs):
        win, dil = A_GROUPS[g]
        row = bb * dec_seq + s
        slope8 = slope8s[g]
        if dil == 1:
            dist = (win + s - m_idx).astype(F32)
            valid = jnp.logical_and(own_k, m_idx >= s)
            new_rows = [(bb * dec_seq + t, float(s - t)) for t in range(s + 1)]
        else:
            dist = (dil * (win // dil - m_idx)).astype(F32)
            valid = own_k
            new_rows = [(row, 0.0)]
        sc = jnp.where(valid, sc - slope8 * dist, -jnp.inf)
        m = jnp.max(sc, axis=-1, keepdims=True)
        koff = g * 2 * A_GW
        new = []
        for nrow, ndist in new_rows:
            k8 = _pick_rows([kv_ref[nrow:nrow + 1, koff + h * A_E:koff + (h + 1) * A_E]
                             for h in range(A_HPG)])
            v8 = _pick_rows([kv_ref[nrow:nrow + 1, koff + A_GW + h * A_E:koff + A_GW + (h + 1) * A_E]
                             for h in range(A_HPG)])
            s_n = jnp.sum(k8 * q8, axis=-1, keepdims=True) * scale - slope8 * ndist
            m = jnp.maximum(m, s_n)
            new.append((s_n, v8))
        p = jnp.exp(sc - m)
        ms.append(m)
        dens.append(jnp.sum(p, axis=-1, keepdims=True))
        pvs.append(pltpu.roll(p, A_HPG, 1).astype(BF16))
        news.append(new)
    accs = [jnp.dot(pv, keys(*u), preferred_element_type=F32) for pv, u in zip(pvs, units)]

    out_tiles = [jnp.zeros((nseq * dec_seq, A_E), F32) for _ in range(A_HPG)]
    out_sub = lax.broadcasted_iota(jnp.int32, (nseq * dec_seq, A_E), 0)
    n_g = len(A_GROUPS)
    for ui in range(0, len(units), n_g):
        bb, s, _ = units[ui]
        outs, lses = [], []
        for m, den, acc, new in zip(ms[ui:ui + n_g], dens[ui:ui + n_g], accs[ui:ui + n_g], news[ui:ui + n_g]):
            for s_n, v8 in new:
                p_n = jnp.exp(s_n - m)
                den = den + p_n
                acc = acc + p_n * v8
            outs.append(acc / den)
            lses.append(m + jnp.log(den))
        mx = functools.reduce(jnp.maximum, lses)
        ws = [jnp.exp(l - mx) for l in lses]
        o8 = sum(w * o for w, o in zip(ws, outs)) / sum(ws)
        for h in range(A_HPG):
            out_tiles[h] = jnp.where(out_sub == bb * dec_seq + s,
                                     jnp.broadcast_to(o8[h:h + 1, :], (nseq * dec_seq, A_E)), out_tiles[h])
    for h in range(A_HPG):
        o_ref[:, h * A_E:(h + 1) * A_E] = out_tiles[h]


def _attn_sample(slopes, u, caches, *, batch, dec_seq):
    nseq = SUBLANE // dec_seq
    assert nseq * dec_seq == SUBLANE and batch % nseq == 0
    views, specs = [], []
    for (win, dil), cache in zip(A_GROUPS, caches):
        wb = cache.shape[1]
        assert wb == win and wb // dil == A_BLOCK and (dil == 1 or dec_seq <= dil)
        views.append(cache.reshape(batch, wb // dil, dil, SUBLANE, A_E))
        res = 1 if dil == 1 else dec_seq
        specs.append(pl.BlockSpec((nseq, A_BLOCK, res, SUBLANE, A_E), lambda i: (i, 0, 0, 0, 0)))
    n_q = len(A_GROUPS) * A_GW
    rows = nseq * dec_seq
    return pl.pallas_call(
        functools.partial(_attn_sample_kernel, dec_seq=dec_seq, nseq=nseq),
        grid=(batch // nseq,),
        in_specs=[
            pl.BlockSpec(memory_space=pltpu.SMEM),
            pl.BlockSpec((rows, n_q), lambda i: (i, QA_OFF // n_q)),
            pl.BlockSpec((rows, 2 * n_q), lambda i: (i, KV_OFF // (2 * n_q))),
        ] + specs,
        out_specs=pl.BlockSpec((rows, A_GW), lambda i: (i, 0)),
        out_shape=jax.ShapeDtypeStruct((batch * dec_seq, A_GW), F32),
        compiler_params=_cparams(("parallel",)),
        name="attn_sample",
    )(slopes, u, u, *views)


def _hi_lo(x):
    hi = pltpu.bitcast(pltpu.bitcast(x, jnp.uint32) & jnp.uint32(0xFFFF0000), F32)
    return hi, x - hi


def _pdot(a, b, mode, dn=(((1,), (0,)), ((), ()))):
    def dot(x, y):
        return lax.dot_general(x.astype(BF16), y.astype(BF16), dn, preferred_element_type=F32)

    if mode == "1":
        return dot(a, b)
    bh, bl = _hi_lo(b)
    if mode == "b":
        return dot(a, bh) + dot(a, bl)
    ah, al = _hi_lo(a)
    return dot(ah, bh) + dot(ah, bl) + dot(al, bh)


_GDN_PREC = ("1", "1", "b", "1", "1", "1", "1")


def _gdn_kernel(qkv_ref, z_ref, ba_ref, cw_ref, alog_ref, dtb_ref, nw_ref, conv0_ref, s0_ref,
                o_ref, sout_ref, cb_ref, s_ref, *, tb, cc, nc, bpb):
    n = pl.program_id(1)

    @pl.when(n == 0)
    def _():
        s_ref[...] = s0_ref[...]
        cb_ref[:, 0:SUBLANE, :] = conv0_ref[...]

    rows = lax.broadcasted_iota(jnp.int32, (cc, 1), 0)
    live = rows < tb
    ii = lax.broadcasted_iota(jnp.int32, (cc, cc), 0)
    jj = lax.broadcasted_iota(jnp.int32, (cc, cc), 1)
    incl = ii >= jj
    strict = ii > jj
    tri = incl.astype(F32)
    cw = cw_ref[...]
    first = SUBLANE - (G_CONV - 1)
    dn_nt = (((1,), (1,)), ((), ()))
    dn_tn = (((0,), (0,)), ((), ()))
    p_a, p_sq, p_app, p_qk, p_ws, p_o, p_s = _GDN_PREC

    qs, ks, kbs, decays, rhss, e_gcs, e_rests, e_ends = [], [], [], [], [], [], [], []
    idx = range(bpb * G_HEADS)
    for bi in range(bpb):
        x = qkv_ref[bi].astype(F32)
        cb_ref[bi, SUBLANE:SUBLANE + tb, :] = x
        if tb < cc:
            cb_ref[bi, SUBLANE + tb:SUBLANE + cc, :] = jnp.zeros((cc - tb, G_CONV_CH), F32)
        y = cb_ref[bi, SUBLANE:SUBLANE + cc, :] * cw[G_CONV - 1:G_CONV, :]
        for j in range(G_CONV - 2, -1, -1):
            y = y + cb_ref[bi, first + j:first + j + cc, :] * cw[j:j + 1, :]
        if nc > 1:
            cb_ref[bi, 0:SUBLANE, :] = x[tb - SUBLANE:tb, :]
        y = y * jax.nn.sigmoid(y)

        ba = ba_ref[bi]
        if tb < cc:
            ba = jnp.concatenate([ba, jnp.zeros((cc - tb, LANE), F32)], axis=0)
        beta_t = jnp.where(live, jax.nn.sigmoid(ba), 0.0)
        xs = ba + dtb_ref[...]
        softplus = jnp.maximum(xs, 0.0) + jnp.log(1.0 + jnp.exp(-jnp.abs(xs)))
        g_t = jnp.where(live, -jnp.exp(alog_ref[...]) * softplus, 0.0)
        gc = jnp.dot(tri, g_t, precision=HIGHEST, preferred_element_type=F32)
        gc_t = jnp.concatenate([gc, jnp.zeros((LANE - cc, LANE), F32)], axis=0).T
        gc_last = gc[cc - 1:cc, :]
        e_gc = jnp.exp(gc)
        e_rest = jnp.exp(gc_last - gc)
        e_end = jnp.exp(gc_last)
        for h in range(G_HEADS):
            sl = slice(h * G_DK, (h + 1) * G_DK)
            gl = slice(G_HEADS + h, G_HEADS + h + 1)
            q = y[:, sl]
            k = y[:, G_QK + h * G_DK:G_QK + (h + 1) * G_DK]
            v = y[:, 2 * G_QK + h * G_DV:2 * G_QK + (h + 1) * G_DV]
            q = q * lax.rsqrt(jnp.sum(q * q, axis=-1, keepdims=True) + NORM_EPS) * (G_DK ** -0.5)
            k = k * lax.rsqrt(jnp.sum(k * k, axis=-1, keepdims=True) + NORM_EPS)
            if tb < cc:
                q = jnp.where(live, q, 0.0)
                k = jnp.where(live, k, 0.0)
                v = jnp.where(live, v, 0.0)
            beta = beta_t[:, h:h + 1]
            kb = k * beta
            qs.append(q)
            ks.append(k)
            kbs.append(kb)
            decays.append(jnp.exp(jnp.where(
                incl, gc[:, gl] - gc_t[G_HEADS + h:G_HEADS + h + 1, 0:cc], -jnp.inf)))
            rhss.append(jnp.concatenate([v * beta, kb * e_gc[:, gl]], axis=1))
            e_gcs.append(e_gc[:, gl])
            e_rests.append(e_rest[:, gl])
            e_ends.append(e_end[:, gl])

    a_low = [jnp.where(strict, _pdot(kbs[i], ks[i], p_a, dn_nt) * decays[i], 0.0) for i in idx]
    qk = [_pdot(qs[i], ks[i], p_qk, dn_nt) * decays[i] for i in idx]
    rhss = [rhss[i] - _pdot(a_low[i], rhss[i], p_app) for i in idx]
    pw = a_low
    span = 2
    while span < cc:
        pw = [_pdot(pw[i], pw[i], p_sq) for i in idx]
        rhss = [rhss[i] + _pdot(pw[i], rhss[i], p_app) for i in idx]
        span *= 2
    s_old = [s_ref[i // G_HEADS, i % G_HEADS] for i in idx]
    ws = [_pdot(jnp.concatenate([rhss[i][:, G_DV:], qs[i] * e_gcs[i]], axis=0), s_old[i], p_ws)
          for i in idx]
    v_new = [rhss[i][:, :G_DV] - ws[i][:cc] for i in idx]
    for i in idx:
        s_ref[i // G_HEADS, i % G_HEADS] = (
            s_old[i] * e_ends[i] + _pdot(ks[i] * e_rests[i], v_new[i], p_s, dn_tn))
    o = [ws[i][cc:] + _pdot(qk[i], v_new[i], p_o) for i in idx]
    for i in idx:
        bi, h = i // G_HEADS, i % G_HEADS
        sl = slice(h * G_DV, (h + 1) * G_DV)
        o_h = o[i] * lax.rsqrt(jnp.mean(o[i] * o[i], axis=-1, keepdims=True) + NORM_EPS) * nw_ref[...]
        zz = z_ref[bi, :, sl].astype(F32)
        o_ref[bi, :, sl] = o_h[:tb] * (zz * jax.nn.sigmoid(zz))

    @pl.when(n == nc - 1)
    def _():
        sout_ref[...] = s_ref[...]


def _gdn(u, ba, conv0, s0, conv_w, alog_t, dtb_t, norm_w, *, batch, seq, bpb):
    cc = G_CHUNK if seq >= G_CHUNK else SUBLANE
    tb = min(seq, cc)
    nc = seq // tb
    assert nc * tb == seq and (nc == 1 or tb == cc) and batch % bpb == 0
    uv = u.reshape(batch, seq, UB_WIDTH)
    bav = ba.reshape(batch, seq, LANE)
    conv0p = jnp.concatenate(
        [jnp.zeros((batch, SUBLANE - (G_CONV - 1), G_CONV_CH), F32), conv0.astype(F32)], axis=1)
    const2 = lambda b, n: (0, 0)
    return pl.pallas_call(
        functools.partial(_gdn_kernel, tb=tb, cc=cc, nc=nc, bpb=bpb),
        grid=(batch // bpb, nc),
        in_specs=[
            pl.BlockSpec((bpb, tb, G_CONV_CH), lambda b, n: (b, n, QKVB_OFF // G_CONV_CH)),
            pl.BlockSpec((bpb, tb, G_QK), lambda b, n: (b, n, Z_OFF // G_QK)),
            pl.BlockSpec((bpb, tb, LANE), lambda b, n: (b, n, 0)),
            pl.BlockSpec((G_CONV, G_CONV_CH), const2),
            pl.BlockSpec((1, LANE), const2),
            pl.BlockSpec((1, LANE), const2),
            pl.BlockSpec((1, G_DV), const2),
            pl.BlockSpec((bpb, SUBLANE, G_CONV_CH), lambda b, n: (b, 0, 0)),
            pl.BlockSpec((bpb, G_HEADS, G_DK, G_DV), lambda b, n: (b, 0, 0, 0)),
        ],
        out_specs=[
            pl.BlockSpec((bpb, tb, G_QK), lambda b, n: (b, n, 0)),
            pl.BlockSpec((bpb, G_HEADS, G_DK, G_DV), lambda b, n: (b, 0, 0, 0)),
        ],
        out_shape=[
            jax.ShapeDtypeStruct((batch, seq, G_QK), F32),
            jax.ShapeDtypeStruct((batch, G_HEADS, G_DK, G_DV), F32),
        ],
        scratch_shapes=[
            pltpu.VMEM((bpb, SUBLANE + cc, G_CONV_CH), F32),
            pltpu.VMEM((bpb, G_HEADS, G_DK, G_DV), F32),
        ],
        compiler_params=_cparams(("parallel", "arbitrary")),
        name="gdn",
    )(uv, uv, bav, conv_w, alog_t, dtb_t, norm_w.reshape(1, G_DV), conv0p, s0)


def _mix_kernel(x_ref, oa_ref, ob_ref, ga_ref, gb_ref, wa_ref, wb_ref, wo_ref, o_ref):
    pa = jnp.dot(oa_ref[...].astype(BF16), wa_ref[...], preferred_element_type=F32)
    pb = jnp.dot(ob_ref[...].astype(BF16), wb_ref[...], preferred_element_type=F32)
    merged = (jax.nn.sigmoid(ga_ref[...].astype(F32)) * pa
              + jax.nn.sigmoid(gb_ref[...].astype(F32)) * pb)
    o_ref[...] = x_ref[...] + jnp.dot(merged.astype(BF16), wo_ref[...], preferred_element_type=F32)


def _mix(x, oa, ob, u, wa, wb, wo, *, tm):
    n, d = x.shape
    const = lambda i: (0, 0)
    return pl.pallas_call(
        _mix_kernel,
        grid=(n // tm,),
        in_specs=[
            pl.BlockSpec((tm, d), lambda i: (i, 0)),
            pl.BlockSpec((tm, A_GW), lambda i: (i, 0)),
            pl.BlockSpec((tm, G_QK), lambda i: (i, 0)),
            pl.BlockSpec((tm, d), lambda i: (i, GATE_OFF // d)),
            pl.BlockSpec((tm, d), lambda i: (i, GATE_OFF // d + 1)),
            pl.BlockSpec((A_GW, d), const),
            pl.BlockSpec((G_QK, d), const),
            pl.BlockSpec((d, d), const),
        ],
        out_specs=pl.BlockSpec((tm, d), lambda i: (i, 0)),
        out_shape=jax.ShapeDtypeStruct((n, d), F32),
        compiler_params=_cparams(("parallel",)),
        name="mix",
    )(x, oa, ob, u, u, wa, wb, wo)


def _w_in_sources():
    aw = len(A_GROUPS) * A_GW
    src_q, src_k, src_v = 0, aw, 2 * aw
    src_qkvb = 3 * aw
    src_z = src_qkvb + G_CONV_CH
    src_ba = src_z + G_QK
    src_gate = src_ba + 2 * G_HEADS
    blocks = [src_qkvb + W_CB * t for t in range(G_CONV_CH // W_CB)]
    blocks += [src_z + W_CB * t for t in range(G_QK // W_CB)]
    blocks += [src_gate + W_CB * t for t in range(2 * D_MODEL // W_CB)]
    for g in range(len(A_GROUPS)):
        blocks += [src_k + g * A_GW, src_v + g * A_GW]
    blocks += [src_q + W_CB * t for t in range(aw // W_CB)]
    assert len(blocks) * W_CB == U_WIDTH and A_GW == W_CB
    assert all(b % SUBLANE == 0 for b in blocks)
    return blocks, src_ba


def _permute_cast_kernel(src_ref, w_ref, o_ref):
    del src_ref
    o_ref[...] = w_ref[...].astype(BF16)


def _permute_w_in(wt):
    d = wt.shape[1]
    blocks, _ = _w_in_sources()
    return pl.pallas_call(
        _permute_cast_kernel,
        grid_spec=pltpu.PrefetchScalarGridSpec(
            num_scalar_prefetch=1,
            grid=(len(blocks),),
            in_specs=[pl.BlockSpec((pl.Element(W_CB), pl.Element(d)),
                                   lambda j, src: (pl.multiple_of(src[j], SUBLANE), 0))],
            out_specs=pl.BlockSpec((W_CB, d), lambda j, src: (j, 0)),
        ),
        out_shape=jax.ShapeDtypeStruct((U_WIDTH, d), BF16),
        compiler_params=_cparams(("parallel",)),
        name="permute_w_in",
    )(jnp.asarray(blocks, jnp.int32), wt)


def _prep_weights(w_in, gdn_a_log, gdn_dt_bias):
    wt = w_in.T
    w_main = _permute_w_in(wt)
    src_ba = _w_in_sources()[1]
    ba_pad = jnp.pad(wt[src_ba:src_ba + 2 * G_HEADS, :], ((0, LANE - 2 * G_HEADS), (0, 0)))
    ba_hi = ba_pad.astype(BF16)
    ba_lo = (ba_pad - ba_hi.astype(F32)).astype(BF16)
    pad = (G_HEADS, LANE - 2 * G_HEADS)
    alog_t = jnp.pad(gdn_a_log.astype(F32), pad).reshape(1, LANE)
    dtb_t = jnp.pad(gdn_dt_bias.astype(F32), pad).reshape(1, LANE)
    return w_main, ba_hi, ba_lo, alog_t, dtb_t


def _kv_rows(u3, group, keep):
    b, t, _ = u3.shape
    off = KV_OFF + group * 2 * A_GW
    return u3[:, t - keep:, off:off + 2 * A_GW].reshape(1, b, keep, 2, A_HPG, A_E)


def _layer(x, batch, seq, conv0, s0, attn_fn, p, *, tm, gdn_bpb, b_dtype):
    tm = min(tm, x.shape[0])
    gdn_bpb = min(gdn_bpb, batch)
    tail = b_dtype != F32
    x1 = _ffn(x, p["norm_ffn1"], p["w_ffn1_gu"], p["w_ffn1_down"], tm=tm)
    ub, ua, ba, *rest = _inproj(x1, p["norm_mix"], p["w_main"], p["ba_hi"], p["ba_lo"],
                                tm=tm, b_dtype=b_dtype, tail=tail)
    oa = attn_fn(ua)
    ob, s_new = _gdn(ub, ba, conv0, s0, p["conv_w"], p["alog_t"], p["dtb_t"], p["gdn_norm"],
                     batch=batch, seq=seq, bpb=gdn_bpb)
    x2 = _mix(x1, oa.reshape(batch * seq, A_GW), ob.reshape(batch * seq, G_QK), ub,
              p["w_proj_a"], p["w_proj_b"], p["w_out"], tm=min(tm, 512))
    y = _ffn(x2, p["norm_ffn2"], p["w_ffn2_gu"], p["w_ffn2_down"], p["norm_out"], tm=tm)
    u3 = ua.reshape(batch, seq, UA_WIDTH)
    rows = [_kv_rows(u3, g, min(win, seq)) for g, (win, _) in enumerate(A_GROUPS)]
    if tail:
        assert seq % tm == 0 and seq >= SUBLANE
        last = rest[0].reshape(batch, seq // tm, SUBLANE, G_CONV_CH)[:, -1]
        conv_new = last[:, SUBLANE - (G_CONV - 1):, :][None]
    else:
        ub3 = ub.reshape(batch, seq, UB_WIDTH)
        conv_new = ub3[:, seq - (G_CONV - 1):, QKVB_OFF:QKVB_OFF + G_CONV_CH][None]
    return y.reshape(batch, seq, D_MODEL), rows, conv_new, s_new[None]


def kernel(x_prompt, x_sample, cache_kv_w128, cache_kv_w512, cache_kv_w2048, state_conv, state_ssm,
           norm_ffn1, w_ffn1_gu, w_ffn1_down, norm_mix, w_in, conv_w, gdn_a_log, gdn_dt_bias, gdn_norm,
           w_proj_a, w_proj_b, w_out, norm_ffn2, w_ffn2_gu, w_ffn2_down, norm_out):
    assert w_in.shape[0] == 1, "single layer"
    n_heads = len(A_GROUPS) * A_HPG
    slopes = jnp.exp2(-8.0 * jnp.arange(1, n_heads + 1, dtype=F32) / n_heads)
    w_main, ba_hi, ba_lo, alog_t, dtb_t = _prep_weights(w_in[0], gdn_a_log[0], gdn_dt_bias[0])
    p = dict(
        norm_ffn1=norm_ffn1[0], w_ffn1_gu=w_ffn1_gu[0].astype(BF16), w_ffn1_down=w_ffn1_down[0].astype(BF16),
        norm_mix=norm_mix[0], w_main=w_main, ba_hi=ba_hi, ba_lo=ba_lo, conv_w=conv_w[0],
        alog_t=alog_t, dtb_t=dtb_t, gdn_norm=gdn_norm[0],
        w_proj_a=w_proj_a[0].astype(BF16), w_proj_b=w_proj_b[0].astype(BF16), w_out=w_out[0].astype(BF16),
        norm_ffn2=norm_ffn2[0], w_ffn2_gu=w_ffn2_gu[0].astype(BF16), w_ffn2_down=w_ffn2_down[0].astype(BF16),
        norm_out=norm_out,
    )
    bp, tp, d = x_prompt.shape
    bs, ts, _ = x_sample.shape

    conv0_p = jnp.zeros((bp, G_CONV - 1, G_CONV_CH), F32)
    ssm0_p = jnp.zeros((bp, G_HEADS, G_DK, G_DV), F32)
    yp, rows_p, conv_p, ssm_p = _layer(
        x_prompt.reshape(bp * tp, d), bp, tp, conv0_p, ssm0_p,
        functools.partial(_attn_prompt, slopes, batch=bp, seq=tp), p, tm=2048, gdn_bpb=2, b_dtype=BF16)

    caches = (cache_kv_w128[0], cache_kv_w512[0], cache_kv_w2048[0])
    ys, rows_s, conv_s, ssm_s = _layer(
        x_sample.reshape(bs * ts, d), bs, ts, state_conv[0], state_ssm[0],
        lambda u: _attn_sample(slopes, u, caches, batch=bs, dec_seq=ts), p, tm=512, gdn_bpb=2, b_dtype=F32)

    return (yp, ys, rows_p[0], rows_p[1], rows_p[2], conv_p, ssm_p,
            rows_s[0], rows_s[1], rows_s[2], conv_s, ssm_s)
```

```python
import functools

import jax
import jax.numpy as jnp
from jax import lax
from jax.experimental import pallas as pl
from jax.experimental.pallas import tpu as pltpu

F32 = jnp.float32
BF16 = jnp.bfloat16
HIGHEST = lax.Precision.HIGHEST

D_MODEL = 1024
D_FF = 2816
NORM_EPS = 1e-6
A_GROUPS = ((128, 1), (512, 4), (2048, 16))
A_HPG = 4
A_E = 128
A_BLOCK = 128
A_GW = A_HPG * A_E
A_ST = 2048
G_HEADS = 8
G_DK = 128
G_DV = 128
G_QK = G_HEADS * G_DK
G_CONV_CH = 3 * G_QK
G_CONV = 4
G_CHUNK = 64

QKVB_OFF = 0
Z_OFF = 3072
GATE_OFF = 4096
UB_WIDTH = 6144
KV_OFF = 0
QA_OFF = 3072
UA_WIDTH = 4608
U_WIDTH = UB_WIDTH + UA_WIDTH
W_CB = 512

VMEM_LIMIT = 56 * 1024 * 1024
LANE = 128
SUBLANE = 8


def _cparams(sem):
    return pltpu.CompilerParams(dimension_semantics=sem, vmem_limit_bytes=VMEM_LIMIT)


def _rms_rows(x, w):
    return x * lax.rsqrt(jnp.mean(x * x, axis=-1, keepdims=True) + NORM_EPS) * w


def _ffn_kernel(x_ref, nw_ref, wg_ref, wu_ref, wd_ref, *rest, n_ff, final_norm):
    if final_norm:
        onw_ref, o_ref, hb_ref = rest
    else:
        o_ref, hb_ref = rest
    j = pl.program_id(1)

    @pl.when(j == 0)
    def _():
        hb_ref[...] = _rms_rows(x_ref[...], nw_ref[...]).astype(BF16)
        o_ref[...] = jnp.zeros_like(o_ref)

    hb = hb_ref[...]
    g = jnp.dot(hb, wg_ref[...], preferred_element_type=F32)
    u = jnp.dot(hb, wu_ref[...], preferred_element_type=F32)
    a = (g * jax.nn.sigmoid(g) * u).astype(BF16)
    o_ref[...] += jnp.dot(a, wd_ref[...], preferred_element_type=F32)

    @pl.when(j == n_ff - 1)
    def _():
        y = x_ref[...] + 0.5 * o_ref[...]
        if final_norm:
            y = _rms_rows(y, onw_ref[...])
        o_ref[...] = y


def _ffn(x, norm_w, w_gu, w_down, out_norm_w=None, *, tm, tf=256):
    n, d = x.shape
    n_ff = D_FF // tf
    final_norm = out_norm_w is not None
    in_specs = [
        pl.BlockSpec((tm, d), lambda i, j: (i, 0)),
        pl.BlockSpec((1, d), lambda i, j: (0, 0)),
        pl.BlockSpec((d, tf), lambda i, j: (0, j)),
        pl.BlockSpec((d, tf), lambda i, j: (0, j + n_ff)),
        pl.BlockSpec((tf, d), lambda i, j: (j, 0)),
    ]
    args = [x, norm_w.reshape(1, d), w_gu, w_gu, w_down]
    if final_norm:
        in_specs.append(pl.BlockSpec((1, d), lambda i, j: (0, 0)))
        args.append(out_norm_w.reshape(1, d))
    return pl.pallas_call(
        functools.partial(_ffn_kernel, n_ff=n_ff, final_norm=final_norm),
        grid=(n // tm, n_ff),
        in_specs=in_specs,
        out_specs=pl.BlockSpec((tm, d), lambda i, j: (i, 0)),
        out_shape=jax.ShapeDtypeStruct((n, d), F32),
        scratch_shapes=[pltpu.VMEM((tm, d), BF16)],
        compiler_params=_cparams(("parallel", "arbitrary")),
        name="ffn_final" if final_norm else "ffn",
    )(*args)


_DN_NT = (((1,), (1,)), ((), ()))


def _inproj_kernel(x_ref, nw_ref, w_ref, wbh_ref, wbl_ref, ub_ref, ua_ref, ba_ref, *rest, nb, nq, tail):
    if tail:
        tail_ref, hb_ref = rest
    else:
        (hb_ref,) = rest
    j = pl.program_id(1)

    @pl.when(j == 0)
    def _():
        h = _rms_rows(x_ref[...], nw_ref[...])
        hb = h.astype(BF16)
        hl = (h - hb.astype(F32)).astype(BF16)
        hb_ref[...] = hb
        ba_ref[...] = (lax.dot_general(hb, wbh_ref[...], _DN_NT, preferred_element_type=F32)
                       + lax.dot_general(hb, wbl_ref[...], _DN_NT, preferred_element_type=F32)
                       + lax.dot_general(hl, wbh_ref[...], _DN_NT, preferred_element_type=F32))

    def project(rows):
        return lax.dot_general(hb_ref[rows, :], w_ref[...], _DN_NT, preferred_element_type=F32)

    @pl.when(j < nb)
    def _():
        ub_ref[...] = project(slice(None)).astype(ub_ref.dtype)

    @pl.when(j >= nb)
    def _():
        ua_ref[...] = project(slice(None))

    if tail:
        @pl.when(j < nq)
        def _():
            tm = hb_ref.shape[0]
            tail_ref[...] = project(slice(tm - 2 * SUBLANE, tm))[SUBLANE:, :]


def _inproj(x, norm_w, w_main, w_ba_hi, w_ba_lo, *, tm, b_dtype, tail, tn=512):
    n, d = x.shape
    nb, nq = UB_WIDTH // tn, G_CONV_CH // tn
    assert nb * tn == UB_WIDTH and nq * tn == G_CONV_CH and U_WIDTH % tn == 0 and QKVB_OFF == 0
    out_specs = [
        pl.BlockSpec((tm, tn), lambda i, j: (i, jnp.minimum(j, nb - 1))),
        pl.BlockSpec((tm, tn), lambda i, j: (i, jnp.maximum(j - nb, 0))),
        pl.BlockSpec((tm, LANE), lambda i, j: (i, 0)),
    ]
    out_shape = [
        jax.ShapeDtypeStruct((n, UB_WIDTH), b_dtype),
        jax.ShapeDtypeStruct((n, UA_WIDTH), F32),
        jax.ShapeDtypeStruct((n, LANE), F32),
    ]
    if tail:
        out_specs.append(pl.BlockSpec((SUBLANE, tn), lambda i, j: (i, jnp.minimum(j, nq - 1))))
        out_shape.append(jax.ShapeDtypeStruct((n // tm * SUBLANE, G_CONV_CH), F32))
    return pl.pallas_call(
        functools.partial(_inproj_kernel, nb=nb, nq=nq, tail=tail),
        grid=(n // tm, U_WIDTH // tn),
        in_specs=[
            pl.BlockSpec((tm, d), lambda i, j: (i, 0), pipeline_mode=pl.Buffered(1)),
            pl.BlockSpec((1, d), lambda i, j: (0, 0)),
            pl.BlockSpec((tn, d), lambda i, j: (j, 0)),
            pl.BlockSpec((LANE, d), lambda i, j: (0, 0)),
            pl.BlockSpec((LANE, d), lambda i, j: (0, 0)),
        ],
        out_specs=out_specs,
        out_shape=out_shape,
        scratch_shapes=[pltpu.VMEM((tm, d), BF16)],
        compiler_params=_cparams(("parallel", "arbitrary")),
        name="inproj",
    )(x, norm_w.reshape(1, d), w_main, w_ba_hi, w_ba_lo)


def _rows(start, size, stride):
    return pl.ds(start, size) if stride == 1 else pl.ds(start, size, stride=stride)


def _attn_prompt_kernel(slopes_ref, *refs, n_st, keeps):
    n_g = len(A_GROUPS)
    ins, o_ref = refs[:5 * n_g], refs[5 * n_g]
    kv_refs, scr = refs[5 * n_g + 1:6 * n_g + 1], refs[6 * n_g + 1:]
    st = pl.program_id(1)
    hh = pl.program_id(2)

    @pl.when(st == n_st - 1)
    def _():
        for g in range(n_g):
            keep = keeps[g]
            kc_ref, vc_ref = ins[5 * g + 1], ins[5 * g + 2]
            kv_refs[g][pl.ds(hh, keep, stride=SUBLANE), :] = kc_ref[A_ST - keep:, :]
            kv_refs[g][pl.ds(A_HPG + hh, keep, stride=SUBLANE), :] = vc_ref[A_ST - keep:, :]

    scale = A_E ** -0.5
    qi = lax.broadcasted_iota(jnp.int32, (A_BLOCK, 2 * A_BLOCK), 0)
    kj = lax.broadcasted_iota(jnp.int32, (A_BLOCK, 2 * A_BLOCK), 1)
    delta = A_BLOCK + qi - kj
    band = jnp.logical_and(delta >= 0, delta <= A_BLOCK)
    band_first = jnp.logical_and(band, jnp.logical_or(kj >= A_BLOCK, st > 0))
    dn_nt = (((1,), (1,)), ((), ()))
    for g, (win, dil) in enumerate(A_GROUPS):
        q_ref, kc_ref, vc_ref, kp_ref, vp_ref = ins[5 * g:5 * g + 5]
        og_ref, lg_ref = scr[2 * g:2 * g + 2]
        slope = slopes_ref[g * A_HPG + hh]
        bias = -slope * (dil * delta).astype(F32)
        bias_mid = jnp.where(band, bias, -jnp.inf)
        bias_first = jnp.where(band_first, bias, -jnp.inf)

        def two_blocks(prev_ref, cur_ref, r, j):
            if j == 0:
                return jnp.concatenate([prev_ref[_rows(r, A_BLOCK, dil), :],
                                        cur_ref[_rows(r, A_BLOCK, dil), :]], axis=0)
            return cur_ref[_rows(r + (j - 1) * A_BLOCK * dil, 2 * A_BLOCK, dil), :]

        for r in range(dil):
            for j in range(A_ST // (A_BLOCK * dil)):
                row0 = r + j * A_BLOCK * dil
                q = q_ref[_rows(row0, A_BLOCK, dil), :].astype(BF16)
                k2 = two_blocks(kp_ref, kc_ref, r, j).astype(BF16)
                v2 = two_blocks(vp_ref, vc_ref, r, j).astype(BF16)
                s = lax.dot_general(q, k2, dn_nt, preferred_element_type=F32) * scale
                s = s + (bias_first if j == 0 else bias_mid)
                m = jnp.max(s, axis=-1, keepdims=True)
                p = jnp.exp(s - m)
                den = jnp.sum(p, axis=-1, keepdims=True)
                acc = jnp.dot(p.astype(BF16), v2, preferred_element_type=F32)
                og_ref[_rows(row0, A_BLOCK, dil), :] = acc / den
                lg_ref[_rows(row0, A_BLOCK, dil), :] = jnp.broadcast_to(m + jnp.log(den), (A_BLOCK, A_E))
    chunk = 2 * A_BLOCK
    for c in range(A_ST // chunk):
        rs = slice(c * chunk, (c + 1) * chunk)
        ls = [scr[2 * g + 1][rs, :] for g in range(n_g)]
        mx = functools.reduce(jnp.maximum, ls)
        ws = [jnp.exp(l - mx) for l in ls]
        num = sum(w * scr[2 * g][rs, :] for g, w in enumerate(ws))
        o_ref[rs, :] = num / sum(ws)


def _attn_prompt(slopes, u, *, batch, seq):
    assert seq % A_ST == 0
    u3 = u.reshape(batch, seq, UA_WIDTH)
    in_specs = [pl.BlockSpec(memory_space=pltpu.SMEM)]
    args = [slopes]
    scratch = []
    keeps = tuple(min(win, seq) for win, _ in A_GROUPS)
    assert max(keeps) <= A_ST
    for g, (win, dil) in enumerate(A_GROUPS):
        assert win == A_BLOCK * dil and A_ST % win == 0
        qcb = (QA_OFF + g * A_GW) // A_E
        kcb = (KV_OFF + g * 2 * A_GW) // A_E
        vcb = kcb + A_HPG
        per = A_ST // win

        def cur(cb):
            return pl.BlockSpec((None, A_ST, A_E), lambda b, st, hh, cb=cb: (b, st, cb + hh))

        def prv(cb, win=win, per=per):
            return pl.BlockSpec((None, win, A_E),
                                lambda b, st, hh, cb=cb: (b, jnp.maximum(st * per - 1, 0), cb + hh))

        in_specs += [cur(qcb), cur(kcb), cur(vcb), prv(kcb), prv(vcb)]
        args += [u3] * 5
        scratch += [pltpu.VMEM((A_ST, A_E), F32), pltpu.VMEM((A_ST, A_E), F32)]
    n_st = seq // A_ST
    out_specs = [pl.BlockSpec((None, A_ST, A_E), lambda b, st, hh: (b, st, hh))]
    out_shape = [jax.ShapeDtypeStruct((batch, seq, A_GW), F32)]
    for keep in keeps:
        out_specs.append(pl.BlockSpec((None, keep * SUBLANE, A_E), lambda b, st, hh: (b, 0, 0),
                                      pipeline_mode=pl.Buffered(1)))
        out_shape.append(jax.ShapeDtypeStruct((batch, keep * SUBLANE, A_E), F32))
    outs = pl.pallas_call(
        functools.partial(_attn_prompt_kernel, n_st=n_st, keeps=keeps),
        grid=(batch, n_st, A_HPG),
        in_specs=in_specs,
        out_specs=out_specs,
        out_shape=out_shape,
        scratch_shapes=scratch,
        compiler_params=_cparams(("parallel", "arbitrary", "arbitrary")),
        name="attn_prompt",
    )(*args)
    rows = [kv.reshape(1, batch, keep, 2, A_HPG, A_E) for kv, keep in zip(outs[1:], keeps)]
    return outs[0], rows


def _pick_rows(rows):
    sub = lax.broadcasted_iota(jnp.int32, (SUBLANE, A_E), 0)
    out = jnp.zeros((SUBLANE, A_E), F32)
    for i, r in enumerate(rows):
        out = jnp.where(sub == i, jnp.broadcast_to(r, (SUBLANE, A_E)), out)
    return out


def _attn_sample_kernel(slopes_ref, q_ref, kv_ref, c0_ref, c1_ref, c2_ref, o_ref, *, dec_seq, nseq):
    scale = A_E ** -0.5
    caches = (c0_ref, c1_ref, c2_ref)
    n_keys = A_BLOCK * SUBLANE
    sub = lax.broadcasted_iota(jnp.int32, (SUBLANE, n_keys), 0)
    lane = lax.broadcasted_iota(jnp.int32, (SUBLANE, n_keys), 1)
    tile_row = lane & (SUBLANE - 1)
    m_idx = lane >> 3
    own_k = tile_row == sub
    sub1 = lax.broadcasted_iota(jnp.int32, (SUBLANE, 1), 0)
    dn_nt = (((1,), (1,)), ((), ()))
    slope8s = []
    for g in range(len(A_GROUPS)):
        slope8 = jnp.zeros((SUBLANE, 1), F32)
        for h in range(A_HPG):
            slope8 = jnp.where(sub1 == h, slopes_ref[g * A_HPG + h], slope8)
        slope8s.append(slope8)

    def keys(bb, s, g):
        res = 0 if A_GROUPS[g][1] == 1 else s
        return caches[g][bb, :, res, :, :].reshape(n_keys, A_E).astype(BF16)

    units = [(bb, s, g) for bb in range(nseq) for s in range(dec_seq) for g in range(len(A_GROUPS))]
    q8s = [_pick_rows([q_ref[bb * dec_seq + s:bb * dec_seq + s + 1, g * A_GW + h * A_E:g * A_GW + (h + 1) * A_E]
                       for h in range(A_HPG)]) for bb, s, g in units]
    scs = [lax.dot_general(q8.astype(BF16), keys(*u), dn_nt, preferred_element_type=F32) * scale
           for q8, u in zip(q8s, units)]
    ms, dens, pvs, news = [], [], [], []
    for (bb, s, g), q8, sc in zip(units, q8s, scs):
        win, dil = A_GROUPS[g]
        row = bb * dec_seq + s
        slope8 = slope8s[g]
        if dil == 1:
            dist = (win + s - m_idx).astype(F32)
            valid = jnp.logical_and(own_k, m_idx >= s)
            new_rows = [(bb * dec_seq + t, float(s - t)) for t in range(s + 1)]
        else:
            dist = (dil * (win // dil - m_idx)).astype(F32)
            valid = own_k
            new_rows = [(row, 0.0)]
        sc = jnp.where(valid, sc - slope8 * dist, -jnp.inf)
        m = jnp.max(sc, axis=-1, keepdims=True)
        koff = g * 2 * A_GW
        new = []
        for nrow, ndist in new_rows:
            k8 = _pick_rows([kv_ref[nrow:nrow + 1, koff + h * A_E:koff + (h + 1) * A_E]
                             for h in range(A_HPG)])
            v8 = _pick_rows([kv_ref[nrow:nrow + 1, koff + A_GW + h * A_E:koff + A_GW + (h + 1) * A_E]
                             for h in range(A_HPG)])
            s_n = jnp.sum(k8 * q8, axis=-1, keepdims=True) * scale - slope8 * ndist
            m = jnp.maximum(m, s_n)
            new.append((s_n, v8))
        p = jnp.exp(sc - m)
        ms.append(m)
        dens.append(jnp.sum(p, axis=-1, keepdims=True))
        pvs.append(pltpu.roll(p, A_HPG, 1).astype(BF16))
        news.append(new)
    accs = [jnp.dot(pv, keys(*u), preferred_element_type=F32) for pv, u in zip(pvs, units)]

    out_tiles = [jnp.zeros((nseq * dec_seq, A_E), F32) for _ in range(A_HPG)]
    out_sub = lax.broadcasted_iota(jnp.int32, (nseq * dec_seq, A_E), 0)
    n_g = len(A_GROUPS)
    for ui in range(0, len(units), n_g):
        bb, s, _ = units[ui]
        outs, lses = [], []
        for m, den, acc, new in zip(ms[ui:ui + n_g], dens[ui:ui + n_g], accs[ui:ui + n_g], news[ui:ui + n_g]):
            for s_n, v8 in new:
                p_n = jnp.exp(s_n - m)
                den = den + p_n
                acc = acc + p_n * v8
            outs.append(acc / den)
            lses.append(m + jnp.log(den))
        mx = functools.reduce(jnp.maximum, lses)
        ws = [jnp.exp(l - mx) for l in lses]
        o8 = sum(w * o for w, o in zip(ws, outs)) / sum(ws)
        for h in range(A_HPG):
            out_tiles[h] = jnp.where(out_sub == bb * dec_seq + s,
                                     jnp.broadcast_to(o8[h:h + 1, :], (nseq * dec_seq, A_E)), out_tiles[h])
    for h in range(A_HPG):
        o_ref[:, h * A_E:(h + 1) * A_E] = out_tiles[h]


def _attn_sample(slopes, u, caches, *, batch, dec_seq):
    nseq = SUBLANE // dec_seq
    assert nseq * dec_seq == SUBLANE and batch % nseq == 0
    views, specs = [], []
    for (win, dil), cache in zip(A_GROUPS, caches):
        wb = cache.shape[1]
        assert wb == win and wb // dil == A_BLOCK and (dil == 1 or dec_seq <= dil)
        views.append(cache.reshape(batch, wb // dil, dil, SUBLANE, A_E))
        res = 1 if dil == 1 else dec_seq
        specs.append(pl.BlockSpec((nseq, A_BLOCK, res, SUBLANE, A_E), lambda i: (i, 0, 0, 0, 0)))
    n_q = len(A_GROUPS) * A_GW
    rows = nseq * dec_seq
    return pl.pallas_call(
        functools.partial(_attn_sample_kernel, dec_seq=dec_seq, nseq=nseq),
        grid=(batch // nseq,),
        in_specs=[
            pl.BlockSpec(memory_space=pltpu.SMEM),
            pl.BlockSpec((rows, n_q), lambda i: (i, QA_OFF // n_q)),
            pl.BlockSpec((rows, 2 * n_q), lambda i: (i, KV_OFF // (2 * n_q))),
        ] + specs,
        out_specs=pl.BlockSpec((rows, A_GW), lambda i: (i, 0)),
        out_shape=jax.ShapeDtypeStruct((batch * dec_seq, A_GW), F32),
        compiler_params=_cparams(("parallel",)),
        name="attn_sample",
    )(slopes, u, u, *views)


def _bdot(a, b, dn=(((1,), (0,)), ((), ()))):
    return lax.dot_general(a.astype(BF16), b.astype(BF16), dn, preferred_element_type=F32)


def _gdn_kernel(qkv_ref, z_ref, ba_ref, cw_ref, alog_ref, dtb_ref, nw_ref, conv0_ref, s0_ref,
                o_ref, sout_ref, cb_ref, s_ref, *, tb, cc, nc, bpb, gs):
    n = pl.program_id(1)

    @pl.when(n == 0)
    def _():
        s_ref[...] = s0_ref[...]
        cb_ref[:, 0:SUBLANE, :] = conv0_ref[...]

    rows = lax.broadcasted_iota(jnp.int32, (cc, 1), 0)
    live = rows < tb
    ii = lax.broadcasted_iota(jnp.int32, (cc, cc), 0)
    jj = lax.broadcasted_iota(jnp.int32, (cc, cc), 1)
    incl = ii >= jj
    strict = ii > jj
    tri = incl.astype(F32)
    cw = cw_ref[...]
    first = SUBLANE - (G_CONV - 1)
    dn_nt = (((1,), (1,)), ((), ()))
    dn_tn = (((0,), (0,)), ((), ()))
    eye = ii == jj

    for b0 in range(0, bpb, gs):
        qs, ks, kbs, decays, rhss, e_gcs, e_rests, e_ends = [], [], [], [], [], [], [], []
        idx = range(gs * G_HEADS)
        for bi in range(b0, b0 + gs):
            x = qkv_ref[bi].astype(F32)
            cb_ref[bi, SUBLANE:SUBLANE + tb, :] = x
            if tb < cc:
                cb_ref[bi, SUBLANE + tb:SUBLANE + cc, :] = jnp.zeros((cc - tb, G_CONV_CH), F32)
            y = cb_ref[bi, SUBLANE:SUBLANE + cc, :] * cw[G_CONV - 1:G_CONV, :]
            for j in range(G_CONV - 2, -1, -1):
                y = y + cb_ref[bi, first + j:first + j + cc, :] * cw[j:j + 1, :]
            if nc > 1:
                cb_ref[bi, 0:SUBLANE, :] = x[tb - SUBLANE:tb, :]
            y = y * jax.nn.sigmoid(y)

            ba = ba_ref[bi]
            if tb < cc:
                ba = jnp.concatenate([ba, jnp.zeros((cc - tb, LANE), F32)], axis=0)
            beta_t = jnp.where(live, jax.nn.sigmoid(ba), 0.0)
            xs = ba + dtb_ref[...]
            softplus = jnp.maximum(xs, 0.0) + jnp.log(1.0 + jnp.exp(-jnp.abs(xs)))
            g_t = jnp.where(live, -jnp.exp(alog_ref[...]) * softplus, 0.0)
            gc = jnp.dot(tri, g_t, precision=HIGHEST, preferred_element_type=F32)
            gc_t = jnp.concatenate([gc, jnp.zeros((LANE - cc, LANE), F32)], axis=0).T
            gc_last = gc[cc - 1:cc, :]
            e_gc = jnp.exp(gc)
            e_rest = jnp.exp(gc_last - gc)
            e_end = jnp.exp(gc_last)
            for h in range(G_HEADS):
                sl = slice(h * G_DK, (h + 1) * G_DK)
                gl = slice(G_HEADS + h, G_HEADS + h + 1)
                q = y[:, sl]
                k = y[:, G_QK + h * G_DK:G_QK + (h + 1) * G_DK]
                v = y[:, 2 * G_QK + h * G_DV:2 * G_QK + (h + 1) * G_DV]
                q = q * lax.rsqrt(jnp.sum(q * q, axis=-1, keepdims=True) + NORM_EPS) * (G_DK ** -0.5)
                k = k * lax.rsqrt(jnp.sum(k * k, axis=-1, keepdims=True) + NORM_EPS)
                if tb < cc:
                    q = jnp.where(live, q, 0.0)
                    k = jnp.where(live, k, 0.0)
                    v = jnp.where(live, v, 0.0)
                beta = beta_t[:, h:h + 1]
                kb = k * beta
                qs.append(q)
                ks.append(k)
                kbs.append(kb)
                decays.append(jnp.exp(jnp.where(
                    incl, gc[:, gl] - gc_t[G_HEADS + h:G_HEADS + h + 1, 0:cc], -jnp.inf)))
                rhss.append(jnp.concatenate([v * beta, kb * e_gc[:, gl]], axis=1))
                e_gcs.append(e_gc[:, gl])
                e_rests.append(e_rest[:, gl])
                e_ends.append(e_end[:, gl])

        kq = [_bdot(jnp.concatenate([kbs[i], qs[i]], axis=0), ks[i], dn_nt) for i in idx]
        a_low = [jnp.where(strict, kq[i][:cc] * decays[i], 0.0) for i in idx]
        qk = [kq[i][cc:] * decays[i] for i in idx]
        inv = [jnp.where(eye, 1.0, -a_low[i]) for i in idx]
        pw = a_low
        span = 2
        while span < cc:
            pw = [_bdot(pw[i], pw[i]) for i in idx]
            inv = [inv[i] + _bdot(pw[i], inv[i]) for i in idx]
            span *= 2
        rhss = [_bdot(inv[i], rhss[i]) for i in idx]
        s_old = [s_ref[b0 + i // G_HEADS, i % G_HEADS] for i in idx]
        ws = [_bdot(jnp.concatenate([rhss[i][:, G_DV:], qs[i] * e_gcs[i]], axis=0), s_old[i]) for i in idx]
        v_new = [rhss[i][:, :G_DV] - ws[i][:cc] for i in idx]
        for i in idx:
            s_ref[b0 + i // G_HEADS, i % G_HEADS] = (
                s_old[i] * e_ends[i] + _bdot(ks[i] * e_rests[i], v_new[i], dn_tn))
        o = [ws[i][cc:] + _bdot(qk[i], v_new[i]) for i in idx]
        for i in idx:
            bi, h = b0 + i // G_HEADS, i % G_HEADS
            sl = slice(h * G_DV, (h + 1) * G_DV)
            o_h = o[i] * lax.rsqrt(jnp.mean(o[i] * o[i], axis=-1, keepdims=True) + NORM_EPS) * nw_ref[...]
            zz = z_ref[bi, :, sl].astype(F32)
            o_ref[bi, :, sl] = o_h[:tb] * (zz * jax.nn.sigmoid(zz))

    @pl.when(n == nc - 1)
    def _():
        sout_ref[...] = s_ref[...]


def _gdn(u, ba, conv0, s0, conv_w, alog_t, dtb_t, norm_w, *, batch, seq, bpb, gs=None):
    gs = bpb if gs is None else min(gs, bpb)
    assert bpb % gs == 0
    cc = G_CHUNK if seq >= G_CHUNK else SUBLANE
    tb = min(seq, cc)
    nc = seq // tb
    assert nc * tb == seq and (nc == 1 or tb == cc) and batch % bpb == 0
    uv = u.reshape(batch, seq, UB_WIDTH)
    bav = ba.reshape(batch, seq, LANE)
    conv0p = jnp.concatenate(
        [jnp.zeros((batch, SUBLANE - (G_CONV - 1), G_CONV_CH), F32), conv0.astype(F32)], axis=1)
    const2 = lambda b, n: (0, 0)
    return pl.pallas_call(
        functools.partial(_gdn_kernel, tb=tb, cc=cc, nc=nc, bpb=bpb, gs=gs),
        grid=(batch // bpb, nc),
        in_specs=[
            pl.BlockSpec((bpb, tb, G_CONV_CH), lambda b, n: (b, n, QKVB_OFF // G_CONV_CH)),
            pl.BlockSpec((bpb, tb, G_QK), lambda b, n: (b, n, Z_OFF // G_QK)),
            pl.BlockSpec((bpb, tb, LANE), lambda b, n: (b, n, 0)),
            pl.BlockSpec((G_CONV, G_CONV_CH), const2),
            pl.BlockSpec((1, LANE), const2),
            pl.BlockSpec((1, LANE), const2),
            pl.BlockSpec((1, G_DV), const2),
            pl.BlockSpec((bpb, SUBLANE, G_CONV_CH), lambda b, n: (b, 0, 0)),
            pl.BlockSpec((bpb, G_HEADS, G_DK, G_DV), lambda b, n: (b, 0, 0, 0)),
        ],
        out_specs=[
            pl.BlockSpec((bpb, tb, G_QK), lambda b, n: (b, n, 0)),
            pl.BlockSpec((bpb, G_HEADS, G_DK, G_DV), lambda b, n: (b, 0, 0, 0)),
        ],
        out_shape=[
            jax.ShapeDtypeStruct((batch, seq, G_QK), F32),
            jax.ShapeDtypeStruct((batch, G_HEADS, G_DK, G_DV), F32),
        ],
        scratch_shapes=[
            pltpu.VMEM((bpb, SUBLANE + cc, G_CONV_CH), F32),
            pltpu.VMEM((bpb, G_HEADS, G_DK, G_DV), F32),
        ],
        compiler_params=_cparams(("parallel", "arbitrary")),
        name="gdn",
    )(uv, uv, bav, conv_w, alog_t, dtb_t, norm_w.reshape(1, G_DV), conv0p, s0)


def _mix_kernel(x_ref, oa_ref, ob_ref, ga_ref, gb_ref, wa_ref, wb_ref, wo_ref, o_ref):
    pa = jnp.dot(oa_ref[...].astype(BF16), wa_ref[...], preferred_element_type=F32)
    pb = jnp.dot(ob_ref[...].astype(BF16), wb_ref[...], preferred_element_type=F32)
    merged = (jax.nn.sigmoid(ga_ref[...].astype(F32)) * pa
              + jax.nn.sigmoid(gb_ref[...].astype(F32)) * pb)
    o_ref[...] = x_ref[...] + jnp.dot(merged.astype(BF16), wo_ref[...], preferred_element_type=F32)


def _mix(x, oa, ob, u, wa, wb, wo, *, tm):
    n, d = x.shape
    const = lambda i: (0, 0)
    return pl.pallas_call(
        _mix_kernel,
        grid=(n // tm,),
        in_specs=[
            pl.BlockSpec((tm, d), lambda i: (i, 0)),
            pl.BlockSpec((tm, A_GW), lambda i: (i, 0)),
            pl.BlockSpec((tm, G_QK), lambda i: (i, 0)),
            pl.BlockSpec((tm, d), lambda i: (i, GATE_OFF // d)),
            pl.BlockSpec((tm, d), lambda i: (i, GATE_OFF // d + 1)),
            pl.BlockSpec((A_GW, d), const),
            pl.BlockSpec((G_QK, d), const),
            pl.BlockSpec((d, d), const),
        ],
        out_specs=pl.BlockSpec((tm, d), lambda i: (i, 0)),
        out_shape=jax.ShapeDtypeStruct((n, d), F32),
        compiler_params=_cparams(("parallel",)),
        name="mix",
    )(x, oa, ob, u, u, wa, wb, wo)


def _w_in_sources():
    aw = len(A_GROUPS) * A_GW
    src_q, src_k, src_v = 0, aw, 2 * aw
    src_qkvb = 3 * aw
    src_z = src_qkvb + G_CONV_CH
    src_ba = src_z + G_QK
    src_gate = src_ba + 2 * G_HEADS
    blocks = [src_qkvb + W_CB * t for t in range(G_CONV_CH // W_CB)]
    blocks += [src_z + W_CB * t for t in range(G_QK // W_CB)]
    blocks += [src_gate + W_CB * t for t in range(2 * D_MODEL // W_CB)]
    for g in range(len(A_GROUPS)):
        blocks += [src_k + g * A_GW, src_v + g * A_GW]
    blocks += [src_q + W_CB * t for t in range(aw // W_CB)]
    assert len(blocks) * W_CB == U_WIDTH and A_GW == W_CB
    assert all(b % SUBLANE == 0 for b in blocks)
    return blocks, src_ba


def _permute_cast_kernel(src_ref, w_ref, o_ref):
    del src_ref
    o_ref[...] = w_ref[...].astype(BF16)


def _permute_w_in(wt):
    d = wt.shape[1]
    blocks, _ = _w_in_sources()
    return pl.pallas_call(
        _permute_cast_kernel,
        grid_spec=pltpu.PrefetchScalarGridSpec(
            num_scalar_prefetch=1,
            grid=(len(blocks),),
            in_specs=[pl.BlockSpec((pl.Element(W_CB), pl.Element(d)),
                                   lambda j, src: (pl.multiple_of(src[j], SUBLANE), 0))],
            out_specs=pl.BlockSpec((W_CB, d), lambda j, src: (j, 0)),
        ),
        out_shape=jax.ShapeDtypeStruct((U_WIDTH, d), BF16),
        compiler_params=_cparams(("parallel",)),
        name="permute_w_in",
    )(jnp.asarray(blocks, jnp.int32), wt)


def _prep_weights(w_in, gdn_a_log, gdn_dt_bias):
    wt = w_in.T
    w_main = _permute_w_in(wt)
    src_ba = _w_in_sources()[1]
    ba_pad = jnp.pad(wt[src_ba:src_ba + 2 * G_HEADS, :], ((0, LANE - 2 * G_HEADS), (0, 0)))
    ba_hi = ba_pad.astype(BF16)
    ba_lo = (ba_pad - ba_hi.astype(F32)).astype(BF16)
    pad = (G_HEADS, LANE - 2 * G_HEADS)
    alog_t = jnp.pad(gdn_a_log.astype(F32), pad).reshape(1, LANE)
    dtb_t = jnp.pad(gdn_dt_bias.astype(F32), pad).reshape(1, LANE)
    return w_main, ba_hi, ba_lo, alog_t, dtb_t


def _kv_rows(u3, group, keep):
    b, t, _ = u3.shape
    off = KV_OFF + group * 2 * A_GW
    return u3[:, t - keep:, off:off + 2 * A_GW].reshape(1, b, keep, 2, A_HPG, A_E)


def _layer(x, batch, seq, conv0, s0, attn_fn, p, *, tm, gdn_bpb, b_dtype):
    tm = min(tm, x.shape[0])
    gdn_bpb = min(gdn_bpb, batch)
    tail = b_dtype != F32
    x1 = _ffn(x, p["norm_ffn1"], p["w_ffn1_gu"], p["w_ffn1_down"], tm=tm)
    ub, ua, ba, *rest = _inproj(x1, p["norm_mix"], p["w_main"], p["ba_hi"], p["ba_lo"],
                                tm=tm, b_dtype=b_dtype, tail=tail)
    oa, rows = attn_fn(ua)
    ob, s_new = _gdn(ub, ba, conv0, s0, p["conv_w"], p["alog_t"], p["dtb_t"], p["gdn_norm"],
                     batch=batch, seq=seq, bpb=gdn_bpb)
    x2 = _mix(x1, oa.reshape(batch * seq, A_GW), ob.reshape(batch * seq, G_QK), ub,
              p["w_proj_a"], p["w_proj_b"], p["w_out"], tm=min(tm, 512))
    y = _ffn(x2, p["norm_ffn2"], p["w_ffn2_gu"], p["w_ffn2_down"], p["norm_out"], tm=tm)
    if rows is None:
        u3 = ua.reshape(batch, seq, UA_WIDTH)
        rows = [_kv_rows(u3, g, min(win, seq)) for g, (win, _) in enumerate(A_GROUPS)]
    if tail:
        assert seq % tm == 0 and seq >= SUBLANE
        last = rest[0].reshape(batch, seq // tm, SUBLANE, G_CONV_CH)[:, -1]
        conv_new = last[:, SUBLANE - (G_CONV - 1):, :][None]
    else:
        ub3 = ub.reshape(batch, seq, UB_WIDTH)
        conv_new = ub3[:, seq - (G_CONV - 1):, QKVB_OFF:QKVB_OFF + G_CONV_CH][None]
    return y.reshape(batch, seq, D_MODEL), rows, conv_new, s_new[None]


def kernel(x_prompt, x_sample, cache_kv_w128, cache_kv_w512, cache_kv_w2048, state_conv, state_ssm,
           norm_ffn1, w_ffn1_gu, w_ffn1_down, norm_mix, w_in, conv_w, gdn_a_log, gdn_dt_bias, gdn_norm,
           w_proj_a, w_proj_b, w_out, norm_ffn2, w_ffn2_gu, w_ffn2_down, norm_out):
    assert w_in.shape[0] == 1, "single layer"
    n_heads = len(A_GROUPS) * A_HPG
    slopes = jnp.exp2(-8.0 * jnp.arange(1, n_heads + 1, dtype=F32) / n_heads)
    w_main, ba_hi, ba_lo, alog_t, dtb_t = _prep_weights(w_in[0], gdn_a_log[0], gdn_dt_bias[0])
    p = dict(
        norm_ffn1=norm_ffn1[0], w_ffn1_gu=w_ffn1_gu[0].astype(BF16), w_ffn1_down=w_ffn1_down[0].astype(BF16),
        norm_mix=norm_mix[0], w_main=w_main, ba_hi=ba_hi, ba_lo=ba_lo, conv_w=conv_w[0],
        alog_t=alog_t, dtb_t=dtb_t, gdn_norm=gdn_norm[0],
        w_proj_a=w_proj_a[0].astype(BF16), w_proj_b=w_proj_b[0].astype(BF16), w_out=w_out[0].astype(BF16),
        norm_ffn2=norm_ffn2[0], w_ffn2_gu=w_ffn2_gu[0].astype(BF16), w_ffn2_down=w_ffn2_down[0].astype(BF16),
        norm_out=norm_out,
    )
    bp, tp, d = x_prompt.shape
    bs, ts, _ = x_sample.shape

    conv0_p = jnp.zeros((bp, G_CONV - 1, G_CONV_CH), F32)
    ssm0_p = jnp.zeros((bp, G_HEADS, G_DK, G_DV), F32)
    yp, rows_p, conv_p, ssm_p = _layer(
        x_prompt.reshape(bp * tp, d), bp, tp, conv0_p, ssm0_p,
        functools.partial(_attn_prompt, slopes, batch=bp, seq=tp), p, tm=2048, gdn_bpb=4, b_dtype=BF16)

    caches = (cache_kv_w128[0], cache_kv_w512[0], cache_kv_w2048[0])
    ys, rows_s, conv_s, ssm_s = _layer(
        x_sample.reshape(bs * ts, d), bs, ts, state_conv[0], state_ssm[0],
        lambda u: (_attn_sample(slopes, u, caches, batch=bs, dec_seq=ts), None), p, tm=512, gdn_bpb=4, b_dtype=F32)

    return (yp, ys, rows_p[0], rows_p[1], rows_p[2], conv_p, ssm_p,
            rows_s[0], rows_s[1], rows_s[2], conv_s, ssm_s)
```

```python
import functools

import jax
import jax.numpy as jnp
from jax import lax
from jax.experimental import pallas as pl
from jax.experimental.pallas import tpu as pltpu

F32 = jnp.float32
BF16 = jnp.bfloat16
HIGHEST = lax.Precision.HIGHEST

D_MODEL = 1024
D_FF = 2816
NORM_EPS = 1e-6
A_GROUPS = ((128, 1), (512, 4), (2048, 16))
A_HPG = 4
A_E = 128
A_BLOCK = 128
A_GW = A_HPG * A_E
A_ST = 2048
G_HEADS = 8
G_DK = 128
G_DV = 128
G_QK = G_HEADS * G_DK
G_CONV_CH = 3 * G_QK
G_CONV = 4
G_CHUNK = 64

QKVB_OFF = 0
Z_OFF = 3072
GATE_OFF = 4096
UB_WIDTH = 6144
KV_OFF = 0
QA_OFF = 3072
UA_WIDTH = 4608
U_WIDTH = UB_WIDTH + UA_WIDTH
W_CB = 512

VMEM_LIMIT = 56 * 1024 * 1024
LANE = 128
SUBLANE = 8


def _cparams(sem):
    return pltpu.CompilerParams(dimension_semantics=sem, vmem_limit_bytes=VMEM_LIMIT)


def _rms_rows(x, w):
    return x * lax.rsqrt(jnp.mean(x * x, axis=-1, keepdims=True) + NORM_EPS) * w


def _ffn_kernel(x_ref, nw_ref, wg_ref, wu_ref, wd_ref, *rest, n_ff, final_norm):
    if final_norm:
        onw_ref, o_ref, hb_ref = rest
    else:
        o_ref, hb_ref = rest
    j = pl.program_id(1)

    @pl.when(j == 0)
    def _():
        hb_ref[...] = _rms_rows(x_ref[...], nw_ref[...]).astype(BF16)
        o_ref[...] = jnp.zeros_like(o_ref)

    hb = hb_ref[...]
    g = jnp.dot(hb, wg_ref[...].astype(BF16), preferred_element_type=F32)
    u = jnp.dot(hb, wu_ref[...].astype(BF16), preferred_element_type=F32)
    a = (g * jax.nn.sigmoid(g) * u).astype(BF16)
    o_ref[...] += jnp.dot(a, wd_ref[...].astype(BF16), preferred_element_type=F32)

    @pl.when(j == n_ff - 1)
    def _():
        y = x_ref[...] + 0.5 * o_ref[...]
        if final_norm:
            y = _rms_rows(y, onw_ref[...])
        o_ref[...] = y


def _ffn(x, norm_w, w_gu, w_down, out_norm_w=None, *, tm, tf=256):
    n, d = x.shape
    n_ff = D_FF // tf
    final_norm = out_norm_w is not None
    in_specs = [
        pl.BlockSpec((tm, d), lambda i, j: (i, 0)),
        pl.BlockSpec((1, d), lambda i, j: (0, 0)),
        pl.BlockSpec((d, tf), lambda i, j: (0, j)),
        pl.BlockSpec((d, tf), lambda i, j: (0, j + n_ff)),
        pl.BlockSpec((tf, d), lambda i, j: (j, 0)),
    ]
    args = [x, norm_w.reshape(1, d), w_gu, w_gu, w_down]
    if final_norm:
        in_specs.append(pl.BlockSpec((1, d), lambda i, j: (0, 0)))
        args.append(out_norm_w.reshape(1, d))
    return pl.pallas_call(
        functools.partial(_ffn_kernel, n_ff=n_ff, final_norm=final_norm),
        grid=(n // tm, n_ff),
        in_specs=in_specs,
        out_specs=pl.BlockSpec((tm, d), lambda i, j: (i, 0)),
        out_shape=jax.ShapeDtypeStruct((n, d), F32),
        scratch_shapes=[pltpu.VMEM((tm, d), BF16)],
        compiler_params=_cparams(("parallel", "arbitrary")),
        name="ffn_final" if final_norm else "ffn",
    )(*args)


_DN_NT = (((1,), (1,)), ((), ()))


def _inproj_kernel(x_ref, nw_ref, w_ref, wbh_ref, wbl_ref, ub_ref, ua_ref, ba_ref, *rest, nb, nq, tail):
    if tail:
        tail_ref, hb_ref = rest
    else:
        (hb_ref,) = rest
    j = pl.program_id(1)

    @pl.when(j == 0)
    def _():
        h = _rms_rows(x_ref[...], nw_ref[...])
        hb = h.astype(BF16)
        hl = (h - hb.astype(F32)).astype(BF16)
        hb_ref[...] = hb
        ba_ref[...] = (lax.dot_general(hb, wbh_ref[...], _DN_NT, preferred_element_type=F32)
                       + lax.dot_general(hb, wbl_ref[...], _DN_NT, preferred_element_type=F32)
                       + lax.dot_general(hl, wbh_ref[...], _DN_NT, preferred_element_type=F32))

    def project(rows):
        return lax.dot_general(hb_ref[rows, :], w_ref[...], _DN_NT, preferred_element_type=F32)

    @pl.when(j < nb)
    def _():
        ub_ref[...] = project(slice(None)).astype(ub_ref.dtype)

    @pl.when(j >= nb)
    def _():
        ua_ref[...] = project(slice(None))

    if tail:
        @pl.when(j < nq)
        def _():
            tm = hb_ref.shape[0]
            tail_ref[...] = project(slice(tm - 2 * SUBLANE, tm))[SUBLANE:, :]


def _inproj(x, norm_w, w_main, w_ba_hi, w_ba_lo, *, tm, b_dtype, tail, tn=768):
    n, d = x.shape
    nb, nq = UB_WIDTH // tn, G_CONV_CH // tn
    assert nb * tn == UB_WIDTH and nq * tn == G_CONV_CH and U_WIDTH % tn == 0 and QKVB_OFF == 0
    out_specs = [
        pl.BlockSpec((tm, tn), lambda i, j: (i, jnp.minimum(j, nb - 1))),
        pl.BlockSpec((tm, tn), lambda i, j: (i, jnp.maximum(j - nb, 0))),
        pl.BlockSpec((tm, LANE), lambda i, j: (i, 0)),
    ]
    out_shape = [
        jax.ShapeDtypeStruct((n, UB_WIDTH), b_dtype),
        jax.ShapeDtypeStruct((n, UA_WIDTH), F32),
        jax.ShapeDtypeStruct((n, LANE), F32),
    ]
    if tail:
        out_specs.append(pl.BlockSpec((SUBLANE, tn), lambda i, j: (i, jnp.minimum(j, nq - 1))))
        out_shape.append(jax.ShapeDtypeStruct((n // tm * SUBLANE, G_CONV_CH), F32))
    return pl.pallas_call(
        functools.partial(_inproj_kernel, nb=nb, nq=nq, tail=tail),
        grid=(n // tm, U_WIDTH // tn),
        in_specs=[
            pl.BlockSpec((tm, d), lambda i, j: (i, 0), pipeline_mode=pl.Buffered(1)),
            pl.BlockSpec((1, d), lambda i, j: (0, 0)),
            pl.BlockSpec((tn, d), lambda i, j: (j, 0)),
            pl.BlockSpec((LANE, d), lambda i, j: (0, 0)),
            pl.BlockSpec((LANE, d), lambda i, j: (0, 0)),
        ],
        out_specs=out_specs,
        out_shape=out_shape,
        scratch_shapes=[pltpu.VMEM((tm, d), BF16)],
        compiler_params=_cparams(("parallel", "arbitrary")),
        name="inproj",
    )(x, norm_w.reshape(1, d), w_main, w_ba_hi, w_ba_lo)


def _rows(start, size, stride):
    return pl.ds(start, size) if stride == 1 else pl.ds(start, size, stride=stride)


def _attn_prompt_kernel(slopes_ref, *refs, n_st, keeps):
    n_g = len(A_GROUPS)
    ins, o_ref = refs[:5 * n_g], refs[5 * n_g]
    kv_refs, scr = refs[5 * n_g + 1:6 * n_g + 1], refs[6 * n_g + 1:]
    st = pl.program_id(1)
    hh = pl.program_id(2)

    @pl.when(st == n_st - 1)
    def _():
        for g in range(n_g):
            keep = keeps[g]
            kc_ref, vc_ref = ins[5 * g + 1], ins[5 * g + 2]
            kv_refs[g][pl.ds(hh, keep, stride=SUBLANE), :] = kc_ref[A_ST - keep:, :]
            kv_refs[g][pl.ds(A_HPG + hh, keep, stride=SUBLANE), :] = vc_ref[A_ST - keep:, :]

    scale = A_E ** -0.5
    qi = lax.broadcasted_iota(jnp.int32, (A_BLOCK, 2 * A_BLOCK), 0)
    kj = lax.broadcasted_iota(jnp.int32, (A_BLOCK, 2 * A_BLOCK), 1)
    delta = A_BLOCK + qi - kj
    band = jnp.logical_and(delta >= 0, delta <= A_BLOCK)
    band_first = jnp.logical_and(band, jnp.logical_or(kj >= A_BLOCK, st > 0))
    dn_nt = (((1,), (1,)), ((), ()))
    for g, (win, dil) in enumerate(A_GROUPS):
        q_ref, kc_ref, vc_ref, kp_ref, vp_ref = ins[5 * g:5 * g + 5]
        og_ref, lg_ref = scr[2 * g:2 * g + 2]
        slope = slopes_ref[g * A_HPG + hh]
        bias = -slope * (dil * delta).astype(F32)
        bias_mid = jnp.where(band, bias, -jnp.inf)
        bias_first = jnp.where(band_first, bias, -jnp.inf)

        def two_blocks(prev_ref, cur_ref, r, j):
            if j == 0:
                return jnp.concatenate([prev_ref[_rows(r, A_BLOCK, dil), :],
                                        cur_ref[_rows(r, A_BLOCK, dil), :]], axis=0)
            return cur_ref[_rows(r + (j - 1) * A_BLOCK * dil, 2 * A_BLOCK, dil), :]

        for r in range(dil):
            for j in range(A_ST // (A_BLOCK * dil)):
                row0 = r + j * A_BLOCK * dil
                q = q_ref[_rows(row0, A_BLOCK, dil), :].astype(BF16)
                k2 = two_blocks(kp_ref, kc_ref, r, j).astype(BF16)
                v2 = two_blocks(vp_ref, vc_ref, r, j).astype(BF16)
                s = lax.dot_general(q, k2, dn_nt, preferred_element_type=F32) * scale
                s = s + (bias_first if j == 0 else bias_mid)
                m = jnp.max(s, axis=-1, keepdims=True)
                p = jnp.exp(s - m)
                den = jnp.sum(p, axis=-1, keepdims=True)
                acc = jnp.dot(p.astype(BF16), v2, preferred_element_type=F32)
                og_ref[_rows(row0, A_BLOCK, dil), :] = acc / den
                lg_ref[_rows(row0, A_BLOCK, dil), :] = jnp.broadcast_to(m + jnp.log(den), (A_BLOCK, A_E))
    chunk = 2 * A_BLOCK
    for c in range(A_ST // chunk):
        rs = slice(c * chunk, (c + 1) * chunk)
        ls = [scr[2 * g + 1][rs, :] for g in range(n_g)]
        mx = functools.reduce(jnp.maximum, ls)
        ws = [jnp.exp(l - mx) for l in ls]
        num = sum(w * scr[2 * g][rs, :] for g, w in enumerate(ws))
        o_ref[rs, :] = num / sum(ws)


def _attn_prompt(slopes, u, *, batch, seq):
    assert seq % A_ST == 0
    u3 = u.reshape(batch, seq, UA_WIDTH)
    in_specs = [pl.BlockSpec(memory_space=pltpu.SMEM)]
    args = [slopes]
    scratch = []
    keeps = tuple(min(win, seq) for win, _ in A_GROUPS)
    assert max(keeps) <= A_ST
    for g, (win, dil) in enumerate(A_GROUPS):
        assert win == A_BLOCK * dil and A_ST % win == 0
        qcb = (QA_OFF + g * A_GW) // A_E
        kcb = (KV_OFF + g * 2 * A_GW) // A_E
        vcb = kcb + A_HPG
        per = A_ST // win

        def cur(cb):
            return pl.BlockSpec((None, A_ST, A_E), lambda b, st, hh, cb=cb: (b, st, cb + hh))

        def prv(cb, win=win, per=per):
            return pl.BlockSpec((None, win, A_E),
                                lambda b, st, hh, cb=cb: (b, jnp.maximum(st * per - 1, 0), cb + hh))

        in_specs += [cur(qcb), cur(kcb), cur(vcb), prv(kcb), prv(vcb)]
        args += [u3] * 5
        scratch += [pltpu.VMEM((A_ST, A_E), F32), pltpu.VMEM((A_ST, A_E), F32)]
    n_st = seq // A_ST
    out_specs = [pl.BlockSpec((None, A_ST, A_E), lambda b, st, hh: (b, st, hh))]
    out_shape = [jax.ShapeDtypeStruct((batch, seq, A_GW), F32)]
    for keep in keeps:
        out_specs.append(pl.BlockSpec((None, keep * SUBLANE, A_E), lambda b, st, hh: (b, 0, 0),
                                      pipeline_mode=pl.Buffered(1)))
        out_shape.append(jax.ShapeDtypeStruct((batch, keep * SUBLANE, A_E), F32))
    outs = pl.pallas_call(
        functools.partial(_attn_prompt_kernel, n_st=n_st, keeps=keeps),
        grid=(batch, n_st, A_HPG),
        in_specs=in_specs,
        out_specs=out_specs,
        out_shape=out_shape,
        scratch_shapes=scratch,
        compiler_params=_cparams(("parallel", "arbitrary", "arbitrary")),
        name="attn_prompt",
    )(*args)
    rows = [kv.reshape(1, batch, keep, 2, A_HPG, A_E) for kv, keep in zip(outs[1:], keeps)]
    return outs[0], rows


def _pick_rows(rows):
    sub = lax.broadcasted_iota(jnp.int32, (SUBLANE, A_E), 0)
    out = jnp.zeros((SUBLANE, A_E), F32)
    for i, r in enumerate(rows):
        out = jnp.where(sub == i, jnp.broadcast_to(r, (SUBLANE, A_E)), out)
    return out


def _attn_sample_kernel(slopes_ref, q_ref, kv_ref, c0_ref, c1_ref, c2_ref, o_ref, *, dec_seq, nseq):
    scale = A_E ** -0.5
    caches = (c0_ref, c1_ref, c2_ref)
    n_keys = A_BLOCK * SUBLANE
    sub = lax.broadcasted_iota(jnp.int32, (SUBLANE, n_keys), 0)
    lane = lax.broadcasted_iota(jnp.int32, (SUBLANE, n_keys), 1)
    tile_row = lane & (SUBLANE - 1)
    m_idx = lane >> 3
    own_k = tile_row == sub
    sub1 = lax.broadcasted_iota(jnp.int32, (SUBLANE, 1), 0)
    dn_nt = (((1,), (1,)), ((), ()))
    slope8s = []
    for g in range(len(A_GROUPS)):
        slope8 = jnp.zeros((SUBLANE, 1), F32)
        for h in range(A_HPG):
            slope8 = jnp.where(sub1 == h, slopes_ref[g * A_HPG + h], slope8)
        slope8s.append(slope8)

    def keys(bb, s, g):
        res = 0 if A_GROUPS[g][1] == 1 else s
        return caches[g][bb, :, res, :, :].reshape(n_keys, A_E).astype(BF16)

    units = [(bb, s, g) for bb in range(nseq) for s in range(dec_seq) for g in range(len(A_GROUPS))]
    q8s = [_pick_rows([q_ref[bb * dec_seq + s:bb * dec_seq + s + 1, g * A_GW + h * A_E:g * A_GW + (h + 1) * A_E]
                       for h in range(A_HPG)]) for bb, s, g in units]
    scs = [lax.dot_general(q8.astype(BF16), keys(*u), dn_nt, preferred_element_type=F32) * scale
           for q8, u in zip(q8s, units)]
    ms, dens, pvs, news = [], [], [], []
    for (bb, s, g), q8, sc in zip(units, q8s, scs):
        win, dil = A_GROUPS[g]
        row = bb * dec_seq + s
        slope8 = slope8s[g]
        if dil == 1:
            dist = (win + s - m_idx).astype(F32)
            valid = jnp.logical_and(own_k, m_idx >= s)
            new_rows = [(bb * dec_seq + t, float(s - t)) for t in range(s + 1)]
        else:
            dist = (dil * (win // dil - m_idx)).astype(F32)
            valid = own_k
            new_rows = [(row, 0.0)]
        sc = jnp.where(valid, sc - slope8 * dist, -jnp.inf)
        m = jnp.max(sc, axis=-1, keepdims=True)
        koff = g * 2 * A_GW
        new = []
        for nrow, ndist in new_rows:
            k8 = _pick_rows([kv_ref[nrow:nrow + 1, koff + h * A_E:koff + (h + 1) * A_E]
                             for h in range(A_HPG)])
            v8 = _pick_rows([kv_ref[nrow:nrow + 1, koff + A_GW + h * A_E:koff + A_GW + (h + 1) * A_E]
                             for h in range(A_HPG)])
            s_n = jnp.sum(k8 * q8, axis=-1, keepdims=True) * scale - slope8 * ndist
            m = jnp.maximum(m, s_n)
            new.append((s_n, v8))
        p = jnp.exp(sc - m)
        ms.append(m)
        dens.append(jnp.sum(p, axis=-1, keepdims=True))
        pvs.append(pltpu.roll(p, A_HPG, 1).astype(BF16))
        news.append(new)
    accs = [jnp.dot(pv, keys(*u), preferred_element_type=F32) for pv, u in zip(pvs, units)]

    out_tiles = [jnp.zeros((nseq * dec_seq, A_E), F32) for _ in range(A_HPG)]
    out_sub = lax.broadcasted_iota(jnp.int32, (nseq * dec_seq, A_E), 0)
    n_g = len(A_GROUPS)
    for ui in range(0, len(units), n_g):
        bb, s, _ = units[ui]
        outs, lses = [], []
        for m, den, acc, new in zip(ms[ui:ui + n_g], dens[ui:ui + n_g], accs[ui:ui + n_g], news[ui:ui + n_g]):
            for s_n, v8 in new:
                p_n = jnp.exp(s_n - m)
                den = den + p_n
                acc = acc + p_n * v8
            outs.append(acc / den)
            lses.append(m + jnp.log(den))
        mx = functools.reduce(jnp.maximum, lses)
        ws = [jnp.exp(l - mx) for l in lses]
        o8 = sum(w * o for w, o in zip(ws, outs)) / sum(ws)
        for h in range(A_HPG):
            out_tiles[h] = jnp.where(out_sub == bb * dec_seq + s,
                                     jnp.broadcast_to(o8[h:h + 1, :], (nseq * dec_seq, A_E)), out_tiles[h])
    for h in range(A_HPG):
        o_ref[:, h * A_E:(h + 1) * A_E] = out_tiles[h]


def _attn_sample(slopes, u, caches, *, batch, dec_seq):
    nseq = SUBLANE // dec_seq
    assert nseq * dec_seq == SUBLANE and batch % nseq == 0
    views, specs = [], []
    for (win, dil), cache in zip(A_GROUPS, caches):
        wb = cache.shape[1]
        assert wb == win and wb // dil == A_BLOCK and (dil == 1 or dec_seq <= dil)
        views.append(cache.reshape(batch, wb // dil, dil, SUBLANE, A_E))
        res = 1 if dil == 1 else dec_seq
        specs.append(pl.BlockSpec((nseq, A_BLOCK, res, SUBLANE, A_E), lambda i: (i, 0, 0, 0, 0)))
    n_q = len(A_GROUPS) * A_GW
    rows = nseq * dec_seq
    return pl.pallas_call(
        functools.partial(_attn_sample_kernel, dec_seq=dec_seq, nseq=nseq),
        grid=(batch // nseq,),
        in_specs=[
            pl.BlockSpec(memory_space=pltpu.SMEM),
            pl.BlockSpec((rows, n_q), lambda i: (i, QA_OFF // n_q)),
            pl.BlockSpec((rows, 2 * n_q), lambda i: (i, KV_OFF // (2 * n_q))),
        ] + specs,
        out_specs=pl.BlockSpec((rows, A_GW), lambda i: (i, 0)),
        out_shape=jax.ShapeDtypeStruct((batch * dec_seq, A_GW), F32),
        compiler_params=_cparams(("parallel",)),
        name="attn_sample",
    )(slopes, u, u, *views)


def _bdot(a, b, dn=(((1,), (0,)), ((), ()))):
    return lax.dot_general(a.astype(BF16), b.astype(BF16), dn, preferred_element_type=F32)


def _gdn_kernel(qkv_ref, z_ref, ba_ref, cw_ref, alog_ref, dtb_ref, nw_ref, conv0_ref, s0_ref,
                o_ref, sout_ref, cb_ref, s_ref, *, tb, cc, nc, bpb, gs):
    n = pl.program_id(1)

    @pl.when(n == 0)
    def _():
        s_ref[...] = s0_ref[...]
        cb_ref[:, 0:SUBLANE, :] = conv0_ref[...]

    rows = lax.broadcasted_iota(jnp.int32, (cc, 1), 0)
    live = rows < tb
    ii = lax.broadcasted_iota(jnp.int32, (cc, cc), 0)
    jj = lax.broadcasted_iota(jnp.int32, (cc, cc), 1)
    incl = ii >= jj
    strict = ii > jj
    tri = incl.astype(F32)
    cw = cw_ref[...]
    first = SUBLANE - (G_CONV - 1)
    dn_nt = (((1,), (1,)), ((), ()))
    dn_tn = (((0,), (0,)), ((), ()))
    eye = ii == jj

    for b0 in range(0, bpb, gs):
        qs, ks, kbs, decays, rhss, e_gcs, e_rests, e_ends = [], [], [], [], [], [], [], []
        idx = range(gs * G_HEADS)
        for bi in range(b0, b0 + gs):
            x = qkv_ref[bi].astype(F32)
            cb_ref[bi, SUBLANE:SUBLANE + tb, :] = x
            if tb < cc:
                cb_ref[bi, SUBLANE + tb:SUBLANE + cc, :] = jnp.zeros((cc - tb, G_CONV_CH), F32)
            y = cb_ref[bi, SUBLANE:SUBLANE + cc, :] * cw[G_CONV - 1:G_CONV, :]
            for j in range(G_CONV - 2, -1, -1):
                y = y + cb_ref[bi, first + j:first + j + cc, :] * cw[j:j + 1, :]
            if nc > 1:
                cb_ref[bi, 0:SUBLANE, :] = x[tb - SUBLANE:tb, :]
            y = y * jax.nn.sigmoid(y)

            ba = ba_ref[bi]
            if tb < cc:
                ba = jnp.concatenate([ba, jnp.zeros((cc - tb, LANE), F32)], axis=0)
            beta_t = jnp.where(live, jax.nn.sigmoid(ba), 0.0)
            xs = ba + dtb_ref[...]
            softplus = jnp.maximum(xs, 0.0) + jnp.log(1.0 + jnp.exp(-jnp.abs(xs)))
            g_t = jnp.where(live, -jnp.exp(alog_ref[...]) * softplus, 0.0)
            gc = jnp.dot(tri, g_t, precision=HIGHEST, preferred_element_type=F32)
            gc_t = jnp.concatenate([gc, jnp.zeros((LANE - cc, LANE), F32)], axis=0).T
            gc_last = gc[cc - 1:cc, :]
            e_gc = jnp.exp(gc)
            e_rest = jnp.exp(gc_last - gc)
            e_end = jnp.exp(gc_last)
            for h in range(G_HEADS):
                sl = slice(h * G_DK, (h + 1) * G_DK)
                gl = slice(G_HEADS + h, G_HEADS + h + 1)
                q = y[:, sl]
                k = y[:, G_QK + h * G_DK:G_QK + (h + 1) * G_DK]
                v = y[:, 2 * G_QK + h * G_DV:2 * G_QK + (h + 1) * G_DV]
                q = q * lax.rsqrt(jnp.sum(q * q, axis=-1, keepdims=True) + NORM_EPS) * (G_DK ** -0.5)
                k = k * lax.rsqrt(jnp.sum(k * k, axis=-1, keepdims=True) + NORM_EPS)
                if tb < cc:
                    q = jnp.where(live, q, 0.0)
                    k = jnp.where(live, k, 0.0)
                    v = jnp.where(live, v, 0.0)
                beta = beta_t[:, h:h + 1]
                kb = k * beta
                qs.append(q)
                ks.append(k)
                kbs.append(kb)
                decays.append(jnp.exp(jnp.where(
                    incl, gc[:, gl] - gc_t[G_HEADS + h:G_HEADS + h + 1, 0:cc], -jnp.inf)))
                rhss.append(jnp.concatenate([v * beta, kb * e_gc[:, gl]], axis=1))
                e_gcs.append(e_gc[:, gl])
                e_rests.append(e_rest[:, gl])
                e_ends.append(e_end[:, gl])

        kq = [_bdot(jnp.concatenate([kbs[i], qs[i]], axis=0), ks[i], dn_nt) for i in idx]
        a_low = [jnp.where(strict, kq[i][:cc] * decays[i], 0.0) for i in idx]
        qk = [kq[i][cc:] * decays[i] for i in idx]
        inv = [jnp.where(eye, 1.0, -a_low[i]) for i in idx]
        pw = a_low
        span = 2
        while span < cc:
            pw = [_bdot(pw[i], pw[i]) for i in idx]
            inv = [inv[i] + _bdot(pw[i], inv[i]) for i in idx]
            span *= 2
        rhss = [_bdot(inv[i], rhss[i]) for i in idx]
        s_old = [s_ref[b0 + i // G_HEADS, i % G_HEADS] for i in idx]
        ws = [_bdot(jnp.concatenate([rhss[i][:, G_DV:], qs[i] * e_gcs[i]], axis=0), s_old[i]) for i in idx]
        v_new = [rhss[i][:, :G_DV] - ws[i][:cc] for i in idx]
        for i in idx:
            s_ref[b0 + i // G_HEADS, i % G_HEADS] = (
                s_old[i] * e_ends[i] + _bdot(ks[i] * e_rests[i], v_new[i], dn_tn))
        o = [ws[i][cc:] + _bdot(qk[i], v_new[i]) for i in idx]
        for i in idx:
            bi, h = b0 + i // G_HEADS, i % G_HEADS
            sl = slice(h * G_DV, (h + 1) * G_DV)
            o_h = o[i] * lax.rsqrt(jnp.mean(o[i] * o[i], axis=-1, keepdims=True) + NORM_EPS) * nw_ref[...]
            zz = z_ref[bi, :, sl].astype(F32)
            o_ref[bi, :, sl] = o_h[:tb] * (zz * jax.nn.sigmoid(zz))

    @pl.when(n == nc - 1)
    def _():
        sout_ref[...] = s_ref[...]


def _gdn(u, ba, conv0, s0, conv_w, alog_t, dtb_t, norm_w, *, batch, seq, bpb, gs=None):
    gs = bpb if gs is None else min(gs, bpb)
    assert bpb % gs == 0
    cc = G_CHUNK if seq >= G_CHUNK else SUBLANE
    tb = min(seq, cc)
    nc = seq // tb
    assert nc * tb == seq and (nc == 1 or tb == cc) and batch % bpb == 0
    uv = u.reshape(batch, seq, UB_WIDTH)
    bav = ba.reshape(batch, seq, LANE)
    conv0p = jnp.concatenate(
        [jnp.zeros((batch, SUBLANE - (G_CONV - 1), G_CONV_CH), F32), conv0.astype(F32)], axis=1)
    const2 = lambda b, n: (0, 0)
    return pl.pallas_call(
        functools.partial(_gdn_kernel, tb=tb, cc=cc, nc=nc, bpb=bpb, gs=gs),
        grid=(batch // bpb, nc),
        in_specs=[
            pl.BlockSpec((bpb, tb, G_CONV_CH), lambda b, n: (b, n, QKVB_OFF // G_CONV_CH)),
            pl.BlockSpec((bpb, tb, G_QK), lambda b, n: (b, n, Z_OFF // G_QK)),
            pl.BlockSpec((bpb, tb, LANE), lambda b, n: (b, n, 0)),
            pl.BlockSpec((G_CONV, G_CONV_CH), const2),
            pl.BlockSpec((1, LANE), const2),
            pl.BlockSpec((1, LANE), const2),
            pl.BlockSpec((1, G_DV), const2),
            pl.BlockSpec((bpb, SUBLANE, G_CONV_CH), lambda b, n: (b, 0, 0)),
            pl.BlockSpec((bpb, G_HEADS, G_DK, G_DV), lambda b, n: (b, 0, 0, 0)),
        ],
        out_specs=[
            pl.BlockSpec((bpb, tb, G_QK), lambda b, n: (b, n, 0)),
            pl.BlockSpec((bpb, G_HEADS, G_DK, G_DV), lambda b, n: (b, 0, 0, 0)),
        ],
        out_shape=[
            jax.ShapeDtypeStruct((batch, seq, G_QK), F32),
            jax.ShapeDtypeStruct((batch, G_HEADS, G_DK, G_DV), F32),
        ],
        scratch_shapes=[
            pltpu.VMEM((bpb, SUBLANE + cc, G_CONV_CH), F32),
            pltpu.VMEM((bpb, G_HEADS, G_DK, G_DV), F32),
        ],
        compiler_params=_cparams(("parallel", "arbitrary")),
        name="gdn",
    )(uv, uv, bav, conv_w, alog_t, dtb_t, norm_w.reshape(1, G_DV), conv0p, s0)


def _mix_kernel(x_ref, oa_ref, ob_ref, ga_ref, gb_ref, wa_ref, wb_ref, wo_ref, o_ref):
    pa = jnp.dot(oa_ref[...].astype(BF16), wa_ref[...], preferred_element_type=F32)
    pb = jnp.dot(ob_ref[...].astype(BF16), wb_ref[...], preferred_element_type=F32)
    merged = (jax.nn.sigmoid(ga_ref[...].astype(F32)) * pa
              + jax.nn.sigmoid(gb_ref[...].astype(F32)) * pb)
    o_ref[...] = x_ref[...] + jnp.dot(merged.astype(BF16), wo_ref[...], preferred_element_type=F32)


def _mix(x, oa, ob, u, wa, wb, wo, *, tm):
    n, d = x.shape
    const = lambda i: (0, 0)
    return pl.pallas_call(
        _mix_kernel,
        grid=(n // tm,),
        in_specs=[
            pl.BlockSpec((tm, d), lambda i: (i, 0)),
            pl.BlockSpec((tm, A_GW), lambda i: (i, 0)),
            pl.BlockSpec((tm, G_QK), lambda i: (i, 0)),
            pl.BlockSpec((tm, d), lambda i: (i, GATE_OFF // d)),
            pl.BlockSpec((tm, d), lambda i: (i, GATE_OFF // d + 1)),
            pl.BlockSpec((A_GW, d), const),
            pl.BlockSpec((G_QK, d), const),
            pl.BlockSpec((d, d), const),
        ],
        out_specs=pl.BlockSpec((tm, d), lambda i: (i, 0)),
        out_shape=jax.ShapeDtypeStruct((n, d), F32),
        compiler_params=_cparams(("parallel",)),
        name="mix",
    )(x, oa, ob, u, u, wa, wb, wo)


def _w_in_sources():
    aw = len(A_GROUPS) * A_GW
    src_q, src_k, src_v = 0, aw, 2 * aw
    src_qkvb = 3 * aw
    src_z = src_qkvb + G_CONV_CH
    src_ba = src_z + G_QK
    src_gate = src_ba + 2 * G_HEADS
    blocks = [src_qkvb + W_CB * t for t in range(G_CONV_CH // W_CB)]
    blocks += [src_z + W_CB * t for t in range(G_QK // W_CB)]
    blocks += [src_gate + W_CB * t for t in range(2 * D_MODEL // W_CB)]
    for g in range(len(A_GROUPS)):
        blocks += [src_k + g * A_GW, src_v + g * A_GW]
    blocks += [src_q + W_CB * t for t in range(aw // W_CB)]
    assert len(blocks) * W_CB == U_WIDTH and A_GW == W_CB
    assert all(b % SUBLANE == 0 for b in blocks)
    return blocks, src_ba


def _permute_cast_kernel(src_ref, w_ref, o_ref):
    del src_ref
    o_ref[...] = w_ref[...].astype(BF16)


def _permute_w_in(wt):
    d = wt.shape[1]
    blocks, _ = _w_in_sources()
    return pl.pallas_call(
        _permute_cast_kernel,
        grid_spec=pltpu.PrefetchScalarGridSpec(
            num_scalar_prefetch=1,
            grid=(len(blocks),),
            in_specs=[pl.BlockSpec((pl.Element(W_CB), pl.Element(d)),
                                   lambda j, src: (pl.multiple_of(src[j], SUBLANE), 0))],
            out_specs=pl.BlockSpec((W_CB, d), lambda j, src: (j, 0)),
        ),
        out_shape=jax.ShapeDtypeStruct((U_WIDTH, d), BF16),
        compiler_params=_cparams(("parallel",)),
        name="permute_w_in",
    )(jnp.asarray(blocks, jnp.int32), wt)


def _prep_weights(w_in, gdn_a_log, gdn_dt_bias):
    wt = w_in.T
    w_main = _permute_w_in(wt)
    src_ba = _w_in_sources()[1]
    ba_pad = jnp.pad(wt[src_ba:src_ba + 2 * G_HEADS, :], ((0, LANE - 2 * G_HEADS), (0, 0)))
    ba_hi = ba_pad.astype(BF16)
    ba_lo = (ba_pad - ba_hi.astype(F32)).astype(BF16)
    pad = (G_HEADS, LANE - 2 * G_HEADS)
    alog_t = jnp.pad(gdn_a_log.astype(F32), pad).reshape(1, LANE)
    dtb_t = jnp.pad(gdn_dt_bias.astype(F32), pad).reshape(1, LANE)
    return w_main, ba_hi, ba_lo, alog_t, dtb_t


def _kv_rows(u3, group, keep):
    b, t, _ = u3.shape
    off = KV_OFF + group * 2 * A_GW
    return u3[:, t - keep:, off:off + 2 * A_GW].reshape(1, b, keep, 2, A_HPG, A_E)


def _layer(x, batch, seq, conv0, s0, attn_fn, p, *, tm, gdn_bpb, b_dtype):
    tm = min(tm, x.shape[0])
    gdn_bpb = min(gdn_bpb, batch)
    tail = b_dtype != F32
    x1 = _ffn(x, p["norm_ffn1"], p["w_ffn1_gu"], p["w_ffn1_down"], tm=tm)
    ub, ua, ba, *rest = _inproj(x1, p["norm_mix"], p["w_main"], p["ba_hi"], p["ba_lo"],
                                tm=tm, b_dtype=b_dtype, tail=tail)
    oa, rows = attn_fn(ua)
    ob, s_new = _gdn(ub, ba, conv0, s0, p["conv_w"], p["alog_t"], p["dtb_t"], p["gdn_norm"],
                     batch=batch, seq=seq, bpb=gdn_bpb)
    x2 = _mix(x1, oa.reshape(batch * seq, A_GW), ob.reshape(batch * seq, G_QK), ub,
              p["w_proj_a"], p["w_proj_b"], p["w_out"], tm=min(tm, 1024))
    y = _ffn(x2, p["norm_ffn2"], p["w_ffn2_gu"], p["w_ffn2_down"], p["norm_out"], tm=tm)
    if rows is None:
        u3 = ua.reshape(batch, seq, UA_WIDTH)
        rows = [_kv_rows(u3, g, min(win, seq)) for g, (win, _) in enumerate(A_GROUPS)]
    if tail:
        assert seq % tm == 0 and seq >= SUBLANE
        last = rest[0].reshape(batch, seq // tm, SUBLANE, G_CONV_CH)[:, -1]
        conv_new = last[:, SUBLANE - (G_CONV - 1):, :][None]
    else:
        ub3 = ub.reshape(batch, seq, UB_WIDTH)
        conv_new = ub3[:, seq - (G_CONV - 1):, QKVB_OFF:QKVB_OFF + G_CONV_CH][None]
    return y.reshape(batch, seq, D_MODEL), rows, conv_new, s_new[None]


def kernel(x_prompt, x_sample, cache_kv_w128, cache_kv_w512, cache_kv_w2048, state_conv, state_ssm,
           norm_ffn1, w_ffn1_gu, w_ffn1_down, norm_mix, w_in, conv_w, gdn_a_log, gdn_dt_bias, gdn_norm,
           w_proj_a, w_proj_b, w_out, norm_ffn2, w_ffn2_gu, w_ffn2_down, norm_out):
    assert w_in.shape[0] == 1, "single layer"
    n_heads = len(A_GROUPS) * A_HPG
    slopes = jnp.exp2(-8.0 * jnp.arange(1, n_heads + 1, dtype=F32) / n_heads)
    w_main, ba_hi, ba_lo, alog_t, dtb_t = _prep_weights(w_in[0], gdn_a_log[0], gdn_dt_bias[0])
    p = dict(
        norm_ffn1=norm_ffn1[0], w_ffn1_gu=w_ffn1_gu[0], w_ffn1_down=w_ffn1_down[0],
        norm_mix=norm_mix[0], w_main=w_main, ba_hi=ba_hi, ba_lo=ba_lo, conv_w=conv_w[0],
        alog_t=alog_t, dtb_t=dtb_t, gdn_norm=gdn_norm[0],
        w_proj_a=w_proj_a[0].astype(BF16), w_proj_b=w_proj_b[0].astype(BF16), w_out=w_out[0].astype(BF16),
        norm_ffn2=norm_ffn2[0], w_ffn2_gu=w_ffn2_gu[0], w_ffn2_down=w_ffn2_down[0],
        norm_out=norm_out,
    )
    bp, tp, d = x_prompt.shape
    bs, ts, _ = x_sample.shape

    conv0_p = jnp.zeros((bp, G_CONV - 1, G_CONV_CH), F32)
    ssm0_p = jnp.zeros((bp, G_HEADS, G_DK, G_DV), F32)
    yp, rows_p, conv_p, ssm_p = _layer(
        x_prompt.reshape(bp * tp, d), bp, tp, conv0_p, ssm0_p,
        functools.partial(_attn_prompt, slopes, batch=bp, seq=tp), p, tm=2048, gdn_bpb=4, b_dtype=BF16)

    caches = (cache_kv_w128[0], cache_kv_w512[0], cache_kv_w2048[0])
    ys, rows_s, conv_s, ssm_s = _layer(
        x_sample.reshape(bs * ts, d), bs, ts, state_conv[0], state_ssm[0],
        lambda u: (_attn_sample(slopes, u, caches, batch=bs, dec_seq=ts), None), p, tm=512, gdn_bpb=8, b_dtype=F32)

    return (yp, ys, rows_p[0], rows_p[1], rows_p[2], conv_p, ssm_p,
            rows_s[0], rows_s[1], rows_s[2], conv_s, ssm_s)
```

```python
import functools

import jax
import jax.numpy as jnp
from jax import lax
from jax.experimental import pallas as pl
from jax.experimental.pallas import tpu as pltpu

F32 = jnp.float32
BF16 = jnp.bfloat16
HIGHEST = lax.Precision.HIGHEST

D_MODEL = 1024
D_FF = 2816
NORM_EPS = 1e-6
A_GROUPS = ((128, 1), (512, 4), (2048, 16))
A_HPG = 4
A_E = 128
A_BLOCK = 128
A_GW = A_HPG * A_E
A_ST = 2048
G_HEADS = 8
G_DK = 128
G_DV = 128
G_QK = G_HEADS * G_DK
G_CONV_CH = 3 * G_QK
G_CONV = 4
G_CHUNK = 64

QKVB_OFF = 0
Z_OFF = 3072
GATE_OFF = 4096
UB_WIDTH = 6144
KV_OFF = 0
QA_OFF = 3072
UA_WIDTH = 4608
U_WIDTH = UB_WIDTH + UA_WIDTH
W_CB = 512

VMEM_LIMIT = 56 * 1024 * 1024
LANE = 128
SUBLANE = 8
MXU_COLS = 256


def _cparams(sem):
    return pltpu.CompilerParams(dimension_semantics=sem, vmem_limit_bytes=VMEM_LIMIT)


def _rms_rows(x, w):
    return x * lax.rsqrt(jnp.mean(x * x, axis=-1, keepdims=True) + NORM_EPS) * w


def _ffn_kernel(x_ref, nw_ref, wg_ref, wu_ref, wd_ref, *rest, n_ff, final_norm):
    if final_norm:
        onw_ref, o_ref, hb_ref = rest
    else:
        o_ref, hb_ref = rest
    j = pl.program_id(1)

    @pl.when(j == 0)
    def _():
        hb_ref[...] = _rms_rows(x_ref[...], nw_ref[...]).astype(BF16)
        o_ref[...] = jnp.zeros_like(o_ref)

    hb = hb_ref[...]
    g = jnp.dot(hb, wg_ref[...].astype(BF16), preferred_element_type=F32)
    u = jnp.dot(hb, wu_ref[...].astype(BF16), preferred_element_type=F32)
    a = (g * jax.nn.sigmoid(g) * u).astype(BF16)
    o_ref[...] += jnp.dot(a, wd_ref[...].astype(BF16), preferred_element_type=F32)

    @pl.when(j == n_ff - 1)
    def _():
        y = x_ref[...] + 0.5 * o_ref[...]
        if final_norm:
            y = _rms_rows(y, onw_ref[...])
        o_ref[...] = y


def _ffn(x, norm_w, w_gu, w_down, out_norm_w=None, *, tm, tf=256):
    n, d = x.shape
    n_ff = D_FF // tf
    final_norm = out_norm_w is not None
    in_specs = [
        pl.BlockSpec((tm, d), lambda i, j: (i, 0)),
        pl.BlockSpec((1, d), lambda i, j: (0, 0)),
        pl.BlockSpec((d, tf), lambda i, j: (0, j)),
        pl.BlockSpec((d, tf), lambda i, j: (0, j + n_ff)),
        pl.BlockSpec((tf, d), lambda i, j: (j, 0)),
    ]
    args = [x, norm_w.reshape(1, d), w_gu, w_gu, w_down]
    if final_norm:
        in_specs.append(pl.BlockSpec((1, d), lambda i, j: (0, 0)))
        args.append(out_norm_w.reshape(1, d))
    return pl.pallas_call(
        functools.partial(_ffn_kernel, n_ff=n_ff, final_norm=final_norm),
        grid=(n // tm, n_ff),
        in_specs=in_specs,
        out_specs=pl.BlockSpec((tm, d), lambda i, j: (i, 0)),
        out_shape=jax.ShapeDtypeStruct((n, d), F32),
        scratch_shapes=[pltpu.VMEM((tm, d), BF16)],
        compiler_params=_cparams(("parallel", "arbitrary")),
        name="ffn_final" if final_norm else "ffn",
    )(*args)


_DN_NT = (((1,), (1,)), ((), ()))


def _inproj_kernel(x_ref, nw_ref, w_ref, wba_ref, ub_ref, ua_ref, ba_ref, *rest, nb, nq, tail):
    if tail:
        tail_ref, hb_ref = rest
    else:
        (hb_ref,) = rest
    j = pl.program_id(1)

    @pl.when(j == 0)
    def _():
        h = _rms_rows(x_ref[...], nw_ref[...])
        hb = h.astype(BF16)
        hl = (h - hb.astype(F32)).astype(BF16)
        hb_ref[...] = hb
        n_ba = 2 * G_HEADS
        r_hi = lax.dot_general(hb, wba_ref[...], _DN_NT, preferred_element_type=F32)[:, :LANE]
        r_lo = lax.dot_general(hl, wba_ref[...], _DN_NT, preferred_element_type=F32)[:, :LANE]
        lane = lax.broadcasted_iota(jnp.int32, r_hi.shape, 1)
        ba_ref[...] = jnp.where(lane < n_ba, r_hi + pltpu.roll(r_hi, LANE - n_ba, 1) + r_lo, 0.0)

    def project(rows):
        return lax.dot_general(hb_ref[rows, :], w_ref[...], _DN_NT, preferred_element_type=F32)

    @pl.when(j < nb)
    def _():
        ub_ref[...] = project(slice(None)).astype(ub_ref.dtype)

    @pl.when(j >= nb)
    def _():
        ua_ref[...] = project(slice(None))

    if tail:
        @pl.when(j < nq)
        def _():
            tm = hb_ref.shape[0]
            tail_ref[...] = project(slice(tm - 2 * SUBLANE, tm))[SUBLANE:, :]


def _inproj(x, norm_w, w_main, w_ba, *, tm, b_dtype, tail, tn=768):
    n, d = x.shape
    nb, nq = UB_WIDTH // tn, G_CONV_CH // tn
    assert nb * tn == UB_WIDTH and nq * tn == G_CONV_CH and U_WIDTH % tn == 0 and QKVB_OFF == 0
    out_specs = [
        pl.BlockSpec((tm, tn), lambda i, j: (i, jnp.minimum(j, nb - 1))),
        pl.BlockSpec((tm, tn), lambda i, j: (i, jnp.maximum(j - nb, 0))),
        pl.BlockSpec((tm, LANE), lambda i, j: (i, 0)),
    ]
    out_shape = [
        jax.ShapeDtypeStruct((n, UB_WIDTH), b_dtype),
        jax.ShapeDtypeStruct((n, UA_WIDTH), F32),
        jax.ShapeDtypeStruct((n, LANE), F32),
    ]
    if tail:
        out_specs.append(pl.BlockSpec((SUBLANE, tn), lambda i, j: (i, jnp.minimum(j, nq - 1))))
        out_shape.append(jax.ShapeDtypeStruct((n // tm * SUBLANE, G_CONV_CH), F32))
    return pl.pallas_call(
        functools.partial(_inproj_kernel, nb=nb, nq=nq, tail=tail),
        grid=(n // tm, U_WIDTH // tn),
        in_specs=[
            pl.BlockSpec((tm, d), lambda i, j: (i, 0), pipeline_mode=pl.Buffered(1)),
            pl.BlockSpec((1, d), lambda i, j: (0, 0)),
            pl.BlockSpec((tn, d), lambda i, j: (j, 0)),
            pl.BlockSpec((MXU_COLS, d), lambda i, j: (0, 0)),
        ],
        out_specs=out_specs,
        out_shape=out_shape,
        scratch_shapes=[pltpu.VMEM((tm, d), BF16)],
        compiler_params=_cparams(("parallel", "arbitrary")),
        name="inproj",
    )(x, norm_w.reshape(1, d), w_main, w_ba)


def _rows(start, size, stride):
    return pl.ds(start, size) if stride == 1 else pl.ds(start, size, stride=stride)


def _attn_prompt_kernel(slopes_ref, *refs, n_st, keeps):
    n_g = len(A_GROUPS)
    ins, o_ref = refs[:5 * n_g], refs[5 * n_g]
    kv_refs, scr = refs[5 * n_g + 1:6 * n_g + 1], refs[6 * n_g + 1:]
    st = pl.program_id(1)
    hh = pl.program_id(2)

    @pl.when(st == n_st - 1)
    def _():
        for g in range(n_g):
            keep = keeps[g]
            kc_ref, vc_ref = ins[5 * g + 1], ins[5 * g + 2]
            kv_refs[g][pl.ds(hh, keep, stride=SUBLANE), :] = kc_ref[A_ST - keep:, :]
            kv_refs[g][pl.ds(A_HPG + hh, keep, stride=SUBLANE), :] = vc_ref[A_ST - keep:, :]

    scale = A_E ** -0.5
    qi = lax.broadcasted_iota(jnp.int32, (A_BLOCK, 2 * A_BLOCK), 0)
    kj = lax.broadcasted_iota(jnp.int32, (A_BLOCK, 2 * A_BLOCK), 1)
    delta = A_BLOCK + qi - kj
    band = jnp.logical_and(delta >= 0, delta <= A_BLOCK)
    band_first = jnp.logical_and(band, jnp.logical_or(kj >= A_BLOCK, st > 0))
    dn_nt = (((1,), (1,)), ((), ()))
    for g, (win, dil) in enumerate(A_GROUPS):
        q_ref, kc_ref, vc_ref, kp_ref, vp_ref = ins[5 * g:5 * g + 5]
        og_ref, lg_ref = scr[2 * g:2 * g + 2]
        slope = slopes_ref[g * A_HPG + hh]
        bias = -slope * (dil * delta).astype(F32)
        bias_mid = jnp.where(band, bias, -jnp.inf)
        bias_first = jnp.where(band_first, bias, -jnp.inf)

        def two_blocks(prev_ref, cur_ref, r, j):
            if j == 0:
                return jnp.concatenate([prev_ref[_rows(r, A_BLOCK, dil), :],
                                        cur_ref[_rows(r, A_BLOCK, dil), :]], axis=0)
            return cur_ref[_rows(r + (j - 1) * A_BLOCK * dil, 2 * A_BLOCK, dil), :]

        for r in range(dil):
            for j in range(A_ST // (A_BLOCK * dil)):
                row0 = r + j * A_BLOCK * dil
                q = q_ref[_rows(row0, A_BLOCK, dil), :].astype(BF16)
                k2 = two_blocks(kp_ref, kc_ref, r, j).astype(BF16)
                v2 = two_blocks(vp_ref, vc_ref, r, j).astype(BF16)
                s = lax.dot_general(q, k2, dn_nt, preferred_element_type=F32) * scale
                s = s + (bias_first if j == 0 else bias_mid)
                m = jnp.max(s, axis=-1, keepdims=True)
                p = jnp.exp(s - m)
                den = jnp.sum(p, axis=-1, keepdims=True)
                acc = jnp.dot(p.astype(BF16), v2, preferred_element_type=F32)
                og_ref[_rows(row0, A_BLOCK, dil), :] = acc / den
                lg_ref[_rows(row0, A_BLOCK, dil), :] = jnp.broadcast_to(m + jnp.log(den), (A_BLOCK, A_E))
    chunk = 2 * A_BLOCK
    for c in range(A_ST // chunk):
        rs = slice(c * chunk, (c + 1) * chunk)
        ls = [scr[2 * g + 1][rs, :] for g in range(n_g)]
        mx = functools.reduce(jnp.maximum, ls)
        ws = [jnp.exp(l - mx) for l in ls]
        num = sum(w * scr[2 * g][rs, :] for g, w in enumerate(ws))
        o_ref[rs, :] = num / sum(ws)


def _attn_prompt(slopes, u, *, batch, seq):
    assert seq % A_ST == 0
    u3 = u.reshape(batch, seq, UA_WIDTH)
    in_specs = [pl.BlockSpec(memory_space=pltpu.SMEM)]
    args = [slopes]
    scratch = []
    keeps = tuple(min(win, seq) for win, _ in A_GROUPS)
    assert max(keeps) <= A_ST
    for g, (win, dil) in enumerate(A_GROUPS):
        assert win == A_BLOCK * dil and A_ST % win == 0
        qcb = (QA_OFF + g * A_GW) // A_E
        kcb = (KV_OFF + g * 2 * A_GW) // A_E
        vcb = kcb + A_HPG
        per = A_ST // win

        def cur(cb):
            return pl.BlockSpec((None, A_ST, A_E), lambda b, st, hh, cb=cb: (b, st, cb + hh))

        def prv(cb, win=win, per=per):
            return pl.BlockSpec((None, win, A_E),
                                lambda b, st, hh, cb=cb: (b, jnp.maximum(st * per - 1, 0), cb + hh))

        in_specs += [cur(qcb), cur(kcb), cur(vcb), prv(kcb), prv(vcb)]
        args += [u3] * 5
        scratch += [pltpu.VMEM((A_ST, A_E), F32), pltpu.VMEM((A_ST, A_E), F32)]
    n_st = seq // A_ST
    out_specs = [pl.BlockSpec((None, A_ST, A_E), lambda b, st, hh: (b, st, hh))]
    out_shape = [jax.ShapeDtypeStruct((batch, seq, A_GW), F32)]
    for keep in keeps:
        out_specs.append(pl.BlockSpec((None, keep * SUBLANE, A_E), lambda b, st, hh: (b, 0, 0),
                                      pipeline_mode=pl.Buffered(1)))
        out_shape.append(jax.ShapeDtypeStruct((batch, keep * SUBLANE, A_E), F32))
    outs = pl.pallas_call(
        functools.partial(_attn_prompt_kernel, n_st=n_st, keeps=keeps),
        grid=(batch, n_st, A_HPG),
        in_specs=in_specs,
        out_specs=out_specs,
        out_shape=out_shape,
        scratch_shapes=scratch,
        compiler_params=_cparams(("parallel", "arbitrary", "arbitrary")),
        name="attn_prompt",
    )(*args)
    rows = [kv.reshape(1, batch, keep, 2, A_HPG, A_E) for kv, keep in zip(outs[1:], keeps)]
    return outs[0], rows


def _pick_rows(rows):
    sub = lax.broadcasted_iota(jnp.int32, (SUBLANE, A_E), 0)
    out = jnp.zeros((SUBLANE, A_E), F32)
    for i, r in enumerate(rows):
        out = jnp.where(sub == i, jnp.broadcast_to(r, (SUBLANE, A_E)), out)
    return out


def _attn_sample_kernel(slopes_ref, q_ref, kv_ref, c0_ref, c1_ref, c2_ref, o_ref, *kvo_refs, dec_seq, nseq):
    scale = A_E ** -0.5
    caches = (c0_ref, c1_ref, c2_ref)
    for g, kvo_ref in enumerate(kvo_refs):
        for t in range(nseq * dec_seq):
            kvo_ref[t * SUBLANE:(t + 1) * SUBLANE, :] = _pick_rows(
                [kv_ref[t:t + 1, g * 2 * A_GW + c * A_E:g * 2 * A_GW + (c + 1) * A_E] for c in range(2 * A_HPG)])
    n_keys = A_BLOCK * SUBLANE
    sub = lax.broadcasted_iota(jnp.int32, (SUBLANE, n_keys), 0)
    lane = lax.broadcasted_iota(jnp.int32, (SUBLANE, n_keys), 1)
    tile_row = lane & (SUBLANE - 1)
    m_idx = lane >> 3
    own_k = tile_row == sub
    sub1 = lax.broadcasted_iota(jnp.int32, (SUBLANE, 1), 0)
    dn_nt = (((1,), (1,)), ((), ()))
    slope8s = []
    for g in range(len(A_GROUPS)):
        slope8 = jnp.zeros((SUBLANE, 1), F32)
        for h in range(A_HPG):
            slope8 = jnp.where(sub1 == h, slopes_ref[g * A_HPG + h], slope8)
        slope8s.append(slope8)

    def keys(bb, s, g):
        res = 0 if A_GROUPS[g][1] == 1 else s
        return caches[g][bb, :, res, :, :].reshape(n_keys, A_E).astype(BF16)

    units = [(bb, s, g) for bb in range(nseq) for s in range(dec_seq) for g in range(len(A_GROUPS))]
    q8s = [_pick_rows([q_ref[bb * dec_seq + s:bb * dec_seq + s + 1, g * A_GW + h * A_E:g * A_GW + (h + 1) * A_E]
                       for h in range(A_HPG)]) for bb, s, g in units]
    scs = [lax.dot_general(q8.astype(BF16), keys(*u), dn_nt, preferred_element_type=F32) * scale
           for q8, u in zip(q8s, units)]
    ms, dens, pvs, news = [], [], [], []
    for (bb, s, g), q8, sc in zip(units, q8s, scs):
        win, dil = A_GROUPS[g]
        row = bb * dec_seq + s
        slope8 = slope8s[g]
        if dil == 1:
            dist = (win + s - m_idx).astype(F32)
            valid = jnp.logical_and(own_k, m_idx >= s)
            new_rows = [(bb * dec_seq + t, float(s - t)) for t in range(s + 1)]
        else:
            dist = (dil * (win // dil - m_idx)).astype(F32)
            valid = own_k
            new_rows = [(row, 0.0)]
        sc = jnp.where(valid, sc - slope8 * dist, -jnp.inf)
        m = jnp.max(sc, axis=-1, keepdims=True)
        koff = g * 2 * A_GW
        new = []
        for nrow, ndist in new_rows:
            k8 = _pick_rows([kv_ref[nrow:nrow + 1, koff + h * A_E:koff + (h + 1) * A_E]
                             for h in range(A_HPG)])
            v8 = _pick_rows([kv_ref[nrow:nrow + 1, koff + A_GW + h * A_E:koff + A_GW + (h + 1) * A_E]
                             for h in range(A_HPG)])
            s_n = jnp.sum(k8 * q8, axis=-1, keepdims=True) * scale - slope8 * ndist
            m = jnp.maximum(m, s_n)
            new.append((s_n, v8))
        p = jnp.exp(sc - m)
        ms.append(m)
        dens.append(jnp.sum(p, axis=-1, keepdims=True))
        pvs.append(pltpu.roll(p, A_HPG, 1).astype(BF16))
        news.append(new)
    accs = [jnp.dot(pv, keys(*u), preferred_element_type=F32) for pv, u in zip(pvs, units)]

    out_tiles = [jnp.zeros((nseq * dec_seq, A_E), F32) for _ in range(A_HPG)]
    out_sub = lax.broadcasted_iota(jnp.int32, (nseq * dec_seq, A_E), 0)
    n_g = len(A_GROUPS)
    for ui in range(0, len(units), n_g):
        bb, s, _ = units[ui]
        outs, lses = [], []
        for m, den, acc, new in zip(ms[ui:ui + n_g], dens[ui:ui + n_g], accs[ui:ui + n_g], news[ui:ui + n_g]):
            for s_n, v8 in new:
                p_n = jnp.exp(s_n - m)
                den = den + p_n
                acc = acc + p_n * v8
            outs.append(acc / den)
            lses.append(m + jnp.log(den))
        mx = functools.reduce(jnp.maximum, lses)
        ws = [jnp.exp(l - mx) for l in lses]
        o8 = sum(w * o for w, o in zip(ws, outs)) / sum(ws)
        for h in range(A_HPG):
            out_tiles[h] = jnp.where(out_sub == bb * dec_seq + s,
                                     jnp.broadcast_to(o8[h:h + 1, :], (nseq * dec_seq, A_E)), out_tiles[h])
    for h in range(A_HPG):
        o_ref[:, h * A_E:(h + 1) * A_E] = out_tiles[h]


def _attn_sample(slopes, u, caches, *, batch, dec_seq):
    nseq = SUBLANE // dec_seq
    assert nseq * dec_seq == SUBLANE and batch % nseq == 0
    views, specs = [], []
    for (win, dil), cache in zip(A_GROUPS, caches):
        wb = cache.shape[1]
        assert wb == win and wb // dil == A_BLOCK and (dil == 1 or dec_seq <= dil)
        views.append(cache.reshape(batch, wb // dil, dil, SUBLANE, A_E))
        res = 1 if dil == 1 else dec_seq
        specs.append(pl.BlockSpec((nseq, A_BLOCK, res, SUBLANE, A_E), lambda i: (i, 0, 0, 0, 0)))
    n_q = len(A_GROUPS) * A_GW
    rows = nseq * dec_seq
    outs = pl.pallas_call(
        functools.partial(_attn_sample_kernel, dec_seq=dec_seq, nseq=nseq),
        grid=(batch // nseq,),
        in_specs=[
            pl.BlockSpec(memory_space=pltpu.SMEM),
            pl.BlockSpec((rows, n_q), lambda i: (i, QA_OFF // n_q)),
            pl.BlockSpec((rows, 2 * n_q), lambda i: (i, KV_OFF // (2 * n_q))),
        ] + specs,
        out_specs=[pl.BlockSpec((rows, A_GW), lambda i: (i, 0))]
        + [pl.BlockSpec((rows * SUBLANE, A_E), lambda i: (i, 0))] * len(A_GROUPS),
        out_shape=[jax.ShapeDtypeStruct((batch * dec_seq, A_GW), F32)]
        + [jax.ShapeDtypeStruct((batch * dec_seq * SUBLANE, A_E), F32)] * len(A_GROUPS),
        compiler_params=_cparams(("parallel",)),
        name="attn_sample",
    )(slopes, u, u, *views)
    return outs[0], [kv.reshape(1, batch, dec_seq, 2, A_HPG, A_E) for kv in outs[1:]]


def _bdot(a, b, dn=(((1,), (0,)), ((), ()))):
    return lax.dot_general(a.astype(BF16), b.astype(BF16), dn, preferred_element_type=F32)


def _gdn_kernel(qkv_ref, z_ref, ba_ref, cw_ref, alog_ref, dtb_ref, nw_ref, conv0_ref, s0_ref,
                o_ref, sout_ref, cb_ref, s_ref, *, tb, cc, nc, bpb, flat):
    n = pl.program_id(1)

    def seq_rows(bi):
        return (slice(bi * tb, (bi + 1) * tb),) if flat else (bi, slice(None))

    @pl.when(n == 0)
    def _():
        s_ref[...] = s0_ref[...]
        cb_ref[:, 0:SUBLANE, :] = conv0_ref[...]

    rows = lax.broadcasted_iota(jnp.int32, (cc, 1), 0)
    live = rows < tb
    ii = lax.broadcasted_iota(jnp.int32, (cc, cc), 0)
    jj = lax.broadcasted_iota(jnp.int32, (cc, cc), 1)
    incl = ii >= jj
    strict = ii > jj
    tri = incl.astype(F32)
    cw = cw_ref[...]
    first = SUBLANE - (G_CONV - 1)
    dn_nt = (((1,), (1,)), ((), ()))
    dn_tn = (((0,), (0,)), ((), ()))
    eye = ii == jj

    gs = bpb
    for b0 in range(0, bpb, gs):
        qs, ks, kbs, decays, rhss, e_gcs, e_rests, e_ends = [], [], [], [], [], [], [], []
        idx = range(gs * G_HEADS)
        for bi in range(b0, b0 + gs):
            x = qkv_ref[seq_rows(bi)].astype(F32)
            cb_ref[bi, SUBLANE:SUBLANE + tb, :] = x
            if tb < cc:
                cb_ref[bi, SUBLANE + tb:SUBLANE + cc, :] = jnp.zeros((cc - tb, G_CONV_CH), F32)
            y = cb_ref[bi, SUBLANE:SUBLANE + cc, :] * cw[G_CONV - 1:G_CONV, :]
            for j in range(G_CONV - 2, -1, -1):
                y = y + cb_ref[bi, first + j:first + j + cc, :] * cw[j:j + 1, :]
            if nc > 1:
                cb_ref[bi, 0:SUBLANE, :] = x[tb - SUBLANE:tb, :]
            y = y * jax.nn.sigmoid(y)

            ba = ba_ref[seq_rows(bi)]
            if tb < cc:
                ba = jnp.concatenate([ba, jnp.zeros((cc - tb, LANE), F32)], axis=0)
            beta_t = jnp.where(live, jax.nn.sigmoid(ba), 0.0)
            xs = ba + dtb_ref[...]
            softplus = jnp.maximum(xs, 0.0) + jnp.log(1.0 + jnp.exp(-jnp.abs(xs)))
            g_t = jnp.where(live, -jnp.exp(alog_ref[...]) * softplus, 0.0)
            gc = jnp.dot(tri, g_t, precision=HIGHEST, preferred_element_type=F32)
            gc_t = jnp.concatenate([gc, jnp.zeros((LANE - cc, LANE), F32)], axis=0).T
            gc_last = gc[cc - 1:cc, :]
            e_gc = jnp.exp(gc)
            e_rest = jnp.exp(gc_last - gc)
            e_end = jnp.exp(gc_last)
            for h in range(G_HEADS):
                sl = slice(h * G_DK, (h + 1) * G_DK)
                gl = slice(G_HEADS + h, G_HEADS + h + 1)
                q = y[:, sl]
                k = y[:, G_QK + h * G_DK:G_QK + (h + 1) * G_DK]
                v = y[:, 2 * G_QK + h * G_DV:2 * G_QK + (h + 1) * G_DV]
                q = q * lax.rsqrt(jnp.sum(q * q, axis=-1, keepdims=True) + NORM_EPS) * (G_DK ** -0.5)
                k = k * lax.rsqrt(jnp.sum(k * k, axis=-1, keepdims=True) + NORM_EPS)
                if tb < cc:
                    q = jnp.where(live, q, 0.0)
                    k = jnp.where(live, k, 0.0)
                    v = jnp.where(live, v, 0.0)
                beta = beta_t[:, h:h + 1]
                kb = k * beta
                qs.append(q)
                ks.append(k)
                kbs.append(kb)
                decays.append(jnp.exp(jnp.where(
                    incl, gc[:, gl] - gc_t[G_HEADS + h:G_HEADS + h + 1, 0:cc], -jnp.inf)))
                rhss.append(jnp.concatenate([v * beta, kb * e_gc[:, gl]], axis=1))
                e_gcs.append(e_gc[:, gl])
                e_rests.append(e_rest[:, gl])
                e_ends.append(e_end[:, gl])

        kq = [_bdot(jnp.concatenate([kbs[i], qs[i]], axis=0), ks[i], dn_nt) for i in idx]
        a_low = [jnp.where(strict, kq[i][:cc] * decays[i], 0.0) for i in idx]
        qk = [kq[i][cc:] * decays[i] for i in idx]
        inv = [jnp.where(eye, 1.0, -a_low[i]) for i in idx]
        pw = a_low
        span = 2
        while span < cc:
            pw = [_bdot(pw[i], pw[i]) for i in idx]
            inv = [inv[i] + _bdot(pw[i], inv[i]) for i in idx]
            span *= 2
        rhss = [_bdot(inv[i], rhss[i]) for i in idx]
        s_old = [s_ref[b0 + i // G_HEADS, i % G_HEADS] for i in idx]
        ws = [_bdot(jnp.concatenate([rhss[i][:, G_DV:], qs[i] * e_gcs[i]], axis=0), s_old[i]) for i in idx]
        v_new = [rhss[i][:, :G_DV] - ws[i][:cc] for i in idx]
        for i in idx:
            s_ref[b0 + i // G_HEADS, i % G_HEADS] = (
                s_old[i] * e_ends[i] + _bdot(ks[i] * e_rests[i], v_new[i], dn_tn))
        o = [ws[i][cc:] + _bdot(qk[i], v_new[i]) for i in idx]
        for i in idx:
            bi, h = b0 + i // G_HEADS, i % G_HEADS
            sl = slice(h * G_DV, (h + 1) * G_DV)
            o_h = o[i] * lax.rsqrt(jnp.mean(o[i] * o[i], axis=-1, keepdims=True) + NORM_EPS) * nw_ref[...]
            zz = z_ref[seq_rows(bi) + (sl,)].astype(F32)
            o_ref[seq_rows(bi) + (sl,)] = o_h[:tb] * (zz * jax.nn.sigmoid(zz))

    @pl.when(n == nc - 1)
    def _():
        sout_ref[...] = s_ref[...]


def _gdn(u, ba, conv0, s0, conv_w, alog_t, dtb_t, norm_w, *, batch, seq, bpb):
    cc = G_CHUNK if seq >= G_CHUNK else SUBLANE
    tb = min(seq, cc)
    nc = seq // tb
    assert nc * tb == seq and (nc == 1 or tb == cc) and batch % bpb == 0
    flat = nc == 1
    if flat:
        assert (bpb * tb) % SUBLANE == 0
        uv, bav, o_rows = u, ba, batch * seq

        def rows(width, col):
            return pl.BlockSpec((bpb * tb, width), lambda b, n: (b, col))
    else:
        uv, bav, o_rows = u.reshape(batch, seq, UB_WIDTH), ba.reshape(batch, seq, LANE), batch

        def rows(width, col):
            return pl.BlockSpec((bpb, tb, width), lambda b, n: (b, n, col))
    o_shape = (o_rows, G_QK) if flat else (batch, seq, G_QK)
    conv0p = jnp.concatenate(
        [jnp.zeros((batch, SUBLANE - (G_CONV - 1), G_CONV_CH), F32), conv0.astype(F32)], axis=1)
    const2 = lambda b, n: (0, 0)
    o, s_new = pl.pallas_call(
        functools.partial(_gdn_kernel, tb=tb, cc=cc, nc=nc, bpb=bpb, flat=flat),
        grid=(batch // bpb, nc),
        in_specs=[
            rows(G_CONV_CH, QKVB_OFF // G_CONV_CH),
            rows(G_QK, Z_OFF // G_QK),
            rows(LANE, 0),
            pl.BlockSpec((G_CONV, G_CONV_CH), const2),
            pl.BlockSpec((1, LANE), const2),
            pl.BlockSpec((1, LANE), const2),
            pl.BlockSpec((1, G_DV), const2),
            pl.BlockSpec((bpb, SUBLANE, G_CONV_CH), lambda b, n: (b, 0, 0)),
            pl.BlockSpec((bpb, G_HEADS, G_DK, G_DV), lambda b, n: (b, 0, 0, 0)),
        ],
        out_specs=[
            rows(G_QK, 0),
            pl.BlockSpec((bpb, G_HEADS, G_DK, G_DV), lambda b, n: (b, 0, 0, 0)),
        ],
        out_shape=[
            jax.ShapeDtypeStruct(o_shape, F32),
            jax.ShapeDtypeStruct((batch, G_HEADS, G_DK, G_DV), F32),
        ],
        scratch_shapes=[
            pltpu.VMEM((bpb, SUBLANE + cc, G_CONV_CH), F32),
            pltpu.VMEM((bpb, G_HEADS, G_DK, G_DV), F32),
        ],
        compiler_params=_cparams(("parallel", "arbitrary")),
        name="gdn",
    )(uv, uv, bav, conv_w, alog_t, dtb_t, norm_w.reshape(1, G_DV), conv0p, s0)
    return o.reshape(batch * seq, G_QK), s_new


def _mix_kernel(x_ref, oa_ref, ob_ref, ga_ref, gb_ref, wa_ref, wb_ref, wo_ref, o_ref):
    pa = jnp.dot(oa_ref[...].astype(BF16), wa_ref[...], preferred_element_type=F32)
    pb = jnp.dot(ob_ref[...].astype(BF16), wb_ref[...], preferred_element_type=F32)
    merged = (jax.nn.sigmoid(ga_ref[...].astype(F32)) * pa
              + jax.nn.sigmoid(gb_ref[...].astype(F32)) * pb)
    o_ref[...] = x_ref[...] + jnp.dot(merged.astype(BF16), wo_ref[...], preferred_element_type=F32)


def _mix(x, oa, ob, u, wa, wb, wo, *, tm):
    n, d = x.shape
    const = lambda i: (0, 0)
    return pl.pallas_call(
        _mix_kernel,
        grid=(n // tm,),
        in_specs=[
            pl.BlockSpec((tm, d), lambda i: (i, 0)),
            pl.BlockSpec((tm, A_GW), lambda i: (i, 0)),
            pl.BlockSpec((tm, G_QK), lambda i: (i, 0)),
            pl.BlockSpec((tm, d), lambda i: (i, GATE_OFF // d)),
            pl.BlockSpec((tm, d), lambda i: (i, GATE_OFF // d + 1)),
            pl.BlockSpec((A_GW, d), const),
            pl.BlockSpec((G_QK, d), const),
            pl.BlockSpec((d, d), const),
        ],
        out_specs=pl.BlockSpec((tm, d), lambda i: (i, 0)),
        out_shape=jax.ShapeDtypeStruct((n, d), F32),
        compiler_params=_cparams(("parallel",)),
        name="mix",
    )(x, oa, ob, u, u, wa, wb, wo)


def _w_in_sources():
    aw = len(A_GROUPS) * A_GW
    src_q, src_k, src_v = 0, aw, 2 * aw
    src_qkvb = 3 * aw
    src_z = src_qkvb + G_CONV_CH
    src_ba = src_z + G_QK
    src_gate = src_ba + 2 * G_HEADS
    blocks = [src_qkvb + W_CB * t for t in range(G_CONV_CH // W_CB)]
    blocks += [src_z + W_CB * t for t in range(G_QK // W_CB)]
    blocks += [src_gate + W_CB * t for t in range(2 * D_MODEL // W_CB)]
    for g in range(len(A_GROUPS)):
        blocks += [src_k + g * A_GW, src_v + g * A_GW]
    blocks += [src_q + W_CB * t for t in range(aw // W_CB)]
    assert len(blocks) * W_CB == U_WIDTH and A_GW == W_CB
    assert all(b % SUBLANE == 0 for b in blocks)
    return blocks, src_ba


def _permute_cast_kernel(src_ref, w_ref, o_ref):
    del src_ref
    o_ref[...] = w_ref[...].astype(BF16)


def _permute_w_in(wt):
    d = wt.shape[1]
    blocks, _ = _w_in_sources()
    return pl.pallas_call(
        _permute_cast_kernel,
        grid_spec=pltpu.PrefetchScalarGridSpec(
            num_scalar_prefetch=1,
            grid=(len(blocks),),
            in_specs=[pl.BlockSpec((pl.Element(W_CB), pl.Element(d)),
                                   lambda j, src: (pl.multiple_of(src[j], SUBLANE), 0))],
            out_specs=pl.BlockSpec((W_CB, d), lambda j, src: (j, 0)),
        ),
        out_shape=jax.ShapeDtypeStruct((U_WIDTH, d), BF16),
        compiler_params=_cparams(("parallel",)),
        name="permute_w_in",
    )(jnp.asarray(blocks, jnp.int32), wt)


def _prep_weights(w_in, gdn_a_log, gdn_dt_bias):
    wt = w_in.T
    w_main = _permute_w_in(wt)
    src_ba = _w_in_sources()[1]
    n_ba = 2 * G_HEADS
    ba = wt[src_ba:src_ba + n_ba, :]
    ba_hi = ba.astype(BF16)
    ba_lo = (ba - ba_hi.astype(F32)).astype(BF16)
    w_ba = jnp.concatenate([ba_hi, ba_lo, jnp.zeros((MXU_COLS - 2 * n_ba, wt.shape[1]), BF16)], axis=0)
    pad = (G_HEADS, LANE - n_ba)
    alog_t = jnp.pad(gdn_a_log.astype(F32), pad).reshape(1, LANE)
    dtb_t = jnp.pad(gdn_dt_bias.astype(F32), pad).reshape(1, LANE)
    return w_main, w_ba, alog_t, dtb_t


def _layer(x, batch, seq, conv0, s0, attn_fn, p, *, tm, gdn_bpb, b_dtype):
    tm = min(tm, x.shape[0])
    gdn_bpb = min(gdn_bpb, batch)
    tail = b_dtype != F32
    x1 = _ffn(x, p["norm_ffn1"], p["w_ffn1_gu"], p["w_ffn1_down"], tm=tm)
    ub, ua, ba, *rest = _inproj(x1, p["norm_mix"], p["w_main"], p["w_ba"],
                                tm=tm, b_dtype=b_dtype, tail=tail)
    oa, rows = attn_fn(ua)
    ob, s_new = _gdn(ub, ba, conv0, s0, p["conv_w"], p["alog_t"], p["dtb_t"], p["gdn_norm"],
                     batch=batch, seq=seq, bpb=gdn_bpb)
    x2 = _mix(x1, oa.reshape(batch * seq, A_GW), ob.reshape(batch * seq, G_QK), ub,
              p["w_proj_a"], p["w_proj_b"], p["w_out"], tm=min(tm, 1024))
    y = _ffn(x2, p["norm_ffn2"], p["w_ffn2_gu"], p["w_ffn2_down"], p["norm_out"], tm=tm)
    if tail:
        assert seq % tm == 0 and seq >= SUBLANE
        last = rest[0].reshape(batch, seq // tm, SUBLANE, G_CONV_CH)[:, -1]
        conv_new = last[:, SUBLANE - (G_CONV - 1):, :][None]
    else:
        qkvb = ub[:, QKVB_OFF:QKVB_OFF + G_CONV_CH].reshape(batch, seq, G_CONV_CH)
        conv_new = qkvb[:, seq - (G_CONV - 1):, :][None]
    return y.reshape(batch, seq, D_MODEL), rows, conv_new, s_new[None]


def kernel(x_prompt, x_sample, cache_kv_w128, cache_kv_w512, cache_kv_w2048, state_conv, state_ssm,
           norm_ffn1, w_ffn1_gu, w_ffn1_down, norm_mix, w_in, conv_w, gdn_a_log, gdn_dt_bias, gdn_norm,
           w_proj_a, w_proj_b, w_out, norm_ffn2, w_ffn2_gu, w_ffn2_down, norm_out):
    assert w_in.shape[0] == 1, "single layer"
    n_heads = len(A_GROUPS) * A_HPG
    slopes = jnp.exp2(-8.0 * jnp.arange(1, n_heads + 1, dtype=F32) / n_heads)
    w_main, w_ba, alog_t, dtb_t = _prep_weights(w_in[0], gdn_a_log[0], gdn_dt_bias[0])
    p = dict(
        norm_ffn1=norm_ffn1[0], w_ffn1_gu=w_ffn1_gu[0], w_ffn1_down=w_ffn1_down[0],
        norm_mix=norm_mix[0], w_main=w_main, w_ba=w_ba, conv_w=conv_w[0],
        alog_t=alog_t, dtb_t=dtb_t, gdn_norm=gdn_norm[0],
        w_proj_a=w_proj_a[0].astype(BF16), w_proj_b=w_proj_b[0].astype(BF16), w_out=w_out[0].astype(BF16),
        norm_ffn2=norm_ffn2[0], w_ffn2_gu=w_ffn2_gu[0], w_ffn2_down=w_ffn2_down[0],
        norm_out=norm_out,
    )
    bp, tp, d = x_prompt.shape
    bs, ts, _ = x_sample.shape

    conv0_p = jnp.zeros((bp, G_CONV - 1, G_CONV_CH), F32)
    ssm0_p = jnp.zeros((bp, G_HEADS, G_DK, G_DV), F32)
    yp, rows_p, conv_p, ssm_p = _layer(
        x_prompt.reshape(bp * tp, d), bp, tp, conv0_p, ssm0_p,
        functools.partial(_attn_prompt, slopes, batch=bp, seq=tp), p, tm=2048, gdn_bpb=4, b_dtype=BF16)

    caches = (cache_kv_w128[0], cache_kv_w512[0], cache_kv_w2048[0])
    ys, rows_s, conv_s, ssm_s = _layer(
        x_sample.reshape(bs * ts, d), bs, ts, state_conv[0], state_ssm[0],
        lambda u: _attn_sample(slopes, u, caches, batch=bs, dec_seq=ts), p, tm=512, gdn_bpb=8, b_dtype=F32)

    return (yp, ys, rows_p[0], rows_p[1], rows_p[2], conv_p, ssm_p,
            rows_s[0], rows_s[1], rows_s[2], conv_s, ssm_s)
```

```python
import functools

import jax
import jax.numpy as jnp
from jax import lax
from jax.experimental import pallas as pl
from jax.experimental.pallas import tpu as pltpu

F32 = jnp.float32
BF16 = jnp.bfloat16
HIGHEST = lax.Precision.HIGHEST

D_MODEL = 1024
D_FF = 2816
NORM_EPS = 1e-6
A_GROUPS = ((128, 1), (512, 4), (2048, 16))
A_HPG = 4
A_E = 128
A_BLOCK = 128
A_GW = A_HPG * A_E
A_ST = 2048
G_HEADS = 8
G_DK = 128
G_DV = 128
G_QK = G_HEADS * G_DK
G_CONV_CH = 3 * G_QK
G_CONV = 4
G_CHUNK = 64

QKVB_OFF = 0
Z_OFF = 3072
GATE_OFF = 4096
UB_WIDTH = 6144
KV_OFF = 0
QA_OFF = 3072
UA_WIDTH = 4608
U_WIDTH = UB_WIDTH + UA_WIDTH
W_CB = 512

VMEM_LIMIT = 56 * 1024 * 1024
LANE = 128
SUBLANE = 8
MXU_COLS = 256


def _cparams(sem):
    return pltpu.CompilerParams(dimension_semantics=sem, vmem_limit_bytes=VMEM_LIMIT)


def _rms_rows(x, w):
    return x * lax.rsqrt(jnp.mean(x * x, axis=-1, keepdims=True) + NORM_EPS) * w


def _ffn_kernel(x_ref, nw_ref, wg_ref, wu_ref, wd_ref, *rest, n_ff, final_norm):
    if final_norm:
        onw_ref, o_ref, hb_ref = rest
    else:
        o_ref, hb_ref = rest
    j = pl.program_id(1)

    @pl.when(j == 0)
    def _():
        hb_ref[...] = _rms_rows(x_ref[...], nw_ref[...]).astype(BF16)
        o_ref[...] = jnp.zeros_like(o_ref)

    hb = hb_ref[...]
    g = jnp.dot(hb, wg_ref[...].astype(BF16), preferred_element_type=F32)
    u = jnp.dot(hb, wu_ref[...].astype(BF16), preferred_element_type=F32)
    a = (g * jax.nn.sigmoid(g) * u).astype(BF16)
    o_ref[...] += jnp.dot(a, wd_ref[...].astype(BF16), preferred_element_type=F32)

    @pl.when(j == n_ff - 1)
    def _():
        y = x_ref[...] + 0.5 * o_ref[...]
        if final_norm:
            y = _rms_rows(y, onw_ref[...])
        o_ref[...] = y


def _ffn(x, norm_w, w_gu, w_down, out_norm_w=None, *, tm, tf=256):
    n, d = x.shape
    n_ff = D_FF // tf
    final_norm = out_norm_w is not None
    in_specs = [
        pl.BlockSpec((tm, d), lambda i, j: (i, 0)),
        pl.BlockSpec((1, d), lambda i, j: (0, 0)),
        pl.BlockSpec((d, tf), lambda i, j: (0, j)),
        pl.BlockSpec((d, tf), lambda i, j: (0, j + n_ff)),
        pl.BlockSpec((tf, d), lambda i, j: (j, 0)),
    ]
    args = [x, norm_w.reshape(1, d), w_gu, w_gu, w_down]
    if final_norm:
        in_specs.append(pl.BlockSpec((1, d), lambda i, j: (0, 0)))
        args.append(out_norm_w.reshape(1, d))
    return pl.pallas_call(
        functools.partial(_ffn_kernel, n_ff=n_ff, final_norm=final_norm),
        grid=(n // tm, n_ff),
        in_specs=in_specs,
        out_specs=pl.BlockSpec((tm, d), lambda i, j: (i, 0)),
        out_shape=jax.ShapeDtypeStruct((n, d), F32),
        scratch_shapes=[pltpu.VMEM((tm, d), BF16)],
        compiler_params=_cparams(("parallel", "arbitrary")),
        name="ffn_final" if final_norm else "ffn",
    )(*args)


_DN_NT = (((1,), (1,)), ((), ()))


def _inproj_kernel(x_ref, nw_ref, w_ref, wba_ref, ub_ref, ua_ref, ba_ref, *rest, nb, nq, tail):
    if tail:
        tail_ref, hb_ref = rest
    else:
        (hb_ref,) = rest
    j = pl.program_id(1)

    @pl.when(j == 0)
    def _():
        h = _rms_rows(x_ref[...], nw_ref[...])
        hb = h.astype(BF16)
        hl = (h - hb.astype(F32)).astype(BF16)
        hb_ref[...] = hb
        n_ba = 2 * G_HEADS
        r_hi = lax.dot_general(hb, wba_ref[...], _DN_NT, preferred_element_type=F32)[:, :LANE]
        r_lo = lax.dot_general(hl, wba_ref[...], _DN_NT, preferred_element_type=F32)[:, :LANE]
        lane = lax.broadcasted_iota(jnp.int32, r_hi.shape, 1)
        ba_ref[...] = jnp.where(lane < n_ba, r_hi + pltpu.roll(r_hi, LANE - n_ba, 1) + r_lo, 0.0)

    def project(rows):
        return lax.dot_general(hb_ref[rows, :], w_ref[...], _DN_NT, preferred_element_type=F32)

    @pl.when(j < nb)
    def _():
        ub_ref[...] = project(slice(None)).astype(ub_ref.dtype)

    @pl.when(j >= nb)
    def _():
        ua_ref[...] = project(slice(None))

    if tail:
        @pl.when(j < nq)
        def _():
            tm = hb_ref.shape[0]
            tail_ref[...] = project(slice(tm - 2 * SUBLANE, tm))[SUBLANE:, :]


def _inproj(x, norm_w, w_main, w_ba, *, tm, b_dtype, tail, tn=768):
    n, d = x.shape
    nb, nq = UB_WIDTH // tn, G_CONV_CH // tn
    assert nb * tn == UB_WIDTH and nq * tn == G_CONV_CH and U_WIDTH % tn == 0 and QKVB_OFF == 0
    out_specs = [
        pl.BlockSpec((tm, tn), lambda i, j: (i, jnp.minimum(j, nb - 1))),
        pl.BlockSpec((tm, tn), lambda i, j: (i, jnp.maximum(j - nb, 0))),
        pl.BlockSpec((tm, LANE), lambda i, j: (i, 0)),
    ]
    out_shape = [
        jax.ShapeDtypeStruct((n, UB_WIDTH), b_dtype),
        jax.ShapeDtypeStruct((n, UA_WIDTH), F32),
        jax.ShapeDtypeStruct((n, LANE), F32),
    ]
    if tail:
        out_specs.append(pl.BlockSpec((SUBLANE, tn), lambda i, j: (i, jnp.minimum(j, nq - 1))))
        out_shape.append(jax.ShapeDtypeStruct((n // tm * SUBLANE, G_CONV_CH), F32))
    return pl.pallas_call(
        functools.partial(_inproj_kernel, nb=nb, nq=nq, tail=tail),
        grid=(n // tm, U_WIDTH // tn),
        in_specs=[
            pl.BlockSpec((tm, d), lambda i, j: (i, 0)),
            pl.BlockSpec((1, d), lambda i, j: (0, 0)),
            pl.BlockSpec((tn, d), lambda i, j: (j, 0)),
            pl.BlockSpec((MXU_COLS, d), lambda i, j: (0, 0)),
        ],
        out_specs=out_specs,
        out_shape=out_shape,
        scratch_shapes=[pltpu.VMEM((tm, d), BF16)],
        compiler_params=_cparams(("parallel", "arbitrary")),
        name="inproj",
    )(x, norm_w.reshape(1, d), w_main, w_ba)


def _rows(start, size, stride):
    return pl.ds(start, size) if stride == 1 else pl.ds(start, size, stride=stride)


def _attn_prompt_kernel(slopes_ref, *refs, n_st, keeps):
    n_g = len(A_GROUPS)
    ins, o_ref = refs[:5 * n_g], refs[5 * n_g]
    kv_refs, scr = refs[5 * n_g + 1:6 * n_g + 1], refs[6 * n_g + 1:]
    st = pl.program_id(1)
    hh = pl.program_id(2)

    @pl.when(st == n_st - 1)
    def _():
        for g in range(n_g):
            keep = keeps[g]
            kc_ref, vc_ref = ins[5 * g + 1], ins[5 * g + 2]
            kv_refs[g][pl.ds(hh, keep, stride=SUBLANE), :] = kc_ref[A_ST - keep:, :]
            kv_refs[g][pl.ds(A_HPG + hh, keep, stride=SUBLANE), :] = vc_ref[A_ST - keep:, :]

    scale = A_E ** -0.5
    qi = lax.broadcasted_iota(jnp.int32, (A_BLOCK, 2 * A_BLOCK), 0)
    kj = lax.broadcasted_iota(jnp.int32, (A_BLOCK, 2 * A_BLOCK), 1)
    delta = A_BLOCK + qi - kj
    band = jnp.logical_and(delta >= 0, delta <= A_BLOCK)
    band_first = jnp.logical_and(band, jnp.logical_or(kj >= A_BLOCK, st > 0))
    dn_nt = (((1,), (1,)), ((), ()))
    for g, (win, dil) in enumerate(A_GROUPS):
        q_ref, kc_ref, vc_ref, kp_ref, vp_ref = ins[5 * g:5 * g + 5]
        og_ref, lg_ref = scr[2 * g:2 * g + 2]
        slope = slopes_ref[g * A_HPG + hh]
        bias = -slope * (dil * delta).astype(F32)
        bias_mid = jnp.where(band, bias, -jnp.inf)
        bias_first = jnp.where(band_first, bias, -jnp.inf)

        def two_blocks(prev_ref, cur_ref, r, j):
            if j == 0:
                return jnp.concatenate([prev_ref[_rows(r, A_BLOCK, dil), :],
                                        cur_ref[_rows(r, A_BLOCK, dil), :]], axis=0)
            return cur_ref[_rows(r + (j - 1) * A_BLOCK * dil, 2 * A_BLOCK, dil), :]

        for r in range(dil):
            for j in range(A_ST // (A_BLOCK * dil)):
                row0 = r + j * A_BLOCK * dil
                q = q_ref[_rows(row0, A_BLOCK, dil), :].astype(BF16)
                k2 = two_blocks(kp_ref, kc_ref, r, j).astype(BF16)
                v2 = two_blocks(vp_ref, vc_ref, r, j).astype(BF16)
                s = lax.dot_general(q, k2, dn_nt, preferred_element_type=F32) * scale
                s = s + (bias_first if j == 0 else bias_mid)
                m = jnp.max(s, axis=-1, keepdims=True)
                p = jnp.exp(s - m)
                den = jnp.sum(p, axis=-1, keepdims=True)
                acc = jnp.dot(p.astype(BF16), v2, preferred_element_type=F32)
                og_ref[_rows(row0, A_BLOCK, dil), :] = acc / den
                lg_ref[_rows(row0, A_BLOCK, dil), :] = jnp.broadcast_to(m + jnp.log(den), (A_BLOCK, A_E))
    chunk = 2 * A_BLOCK
    for c in range(A_ST // chunk):
        rs = slice(c * chunk, (c + 1) * chunk)
        ls = [scr[2 * g + 1][rs, :] for g in range(n_g)]
        mx = functools.reduce(jnp.maximum, ls)
        ws = [jnp.exp(l - mx) for l in ls]
        num = sum(w * scr[2 * g][rs, :] for g, w in enumerate(ws))
        o_ref[rs, :] = num / sum(ws)


def _attn_prompt(slopes, u, *, batch, seq):
    assert seq % A_ST == 0
    u3 = u.reshape(batch, seq, UA_WIDTH)
    in_specs = [pl.BlockSpec(memory_space=pltpu.SMEM)]
    args = [slopes]
    scratch = []
    keeps = tuple(min(win, seq) for win, _ in A_GROUPS)
    assert max(keeps) <= A_ST
    for g, (win, dil) in enumerate(A_GROUPS):
        assert win == A_BLOCK * dil and A_ST % win == 0
        qcb = (QA_OFF + g * A_GW) // A_E
        kcb = (KV_OFF + g * 2 * A_GW) // A_E
        vcb = kcb + A_HPG
        per = A_ST // win

        def cur(cb):
            return pl.BlockSpec((None, A_ST, A_E), lambda b, st, hh, cb=cb: (b, st, cb + hh))

        def prv(cb, win=win, per=per):
            return pl.BlockSpec((None, win, A_E),
                                lambda b, st, hh, cb=cb: (b, jnp.maximum(st * per - 1, 0), cb + hh))

        in_specs += [cur(qcb), cur(kcb), cur(vcb), prv(kcb), prv(vcb)]
        args += [u3] * 5
        scratch += [pltpu.VMEM((A_ST, A_E), F32), pltpu.VMEM((A_ST, A_E), F32)]
    n_st = seq // A_ST
    out_specs = [pl.BlockSpec((None, A_ST, A_E), lambda b, st, hh: (b, st, hh))]
    out_shape = [jax.ShapeDtypeStruct((batch, seq, A_GW), F32)]
    for keep in keeps:
        out_specs.append(pl.BlockSpec((None, keep * SUBLANE, A_E), lambda b, st, hh: (b, 0, 0),
                                      pipeline_mode=pl.Buffered(1)))
        out_shape.append(jax.ShapeDtypeStruct((batch, keep * SUBLANE, A_E), F32))
    outs = pl.pallas_call(
        functools.partial(_attn_prompt_kernel, n_st=n_st, keeps=keeps),
        grid=(batch, n_st, A_HPG),
        in_specs=in_specs,
        out_specs=out_specs,
        out_shape=out_shape,
        scratch_shapes=scratch,
        compiler_params=_cparams(("parallel", "arbitrary", "arbitrary")),
        name="attn_prompt",
    )(*args)
    rows = [kv.reshape(1, batch, keep, 2, A_HPG, A_E) for kv, keep in zip(outs[1:], keeps)]
    return outs[0], rows


def _pick_rows(rows):
    sub = lax.broadcasted_iota(jnp.int32, (SUBLANE, A_E), 0)
    out = jnp.zeros((SUBLANE, A_E), F32)
    for i, r in enumerate(rows):
        out = jnp.where(sub == i, jnp.broadcast_to(r, (SUBLANE, A_E)), out)
    return out


def _attn_sample_kernel(slopes_ref, q_ref, kv_ref, c0_ref, c1_ref, c2_ref, o_ref, *kvo_refs, dec_seq, nseq):
    scale = A_E ** -0.5
    caches = (c0_ref, c1_ref, c2_ref)
    for g, kvo_ref in enumerate(kvo_refs):
        for t in range(nseq * dec_seq):
            kvo_ref[t * SUBLANE:(t + 1) * SUBLANE, :] = _pick_rows(
                [kv_ref[t:t + 1, g * 2 * A_GW + c * A_E:g * 2 * A_GW + (c + 1) * A_E] for c in range(2 * A_HPG)])
    n_keys = A_BLOCK * SUBLANE
    sub = lax.broadcasted_iota(jnp.int32, (SUBLANE, n_keys), 0)
    lane = lax.broadcasted_iota(jnp.int32, (SUBLANE, n_keys), 1)
    tile_row = lane & (SUBLANE - 1)
    m_idx = lane >> 3
    own_k = tile_row == sub
    sub1 = lax.broadcasted_iota(jnp.int32, (SUBLANE, 1), 0)
    dn_nt = (((1,), (1,)), ((), ()))
    slope8s = []
    for g in range(len(A_GROUPS)):
        slope8 = jnp.zeros((SUBLANE, 1), F32)
        for h in range(A_HPG):
            slope8 = jnp.where(sub1 == h, slopes_ref[g * A_HPG + h], slope8)
        slope8s.append(slope8)

    def keys(bb, s, g):
        res = 0 if A_GROUPS[g][1] == 1 else s
        return caches[g][bb, :, res, :, :].reshape(n_keys, A_E).astype(BF16)

    units = [(bb, s, g) for bb in range(nseq) for s in range(dec_seq) for g in range(len(A_GROUPS))]
    q8s = [_pick_rows([q_ref[bb * dec_seq + s:bb * dec_seq + s + 1, g * A_GW + h * A_E:g * A_GW + (h + 1) * A_E]
                       for h in range(A_HPG)]) for bb, s, g in units]
    scs = [lax.dot_general(q8.astype(BF16), keys(*u), dn_nt, preferred_element_type=F32) * scale
           for q8, u in zip(q8s, units)]
    ms, dens, pvs, news = [], [], [], []
    for (bb, s, g), q8, sc in zip(units, q8s, scs):
        win, dil = A_GROUPS[g]
        row = bb * dec_seq + s
        slope8 = slope8s[g]
        if dil == 1:
            dist = (win + s - m_idx).astype(F32)
            valid = jnp.logical_and(own_k, m_idx >= s)
            new_rows = [(bb * dec_seq + t, float(s - t)) for t in range(s + 1)]
        else:
            dist = (dil * (win // dil - m_idx)).astype(F32)
            valid = own_k
            new_rows = [(row, 0.0)]
        sc = jnp.where(valid, sc - slope8 * dist, -jnp.inf)
        m = jnp.max(sc, axis=-1, keepdims=True)
        koff = g * 2 * A_GW
        new = []
        for nrow, ndist in new_rows:
            k8 = _pick_rows([kv_ref[nrow:nrow + 1, koff + h * A_E:koff + (h + 1) * A_E]
                             for h in range(A_HPG)])
            v8 = _pick_rows([kv_ref[nrow:nrow + 1, koff + A_GW + h * A_E:koff + A_GW + (h + 1) * A_E]
                             for h in range(A_HPG)])
            s_n = jnp.sum(k8 * q8, axis=-1, keepdims=True) * scale - slope8 * ndist
            m = jnp.maximum(m, s_n)
            new.append((s_n, v8))
        p = jnp.exp(sc - m)
        ms.append(m)
        dens.append(jnp.sum(p, axis=-1, keepdims=True))
        pvs.append(pltpu.roll(p, A_HPG, 1).astype(BF16))
        news.append(new)
    accs = [jnp.dot(pv, keys(*u), preferred_element_type=F32) for pv, u in zip(pvs, units)]

    out_tiles = [jnp.zeros((nseq * dec_seq, A_E), F32) for _ in range(A_HPG)]
    out_sub = lax.broadcasted_iota(jnp.int32, (nseq * dec_seq, A_E), 0)
    n_g = len(A_GROUPS)
    for ui in range(0, len(units), n_g):
        bb, s, _ = units[ui]
        outs, lses = [], []
        for m, den, acc, new in zip(ms[ui:ui + n_g], dens[ui:ui + n_g], accs[ui:ui + n_g], news[ui:ui + n_g]):
            for s_n, v8 in new:
                p_n = jnp.exp(s_n - m)
                den = den + p_n
                acc = acc + p_n * v8
            outs.append(acc / den)
            lses.append(m + jnp.log(den))
        mx = functools.reduce(jnp.maximum, lses)
        ws = [jnp.exp(l - mx) for l in lses]
        o8 = sum(w * o for w, o in zip(ws, outs)) / sum(ws)
        for h in range(A_HPG):
            out_tiles[h] = jnp.where(out_sub == bb * dec_seq + s,
                                     jnp.broadcast_to(o8[h:h + 1, :], (nseq * dec_seq, A_E)), out_tiles[h])
    for h in range(A_HPG):
        o_ref[:, h * A_E:(h + 1) * A_E] = out_tiles[h]


def _attn_sample(slopes, u, caches, *, batch, dec_seq):
    nseq = SUBLANE // dec_seq
    assert nseq * dec_seq == SUBLANE and batch % nseq == 0
    views, specs = [], []
    for (win, dil), cache in zip(A_GROUPS, caches):
        wb = cache.shape[1]
        assert wb == win and wb // dil == A_BLOCK and (dil == 1 or dec_seq <= dil)
        views.append(cache.reshape(batch, wb // dil, dil, SUBLANE, A_E))
        res = 1 if dil == 1 else dec_seq
        specs.append(pl.BlockSpec((nseq, A_BLOCK, res, SUBLANE, A_E), lambda i: (i, 0, 0, 0, 0)))
    n_q = len(A_GROUPS) * A_GW
    rows = nseq * dec_seq
    outs = pl.pallas_call(
        functools.partial(_attn_sample_kernel, dec_seq=dec_seq, nseq=nseq),
        grid=(batch // nseq,),
        in_specs=[
            pl.BlockSpec(memory_space=pltpu.SMEM),
            pl.BlockSpec((rows, n_q), lambda i: (i, QA_OFF // n_q)),
            pl.BlockSpec((rows, 2 * n_q), lambda i: (i, KV_OFF // (2 * n_q))),
        ] + specs,
        out_specs=[pl.BlockSpec((rows, A_GW), lambda i: (i, 0))]
        + [pl.BlockSpec((rows * SUBLANE, A_E), lambda i: (i, 0))] * len(A_GROUPS),
        out_shape=[jax.ShapeDtypeStruct((batch * dec_seq, A_GW), F32)]
        + [jax.ShapeDtypeStruct((batch * dec_seq * SUBLANE, A_E), F32)] * len(A_GROUPS),
        compiler_params=_cparams(("parallel",)),
        name="attn_sample",
    )(slopes, u, u, *views)
    return outs[0], [kv.reshape(1, batch, dec_seq, 2, A_HPG, A_E) for kv in outs[1:]]


def _bdot(a, b, dn=(((1,), (0,)), ((), ()))):
    return lax.dot_general(a.astype(BF16), b.astype(BF16), dn, preferred_element_type=F32)


def _gdn_kernel(qkv_ref, z_ref, ba_ref, cw_ref, alog_ref, dtb_ref, nw_ref, conv0_ref, s0_ref,
                o_ref, sout_ref, cb_ref, s_ref, *, tb, cc, nc, bpb, flat):
    n = pl.program_id(1)

    def seq_rows(bi):
        return (slice(bi * tb, (bi + 1) * tb),) if flat else (bi, slice(None))

    @pl.when(n == 0)
    def _():
        s_ref[...] = s0_ref[...]
        cb_ref[:, 0:SUBLANE, :] = conv0_ref[...]

    rows = lax.broadcasted_iota(jnp.int32, (cc, 1), 0)
    live = rows < tb
    ii = lax.broadcasted_iota(jnp.int32, (cc, cc), 0)
    jj = lax.broadcasted_iota(jnp.int32, (cc, cc), 1)
    incl = ii >= jj
    strict = ii > jj
    tri = incl.astype(F32)
    cw = cw_ref[...]
    first = SUBLANE - (G_CONV - 1)
    dn_nt = (((1,), (1,)), ((), ()))
    dn_tn = (((0,), (0,)), ((), ()))
    eye = ii == jj

    gs = bpb
    for b0 in range(0, bpb, gs):
        qs, ks, kbs, decays, rhss, e_gcs, e_rests, e_ends = [], [], [], [], [], [], [], []
        idx = range(gs * G_HEADS)
        for bi in range(b0, b0 + gs):
            x = qkv_ref[seq_rows(bi)].astype(F32)
            cb_ref[bi, SUBLANE:SUBLANE + tb, :] = x
            if tb < cc:
                cb_ref[bi, SUBLANE + tb:SUBLANE + cc, :] = jnp.zeros((cc - tb, G_CONV_CH), F32)
            y = cb_ref[bi, SUBLANE:SUBLANE + cc, :] * cw[G_CONV - 1:G_CONV, :]
            for j in range(G_CONV - 2, -1, -1):
                y = y + cb_ref[bi, first + j:first + j + cc, :] * cw[j:j + 1, :]
            if nc > 1:
                cb_ref[bi, 0:SUBLANE, :] = x[tb - SUBLANE:tb, :]
            y = y * jax.nn.sigmoid(y)

            ba = ba_ref[seq_rows(bi)]
            if tb < cc:
                ba = jnp.concatenate([ba, jnp.zeros((cc - tb, LANE), F32)], axis=0)
            beta_t = jnp.where(live, jax.nn.sigmoid(ba), 0.0)
            xs = ba + dtb_ref[...]
            softplus = jnp.maximum(xs, 0.0) + jnp.log(1.0 + jnp.exp(-jnp.abs(xs)))
            g_t = jnp.where(live, -jnp.exp(alog_ref[...]) * softplus, 0.0)
            gc = jnp.dot(tri, g_t, precision=HIGHEST, preferred_element_type=F32)
            gc_t = jnp.concatenate([gc, jnp.zeros((LANE - cc, LANE), F32)], axis=0).T
            gc_last = gc[cc - 1:cc, :]
            e_gc = jnp.exp(gc)
            e_rest = jnp.exp(gc_last - gc)
            e_end = jnp.exp(gc_last)
            for h in range(G_HEADS):
                sl = slice(h * G_DK, (h + 1) * G_DK)
                gl = slice(G_HEADS + h, G_HEADS + h + 1)
                q = y[:, sl]
                k = y[:, G_QK + h * G_DK:G_QK + (h + 1) * G_DK]
                v = y[:, 2 * G_QK + h * G_DV:2 * G_QK + (h + 1) * G_DV]
                q = q * lax.rsqrt(jnp.sum(q * q, axis=-1, keepdims=True) + NORM_EPS) * (G_DK ** -0.5)
                k = k * lax.rsqrt(jnp.sum(k * k, axis=-1, keepdims=True) + NORM_EPS)
                if tb < cc:
                    q = jnp.where(live, q, 0.0)
                    k = jnp.where(live, k, 0.0)
                    v = jnp.where(live, v, 0.0)
                beta = beta_t[:, h:h + 1]
                kb = k * beta
                qs.append(q)
                ks.append(k)
                kbs.append(kb)
                decays.append(jnp.exp(jnp.where(
                    incl, gc[:, gl] - gc_t[G_HEADS + h:G_HEADS + h + 1, 0:cc], -jnp.inf)))
                rhss.append(jnp.concatenate([v * beta, kb * e_gc[:, gl]], axis=1))
                e_gcs.append(e_gc[:, gl])
                e_rests.append(e_rest[:, gl])
                e_ends.append(e_end[:, gl])

        kq = [_bdot(jnp.concatenate([kbs[i], qs[i]], axis=0), ks[i], dn_nt) for i in idx]
        a_low = [jnp.where(strict, kq[i][:cc] * decays[i], 0.0) for i in idx]
        qk = [kq[i][cc:] * decays[i] for i in idx]
        inv = [jnp.where(eye, 1.0, -a_low[i]) for i in idx]
        pw = a_low
        span = 2
        while span < cc:
            pw = [_bdot(pw[i], pw[i]) for i in idx]
            inv = [inv[i] + _bdot(pw[i], inv[i]) for i in idx]
            span *= 2
        rhss = [_bdot(inv[i], rhss[i]) for i in idx]
        s_old = [s_ref[b0 + i // G_HEADS, i % G_HEADS] for i in idx]
        ws = [_bdot(jnp.concatenate([rhss[i][:, G_DV:], qs[i] * e_gcs[i]], axis=0), s_old[i]) for i in idx]
        v_new = [rhss[i][:, :G_DV] - ws[i][:cc] for i in idx]
        for i in idx:
            s_ref[b0 + i // G_HEADS, i % G_HEADS] = (
                s_old[i] * e_ends[i] + _bdot(ks[i] * e_rests[i], v_new[i], dn_tn))
        o = [ws[i][cc:] + _bdot(qk[i], v_new[i]) for i in idx]
        for i in idx:
            bi, h = b0 + i // G_HEADS, i % G_HEADS
            sl = slice(h * G_DV, (h + 1) * G_DV)
            o_h = o[i] * lax.rsqrt(jnp.mean(o[i] * o[i], axis=-1, keepdims=True) + NORM_EPS) * nw_ref[...]
            zz = z_ref[seq_rows(bi) + (sl,)].astype(F32)
            o_ref[seq_rows(bi) + (sl,)] = o_h[:tb] * (zz * jax.nn.sigmoid(zz))

    @pl.when(n == nc - 1)
    def _():
        sout_ref[...] = s_ref[...]


def _gdn(u, ba, conv0, s0, conv_w, alog_t, dtb_t, norm_w, *, batch, seq, bpb):
    cc = G_CHUNK if seq >= G_CHUNK else SUBLANE
    tb = min(seq, cc)
    nc = seq // tb
    assert nc * tb == seq and (nc == 1 or tb == cc) and batch % bpb == 0
    flat = nc == 1
    if flat:
        assert (bpb * tb) % SUBLANE == 0
        uv, bav, o_rows = u, ba, batch * seq

        def rows(width, col):
            return pl.BlockSpec((bpb * tb, width), lambda b, n: (b, col))
    else:
        uv, bav, o_rows = u.reshape(batch, seq, UB_WIDTH), ba.reshape(batch, seq, LANE), batch

        def rows(width, col):
            return pl.BlockSpec((bpb, tb, width), lambda b, n: (b, n, col))
    o_shape = (o_rows, G_QK) if flat else (batch, seq, G_QK)
    conv0p = jnp.concatenate(
        [jnp.zeros((batch, SUBLANE - (G_CONV - 1), G_CONV_CH), F32), conv0.astype(F32)], axis=1)
    const2 = lambda b, n: (0, 0)
    o, s_new = pl.pallas_call(
        functools.partial(_gdn_kernel, tb=tb, cc=cc, nc=nc, bpb=bpb, flat=flat),
        grid=(batch // bpb, nc),
        in_specs=[
            rows(G_CONV_CH, QKVB_OFF // G_CONV_CH),
            rows(G_QK, Z_OFF // G_QK),
            rows(LANE, 0),
            pl.BlockSpec((G_CONV, G_CONV_CH), const2),
            pl.BlockSpec((1, LANE), const2),
            pl.BlockSpec((1, LANE), const2),
            pl.BlockSpec((1, G_DV), const2),
            pl.BlockSpec((bpb, SUBLANE, G_CONV_CH), lambda b, n: (b, 0, 0)),
            pl.BlockSpec((bpb, G_HEADS, G_DK, G_DV), lambda b, n: (b, 0, 0, 0)),
        ],
        out_specs=[
            rows(G_QK, 0),
            pl.BlockSpec((bpb, G_HEADS, G_DK, G_DV), lambda b, n: (b, 0, 0, 0)),
        ],
        out_shape=[
            jax.ShapeDtypeStruct(o_shape, F32),
            jax.ShapeDtypeStruct((batch, G_HEADS, G_DK, G_DV), F32),
        ],
        scratch_shapes=[
            pltpu.VMEM((bpb, SUBLANE + cc, G_CONV_CH), F32),
            pltpu.VMEM((bpb, G_HEADS, G_DK, G_DV), F32),
        ],
        compiler_params=_cparams(("parallel", "arbitrary")),
        name="gdn",
    )(uv, uv, bav, conv_w, alog_t, dtb_t, norm_w.reshape(1, G_DV), conv0p, s0)
    return o.reshape(batch * seq, G_QK), s_new


def _mix_kernel(x_ref, oa_ref, ob_ref, ga_ref, gb_ref, wa_ref, wb_ref, wo_ref, o_ref):
    pa = jnp.dot(oa_ref[...].astype(BF16), wa_ref[...], preferred_element_type=F32)
    pb = jnp.dot(ob_ref[...].astype(BF16), wb_ref[...], preferred_element_type=F32)
    merged = (jax.nn.sigmoid(ga_ref[...].astype(F32)) * pa
              + jax.nn.sigmoid(gb_ref[...].astype(F32)) * pb)
    o_ref[...] = x_ref[...] + jnp.dot(merged.astype(BF16), wo_ref[...], preferred_element_type=F32)


def _mix(x, oa, ob, u, wa, wb, wo, *, tm):
    n, d = x.shape
    const = lambda i: (0, 0)
    return pl.pallas_call(
        _mix_kernel,
        grid=(n // tm,),
        in_specs=[
            pl.BlockSpec((tm, d), lambda i: (i, 0)),
            pl.BlockSpec((tm, A_GW), lambda i: (i, 0)),
            pl.BlockSpec((tm, G_QK), lambda i: (i, 0)),
            pl.BlockSpec((tm, d), lambda i: (i, GATE_OFF // d)),
            pl.BlockSpec((tm, d), lambda i: (i, GATE_OFF // d + 1)),
            pl.BlockSpec((A_GW, d), const),
            pl.BlockSpec((G_QK, d), const),
            pl.BlockSpec((d, d), const),
        ],
        out_specs=pl.BlockSpec((tm, d), lambda i: (i, 0)),
        out_shape=jax.ShapeDtypeStruct((n, d), F32),
        compiler_params=_cparams(("parallel",)),
        name="mix",
    )(x, oa, ob, u, u, wa, wb, wo)


def _w_in_sources():
    aw = len(A_GROUPS) * A_GW
    src_q, src_k, src_v = 0, aw, 2 * aw
    src_qkvb = 3 * aw
    src_z = src_qkvb + G_CONV_CH
    src_ba = src_z + G_QK
    src_gate = src_ba + 2 * G_HEADS
    blocks = [src_qkvb + W_CB * t for t in range(G_CONV_CH // W_CB)]
    blocks += [src_z + W_CB * t for t in range(G_QK // W_CB)]
    blocks += [src_gate + W_CB * t for t in range(2 * D_MODEL // W_CB)]
    for g in range(len(A_GROUPS)):
        blocks += [src_k + g * A_GW, src_v + g * A_GW]
    blocks += [src_q + W_CB * t for t in range(aw // W_CB)]
    assert len(blocks) * W_CB == U_WIDTH and A_GW == W_CB
    assert all(b % SUBLANE == 0 for b in blocks)
    return blocks, src_ba


def _permute_cast_kernel(src_ref, w_ref, o_ref):
    del src_ref
    o_ref[...] = w_ref[...].astype(BF16)


def _permute_w_in(wt):
    d = wt.shape[1]
    blocks, _ = _w_in_sources()
    return pl.pallas_call(
        _permute_cast_kernel,
        grid_spec=pltpu.PrefetchScalarGridSpec(
            num_scalar_prefetch=1,
            grid=(len(blocks),),
            in_specs=[pl.BlockSpec((pl.Element(W_CB), pl.Element(d)),
                                   lambda j, src: (pl.multiple_of(src[j], SUBLANE), 0))],
            out_specs=pl.BlockSpec((W_CB, d), lambda j, src: (j, 0)),
        ),
        out_shape=jax.ShapeDtypeStruct((U_WIDTH, d), BF16),
        compiler_params=_cparams(("parallel",)),
        name="permute_w_in",
    )(jnp.asarray(blocks, jnp.int32), wt)


def _prep_weights(w_in, gdn_a_log, gdn_dt_bias):
    wt = w_in.T
    w_main = _permute_w_in(wt)
    src_ba = _w_in_sources()[1]
    n_ba = 2 * G_HEADS
    ba = wt[src_ba:src_ba + n_ba, :]
    ba_hi = ba.astype(BF16)
    ba_lo = (ba - ba_hi.astype(F32)).astype(BF16)
    w_ba = jnp.concatenate([ba_hi, ba_lo, jnp.zeros((MXU_COLS - 2 * n_ba, wt.shape[1]), BF16)], axis=0)
    pad = (G_HEADS, LANE - n_ba)
    alog_t = jnp.pad(gdn_a_log.astype(F32), pad).reshape(1, LANE)
    dtb_t = jnp.pad(gdn_dt_bias.astype(F32), pad).reshape(1, LANE)
    return w_main, w_ba, alog_t, dtb_t


def _layer(x, batch, seq, conv0, s0, attn_fn, p, *, tm, gdn_bpb, b_dtype):
    tm = min(tm, x.shape[0])
    gdn_bpb = min(gdn_bpb, batch)
    tail = b_dtype != F32
    x1 = _ffn(x, p["norm_ffn1"], p["w_ffn1_gu"], p["w_ffn1_down"], tm=tm)
    ub, ua, ba, *rest = _inproj(x1, p["norm_mix"], p["w_main"], p["w_ba"],
                                tm=tm, b_dtype=b_dtype, tail=tail)
    oa, rows = attn_fn(ua)
    ob, s_new = _gdn(ub, ba, conv0, s0, p["conv_w"], p["alog_t"], p["dtb_t"], p["gdn_norm"],
                     batch=batch, seq=seq, bpb=gdn_bpb)
    x2 = _mix(x1, oa.reshape(batch * seq, A_GW), ob.reshape(batch * seq, G_QK), ub,
              p["w_proj_a"], p["w_proj_b"], p["w_out"], tm=min(tm, 1024))
    y = _ffn(x2, p["norm_ffn2"], p["w_ffn2_gu"], p["w_ffn2_down"], p["norm_out"], tm=tm)
    if tail:
        assert seq % tm == 0 and seq >= SUBLANE
        last = rest[0].reshape(batch, seq // tm, SUBLANE, G_CONV_CH)[:, -1]
        conv_new = last[:, SUBLANE - (G_CONV - 1):, :][None]
    else:
        qkvb = ub[:, QKVB_OFF:QKVB_OFF + G_CONV_CH].reshape(batch, seq, G_CONV_CH)
        conv_new = qkvb[:, seq - (G_CONV - 1):, :][None]
    return y.reshape(batch, seq, D_MODEL), rows, conv_new, s_new[None]


def kernel(x_prompt, x_sample, cache_kv_w128, cache_kv_w512, cache_kv_w2048, state_conv, state_ssm,
           norm_ffn1, w_ffn1_gu, w_ffn1_down, norm_mix, w_in, conv_w, gdn_a_log, gdn_dt_bias, gdn_norm,
           w_proj_a, w_proj_b, w_out, norm_ffn2, w_ffn2_gu, w_ffn2_down, norm_out):
    assert w_in.shape[0] == 1, "single layer"
    n_heads = len(A_GROUPS) * A_HPG
    slopes = jnp.exp2(-8.0 * jnp.arange(1, n_heads + 1, dtype=F32) / n_heads)
    w_main, w_ba, alog_t, dtb_t = _prep_weights(w_in[0], gdn_a_log[0], gdn_dt_bias[0])
    p = dict(
        norm_ffn1=norm_ffn1[0], w_ffn1_gu=w_ffn1_gu[0], w_ffn1_down=w_ffn1_down[0],
        norm_mix=norm_mix[0], w_main=w_main, w_ba=w_ba, conv_w=conv_w[0],
        alog_t=alog_t, dtb_t=dtb_t, gdn_norm=gdn_norm[0],
        w_proj_a=w_proj_a[0].astype(BF16), w_proj_b=w_proj_b[0].astype(BF16), w_out=w_out[0].astype(BF16),
        norm_ffn2=norm_ffn2[0], w_ffn2_gu=w_ffn2_gu[0], w_ffn2_down=w_ffn2_down[0],
        norm_out=norm_out,
    )
    bp, tp, d = x_prompt.shape
    bs, ts, _ = x_sample.shape

    conv0_p = jnp.zeros((bp, G_CONV - 1, G_CONV_CH), F32)
    ssm0_p = jnp.zeros((bp, G_HEADS, G_DK, G_DV), F32)
    yp, rows_p, conv_p, ssm_p = _layer(
        x_prompt.reshape(bp * tp, d), bp, tp, conv0_p, ssm0_p,
        functools.partial(_attn_prompt, slopes, batch=bp, seq=tp), p, tm=2048, gdn_bpb=4, b_dtype=BF16)

    caches = (cache_kv_w128[0], cache_kv_w512[0], cache_kv_w2048[0])
    ys, rows_s, conv_s, ssm_s = _layer(
        x_sample.reshape(bs * ts, d), bs, ts, state_conv[0], state_ssm[0],
        lambda u: _attn_sample(slopes, u, caches, batch=bs, dec_seq=ts), p, tm=512, gdn_bpb=8, b_dtype=F32)

    return (yp, ys, rows_p[0], rows_p[1], rows_p[2], conv_p, ssm_p,
            rows_s[0], rows_s[1], rows_s[2], conv_s, ssm_s)
```

```python
import functools

import jax
import jax.numpy as jnp
from jax import lax
from jax.experimental import pallas as pl
from jax.experimental.pallas import tpu as pltpu

F32 = jnp.float32
BF16 = jnp.bfloat16
HIGHEST = lax.Precision.HIGHEST

D_MODEL = 1024
D_FF = 2816
NORM_EPS = 1e-6
A_GROUPS = ((128, 1), (512, 4), (2048, 16))
A_HPG = 4
A_E = 128
A_BLOCK = 128
A_GW = A_HPG * A_E
A_ST = 2048
G_HEADS = 8
G_DK = 128
G_DV = 128
G_QK = G_HEADS * G_DK
G_CONV_CH = 3 * G_QK
G_CONV = 4
G_CHUNK = 64

QKVB_OFF = 0
Z_OFF = 3072
GATE_OFF = 4096
UB_WIDTH = 6144
KV_OFF = 0
QA_OFF = 3072
UA_WIDTH = 4608
U_WIDTH = UB_WIDTH + UA_WIDTH
W_CB = 512

VMEM_LIMIT = 56 * 1024 * 1024
LANE = 128
SUBLANE = 8
MXU_COLS = 256
LOG2E = 1.4426950408889634


def _cparams(sem):
    return pltpu.CompilerParams(dimension_semantics=sem, vmem_limit_bytes=VMEM_LIMIT)


def _rms_rows(x, w):
    return x * lax.rsqrt(jnp.mean(x * x, axis=-1, keepdims=True) + NORM_EPS) * w


def _ffn_kernel(x_ref, nw_ref, wg_ref, wu_ref, wd_ref, *rest, n_ff, final_norm):
    if final_norm:
        onw_ref, o_ref, hb_ref = rest
    else:
        o_ref, hb_ref = rest
    j = pl.program_id(1)

    @pl.when(j == 0)
    def _():
        hb_ref[...] = _rms_rows(x_ref[...], nw_ref[...]).astype(BF16)
        o_ref[...] = jnp.zeros_like(o_ref)

    hb = hb_ref[...]
    g = jnp.dot(hb, wg_ref[...].astype(BF16), preferred_element_type=F32)
    u = jnp.dot(hb, wu_ref[...].astype(BF16), preferred_element_type=F32)
    a = (g * jax.nn.sigmoid(g) * u).astype(BF16)
    o_ref[...] += jnp.dot(a, wd_ref[...].astype(BF16), preferred_element_type=F32)

    @pl.when(j == n_ff - 1)
    def _():
        y = x_ref[...] + 0.5 * o_ref[...]
        if final_norm:
            y = _rms_rows(y, onw_ref[...])
        o_ref[...] = y


def _ffn(x, norm_w, w_gu, w_down, out_norm_w=None, *, tm, tf):
    n, d = x.shape
    n_ff = D_FF // tf
    final_norm = out_norm_w is not None
    in_specs = [
        pl.BlockSpec((tm, d), lambda i, j: (i, 0)),
        pl.BlockSpec((1, d), lambda i, j: (0, 0)),
        pl.BlockSpec((d, tf), lambda i, j: (0, j)),
        pl.BlockSpec((d, tf), lambda i, j: (0, j + n_ff)),
        pl.BlockSpec((tf, d), lambda i, j: (j, 0)),
    ]
    args = [x, norm_w.reshape(1, d), w_gu, w_gu, w_down]
    if final_norm:
        in_specs.append(pl.BlockSpec((1, d), lambda i, j: (0, 0)))
        args.append(out_norm_w.reshape(1, d))
    return pl.pallas_call(
        functools.partial(_ffn_kernel, n_ff=n_ff, final_norm=final_norm),
        grid=(n // tm, n_ff),
        in_specs=in_specs,
        out_specs=pl.BlockSpec((tm, d), lambda i, j: (i, 0)),
        out_shape=jax.ShapeDtypeStruct((n, d), F32),
        scratch_shapes=[pltpu.VMEM((tm, d), BF16)],
        compiler_params=_cparams(("parallel", "arbitrary")),
        name="ffn_final" if final_norm else "ffn",
    )(*args)


_DN_NT = (((1,), (1,)), ((), ()))


def _inproj_kernel(x_ref, nw_ref, w_ref, wba_ref, ub_ref, ua_ref, ba_ref, *rest, nb, nq, tail):
    if tail:
        tail_ref, hb_ref = rest
    else:
        (hb_ref,) = rest
    j = pl.program_id(1)

    @pl.when(j == 0)
    def _():
        h = _rms_rows(x_ref[...], nw_ref[...])
        hb = h.astype(BF16)
        hl = (h - hb.astype(F32)).astype(BF16)
        hb_ref[...] = hb
        n_ba = 2 * G_HEADS
        r_hi = lax.dot_general(hb, wba_ref[...], _DN_NT, preferred_element_type=F32)[:, :LANE]
        r_lo = lax.dot_general(hl, wba_ref[...], _DN_NT, preferred_element_type=F32)[:, :LANE]
        lane = lax.broadcasted_iota(jnp.int32, r_hi.shape, 1)
        ba_ref[...] = jnp.where(lane < n_ba, r_hi + pltpu.roll(r_hi, LANE - n_ba, 1) + r_lo, 0.0)

    def project(rows):
        return lax.dot_general(hb_ref[rows, :], w_ref[...], _DN_NT, preferred_element_type=F32)

    @pl.when(j < nb)
    def _():
        ub_ref[...] = project(slice(None)).astype(ub_ref.dtype)

    @pl.when(j >= nb)
    def _():
        ua_ref[...] = project(slice(None))

    if tail:
        @pl.when(j < nq)
        def _():
            tm = hb_ref.shape[0]
            tail_ref[...] = project(slice(tm - 2 * SUBLANE, tm))[SUBLANE:, :]


def _inproj(x, norm_w, w_main, w_ba, *, tm, tn, b_dtype, tail):
    n, d = x.shape
    nb, nq = UB_WIDTH // tn, G_CONV_CH // tn
    assert nb * tn == UB_WIDTH and nq * tn == G_CONV_CH and U_WIDTH % tn == 0 and QKVB_OFF == 0
    out_specs = [
        pl.BlockSpec((tm, tn), lambda i, j: (i, jnp.minimum(j, nb - 1))),
        pl.BlockSpec((tm, tn), lambda i, j: (i, jnp.maximum(j - nb, 0))),
        pl.BlockSpec((tm, LANE), lambda i, j: (i, 0)),
    ]
    out_shape = [
        jax.ShapeDtypeStruct((n, UB_WIDTH), b_dtype),
        jax.ShapeDtypeStruct((n, UA_WIDTH), F32),
        jax.ShapeDtypeStruct((n, LANE), F32),
    ]
    if tail:
        out_specs.append(pl.BlockSpec((SUBLANE, tn), lambda i, j: (i, jnp.minimum(j, nq - 1))))
        out_shape.append(jax.ShapeDtypeStruct((n // tm * SUBLANE, G_CONV_CH), F32))
    return pl.pallas_call(
        functools.partial(_inproj_kernel, nb=nb, nq=nq, tail=tail),
        grid=(n // tm, U_WIDTH // tn),
        in_specs=[
            pl.BlockSpec((tm, d), lambda i, j: (i, 0)),
            pl.BlockSpec((1, d), lambda i, j: (0, 0)),
            pl.BlockSpec((tn, d), lambda i, j: (j, 0)),
            pl.BlockSpec((MXU_COLS, d), lambda i, j: (0, 0)),
        ],
        out_specs=out_specs,
        out_shape=out_shape,
        scratch_shapes=[pltpu.VMEM((tm, d), BF16)],
        compiler_params=_cparams(("parallel", "arbitrary")),
        name="inproj",
    )(x, norm_w.reshape(1, d), w_main, w_ba)


def _rows(start, size, stride):
    return pl.ds(start, size) if stride == 1 else pl.ds(start, size, stride=stride)


def _attn_prompt_kernel(slopes_ref, *refs, n_st, keeps):
    n_g = len(A_GROUPS)
    ins, o_ref = refs[:5 * n_g], refs[5 * n_g]
    kv_refs, scr = refs[5 * n_g + 1:6 * n_g + 1], refs[6 * n_g + 1:]
    st = pl.program_id(1)
    hh = pl.program_id(2)

    @pl.when(st == n_st - 1)
    def _():
        for g in range(n_g):
            keep = keeps[g]
            kc_ref, vc_ref = ins[5 * g + 1], ins[5 * g + 2]
            kv_refs[g][pl.ds(hh, keep, stride=SUBLANE), :] = kc_ref[A_ST - keep:, :]
            kv_refs[g][pl.ds(A_HPG + hh, keep, stride=SUBLANE), :] = vc_ref[A_ST - keep:, :]

    scale = A_E ** -0.5
    qi = lax.broadcasted_iota(jnp.int32, (A_BLOCK, 2 * A_BLOCK), 0)
    kj = lax.broadcasted_iota(jnp.int32, (A_BLOCK, 2 * A_BLOCK), 1)
    delta = A_BLOCK + qi - kj
    band = jnp.logical_and(delta >= 0, delta <= A_BLOCK)
    band_first = jnp.logical_and(band, jnp.logical_or(kj >= A_BLOCK, st > 0))
    dn_nt = (((1,), (1,)), ((), ()))
    for g, (win, dil) in enumerate(A_GROUPS):
        q_ref, kc_ref, vc_ref, kp_ref, vp_ref = ins[5 * g:5 * g + 5]
        og_ref, lg_ref = scr[2 * g:2 * g + 2]
        slope = slopes_ref[g * A_HPG + hh]
        bias = -slope * (dil * delta).astype(F32) * LOG2E
        bias_mid = jnp.where(band, bias, -jnp.inf)
        bias_first = jnp.where(band_first, bias, -jnp.inf)

        def two_blocks(prev_ref, cur_ref, r, j):
            if j == 0:
                return jnp.concatenate([prev_ref[_rows(r, A_BLOCK, dil), :],
                                        cur_ref[_rows(r, A_BLOCK, dil), :]], axis=0)
            return cur_ref[_rows(r + (j - 1) * A_BLOCK * dil, 2 * A_BLOCK, dil), :]

        for r in range(dil):
            for j in range(A_ST // (A_BLOCK * dil)):
                row0 = r + j * A_BLOCK * dil
                q = (q_ref[_rows(row0, A_BLOCK, dil), :] * (scale * LOG2E)).astype(BF16)
                k2 = two_blocks(kp_ref, kc_ref, r, j).astype(BF16)
                v2 = two_blocks(vp_ref, vc_ref, r, j).astype(BF16)
                s = lax.dot_general(q, k2, dn_nt, preferred_element_type=F32)
                s = s + (bias_first if j == 0 else bias_mid)
                m = jnp.max(s, axis=-1, keepdims=True)
                p = jnp.exp2(s - m)
                den = jnp.sum(p, axis=-1, keepdims=True)
                acc = jnp.dot(p.astype(BF16), v2, preferred_element_type=F32)
                og_ref[_rows(row0, A_BLOCK, dil), :] = acc / den
                lg_ref[_rows(row0, A_BLOCK, dil), :] = jnp.broadcast_to(m + jnp.log2(den), (A_BLOCK, A_E))
    chunk = 2 * A_BLOCK
    for c in range(A_ST // chunk):
        rs = slice(c * chunk, (c + 1) * chunk)
        ls = [scr[2 * g + 1][rs, :] for g in range(n_g)]
        mx = functools.reduce(jnp.maximum, ls)
        ws = [jnp.exp2(l - mx) for l in ls]
        num = sum(w * scr[2 * g][rs, :] for g, w in enumerate(ws))
        o_ref[rs, :] = num / sum(ws)


def _attn_prompt(slopes, u, *, batch, seq):
    assert seq % A_ST == 0
    u3 = u.reshape(batch, seq, UA_WIDTH)
    in_specs = [pl.BlockSpec(memory_space=pltpu.SMEM)]
    args = [slopes]
    scratch = []
    keeps = tuple(min(win, seq) for win, _ in A_GROUPS)
    assert max(keeps) <= A_ST
    for g, (win, dil) in enumerate(A_GROUPS):
        assert win == A_BLOCK * dil and A_ST % win == 0
        qcb = (QA_OFF + g * A_GW) // A_E
        kcb = (KV_OFF + g * 2 * A_GW) // A_E
        vcb = kcb + A_HPG
        per = A_ST // win

        def cur(cb):
            return pl.BlockSpec((None, A_ST, A_E), lambda b, st, hh, cb=cb: (b, st, cb + hh))

        def prv(cb, win=win, per=per):
            return pl.BlockSpec((None, win, A_E),
                                lambda b, st, hh, cb=cb: (b, jnp.maximum(st * per - 1, 0), cb + hh))

        in_specs += [cur(qcb), cur(kcb), cur(vcb), prv(kcb), prv(vcb)]
        args += [u3] * 5
        scratch += [pltpu.VMEM((A_ST, A_E), F32), pltpu.VMEM((A_ST, A_E), F32)]
    n_st = seq // A_ST
    out_specs = [pl.BlockSpec((None, A_ST, A_E), lambda b, st, hh: (b, st, hh))]
    out_shape = [jax.ShapeDtypeStruct((batch, seq, A_GW), F32)]
    for keep in keeps:
        out_specs.append(pl.BlockSpec((None, keep * SUBLANE, A_E), lambda b, st, hh: (b, 0, 0),
                                      pipeline_mode=pl.Buffered(1)))
        out_shape.append(jax.ShapeDtypeStruct((batch, keep * SUBLANE, A_E), F32))
    outs = pl.pallas_call(
        functools.partial(_attn_prompt_kernel, n_st=n_st, keeps=keeps),
        grid=(batch, n_st, A_HPG),
        in_specs=in_specs,
        out_specs=out_specs,
        out_shape=out_shape,
        scratch_shapes=scratch,
        compiler_params=_cparams(("parallel", "arbitrary", "arbitrary")),
        name="attn_prompt",
    )(*args)
    rows = [kv.reshape(1, batch, keep, 2, A_HPG, A_E) for kv, keep in zip(outs[1:], keeps)]
    return outs[0], rows


def _pick_rows(rows):
    sub = lax.broadcasted_iota(jnp.int32, (SUBLANE, A_E), 0)
    out = jnp.zeros((SUBLANE, A_E), F32)
    for i, r in enumerate(rows):
        out = jnp.where(sub == i, jnp.broadcast_to(r, (SUBLANE, A_E)), out)
    return out


def _attn_sample_kernel(slopes_ref, q_ref, kv_ref, c0_ref, c1_ref, c2_ref, o_ref, *kvo_refs, dec_seq, nseq):
    scale = A_E ** -0.5
    caches = (c0_ref, c1_ref, c2_ref)
    for g, kvo_ref in enumerate(kvo_refs):
        for t in range(nseq * dec_seq):
            kvo_ref[t * SUBLANE:(t + 1) * SUBLANE, :] = _pick_rows(
                [kv_ref[t:t + 1, g * 2 * A_GW + c * A_E:g * 2 * A_GW + (c + 1) * A_E] for c in range(2 * A_HPG)])
    n_keys = A_BLOCK * SUBLANE
    sub = lax.broadcasted_iota(jnp.int32, (SUBLANE, n_keys), 0)
    lane = lax.broadcasted_iota(jnp.int32, (SUBLANE, n_keys), 1)
    tile_row = lane & (SUBLANE - 1)
    m_idx = lane >> 3
    own_k = tile_row == sub
    sub1 = lax.broadcasted_iota(jnp.int32, (SUBLANE, 1), 0)
    dn_nt = (((1,), (1,)), ((), ()))
    slope8s = []
    for g in range(len(A_GROUPS)):
        slope8 = jnp.zeros((SUBLANE, 1), F32)
        for h in range(A_HPG):
            slope8 = jnp.where(sub1 == h, slopes_ref[g * A_HPG + h], slope8)
        slope8s.append(slope8)

    def keys(bb, s, g):
        res = 0 if A_GROUPS[g][1] == 1 else s
        return caches[g][bb, :, res, :, :].reshape(n_keys, A_E).astype(BF16)

    units = [(bb, s, g) for bb in range(nseq) for s in range(dec_seq) for g in range(len(A_GROUPS))]
    q8s = [_pick_rows([q_ref[bb * dec_seq + s:bb * dec_seq + s + 1, g * A_GW + h * A_E:g * A_GW + (h + 1) * A_E]
                       for h in range(A_HPG)]) for bb, s, g in units]
    scs = [lax.dot_general(q8.astype(BF16), keys(*u), dn_nt, preferred_element_type=F32) * scale
           for q8, u in zip(q8s, units)]
    ms, dens, pvs, news = [], [], [], []
    for (bb, s, g), q8, sc in zip(units, q8s, scs):
        win, dil = A_GROUPS[g]
        row = bb * dec_seq + s
        slope8 = slope8s[g]
        if dil == 1:
            dist = (win + s - m_idx).astype(F32)
            valid = jnp.logical_and(own_k, m_idx >= s)
            new_rows = [(bb * dec_seq + t, float(s - t)) for t in range(s + 1)]
        else:
            dist = (dil * (win // dil - m_idx)).astype(F32)
            valid = own_k
            new_rows = [(row, 0.0)]
        sc = jnp.where(valid, sc - slope8 * dist, -jnp.inf)
        m = jnp.max(sc, axis=-1, keepdims=True)
        koff = g * 2 * A_GW
        new = []
        for nrow, ndist in new_rows:
            k8 = _pick_rows([kv_ref[nrow:nrow + 1, koff + h * A_E:koff + (h + 1) * A_E]
                             for h in range(A_HPG)])
            v8 = _pick_rows([kv_ref[nrow:nrow + 1, koff + A_GW + h * A_E:koff + A_GW + (h + 1) * A_E]
                             for h in range(A_HPG)])
            s_n = jnp.sum(k8 * q8, axis=-1, keepdims=True) * scale - slope8 * ndist
            m = jnp.maximum(m, s_n)
            new.append((s_n, v8))
        p = jnp.exp(sc - m)
        ms.append(m)
        dens.append(jnp.sum(p, axis=-1, keepdims=True))
        pvs.append(pltpu.roll(p, A_HPG, 1).astype(BF16))
        news.append(new)
    accs = [jnp.dot(pv, keys(*u), preferred_element_type=F32) for pv, u in zip(pvs, units)]

    out_tiles = [jnp.zeros((nseq * dec_seq, A_E), F32) for _ in range(A_HPG)]
    out_sub = lax.broadcasted_iota(jnp.int32, (nseq * dec_seq, A_E), 0)
    n_g = len(A_GROUPS)
    for ui in range(0, len(units), n_g):
        bb, s, _ = units[ui]
        outs, lses = [], []
        for m, den, acc, new in zip(ms[ui:ui + n_g], dens[ui:ui + n_g], accs[ui:ui + n_g], news[ui:ui + n_g]):
            for s_n, v8 in new:
                p_n = jnp.exp(s_n - m)
                den = den + p_n
                acc = acc + p_n * v8
            outs.append(acc / den)
            lses.append(m + jnp.log(den))
        mx = functools.reduce(jnp.maximum, lses)
        ws = [jnp.exp(l - mx) for l in lses]
        o8 = sum(w * o for w, o in zip(ws, outs)) / sum(ws)
        for h in range(A_HPG):
            out_tiles[h] = jnp.where(out_sub == bb * dec_seq + s,
                                     jnp.broadcast_to(o8[h:h + 1, :], (nseq * dec_seq, A_E)), out_tiles[h])
    for h in range(A_HPG):
        o_ref[:, h * A_E:(h + 1) * A_E] = out_tiles[h]


def _attn_sample(slopes, u, caches, *, batch, dec_seq):
    nseq = SUBLANE // dec_seq
    assert nseq * dec_seq == SUBLANE and batch % nseq == 0
    views, specs = [], []
    for (win, dil), cache in zip(A_GROUPS, caches):
        wb = cache.shape[1]
        assert wb == win and wb // dil == A_BLOCK and (dil == 1 or dec_seq <= dil)
        views.append(cache.reshape(batch, wb // dil, dil, SUBLANE, A_E))
        res = 1 if dil == 1 else dec_seq
        specs.append(pl.BlockSpec((nseq, A_BLOCK, res, SUBLANE, A_E), lambda i: (i, 0, 0, 0, 0)))
    n_q = len(A_GROUPS) * A_GW
    rows = nseq * dec_seq
    outs = pl.pallas_call(
        functools.partial(_attn_sample_kernel, dec_seq=dec_seq, nseq=nseq),
        grid=(batch // nseq,),
        in_specs=[
            pl.BlockSpec(memory_space=pltpu.SMEM),
            pl.BlockSpec((rows, n_q), lambda i: (i, QA_OFF // n_q)),
            pl.BlockSpec((rows, 2 * n_q), lambda i: (i, KV_OFF // (2 * n_q))),
        ] + specs,
        out_specs=[pl.BlockSpec((rows, A_GW), lambda i: (i, 0))]
        + [pl.BlockSpec((rows * SUBLANE, A_E), lambda i: (i, 0))] * len(A_GROUPS),
        out_shape=[jax.ShapeDtypeStruct((batch * dec_seq, A_GW), F32)]
        + [jax.ShapeDtypeStruct((batch * dec_seq * SUBLANE, A_E), F32)] * len(A_GROUPS),
        compiler_params=_cparams(("parallel",)),
        name="attn_sample",
    )(slopes, u, u, *views)
    return outs[0], [kv.reshape(1, batch, dec_seq, 2, A_HPG, A_E) for kv in outs[1:]]


def _bdot(a, b, dn=(((1,), (0,)), ((), ()))):
    return lax.dot_general(a.astype(BF16), b.astype(BF16), dn, preferred_element_type=F32)


def _gdn_kernel(qkv_ref, z_ref, ba_ref, cw_ref, alog_ref, dtb_ref, nw_ref, conv0_ref, s0_ref,
                o_ref, sout_ref, cb_ref, s_ref, *, tb, cc, nc, bpb, flat):
    n = pl.program_id(1)

    def seq_rows(bi):
        return (slice(bi * tb, (bi + 1) * tb),) if flat else (bi, slice(None))

    @pl.when(n == 0)
    def _():
        s_ref[...] = s0_ref[...]
        cb_ref[:, 0:SUBLANE, :] = conv0_ref[...]

    rows = lax.broadcasted_iota(jnp.int32, (cc, 1), 0)
    live = rows < tb
    ii = lax.broadcasted_iota(jnp.int32, (cc, cc), 0)
    jj = lax.broadcasted_iota(jnp.int32, (cc, cc), 1)
    incl = ii >= jj
    strict = ii > jj
    tri = incl.astype(F32)
    cw = cw_ref[...]
    first = SUBLANE - (G_CONV - 1)
    dn_nt = (((1,), (1,)), ((), ()))
    dn_tn = (((0,), (0,)), ((), ()))
    eye = ii == jj

    gs = bpb
    for b0 in range(0, bpb, gs):
        qs, ks, kbs, decays, rhss, e_gcs, e_rests, e_ends = [], [], [], [], [], [], [], []
        idx = range(gs * G_HEADS)
        for bi in range(b0, b0 + gs):
            x = qkv_ref[seq_rows(bi)].astype(F32)
            cb_ref[bi, SUBLANE:SUBLANE + tb, :] = x
            if tb < cc:
                cb_ref[bi, SUBLANE + tb:SUBLANE + cc, :] = jnp.zeros((cc - tb, G_CONV_CH), F32)
            y = cb_ref[bi, SUBLANE:SUBLANE + cc, :] * cw[G_CONV - 1:G_CONV, :]
            for j in range(G_CONV - 2, -1, -1):
                y = y + cb_ref[bi, first + j:first + j + cc, :] * cw[j:j + 1, :]
            if nc > 1:
                cb_ref[bi, 0:SUBLANE, :] = x[tb - SUBLANE:tb, :]
            y = y * jax.nn.sigmoid(y)

            ba = ba_ref[seq_rows(bi)]
            if tb < cc:
                ba = jnp.concatenate([ba, jnp.zeros((cc - tb, LANE), F32)], axis=0)
            beta_t = jnp.where(live, jax.nn.sigmoid(ba), 0.0)
            xs = ba + dtb_ref[...]
            softplus = jnp.maximum(xs, 0.0) + jnp.log(1.0 + jnp.exp(-jnp.abs(xs)))
            g_t = jnp.where(live, -jnp.exp(alog_ref[...]) * softplus, 0.0)
            gc = jnp.dot(tri, g_t, precision=HIGHEST, preferred_element_type=F32)
            gc_t = jnp.concatenate([gc, jnp.zeros((LANE - cc, LANE), F32)], axis=0).T
            gc_last = gc[cc - 1:cc, :]
            e_gc = jnp.exp(gc)
            e_rest = jnp.exp(gc_last - gc)
            e_end = jnp.exp(gc_last)
            for h in range(G_HEADS):
                sl = slice(h * G_DK, (h + 1) * G_DK)
                gl = slice(G_HEADS + h, G_HEADS + h + 1)
                q = y[:, sl]
                k = y[:, G_QK + h * G_DK:G_QK + (h + 1) * G_DK]
                v = y[:, 2 * G_QK + h * G_DV:2 * G_QK + (h + 1) * G_DV]
                q = q * (lax.rsqrt(jnp.sum(q * q, axis=-1, keepdims=True) + NORM_EPS) * (G_DK ** -0.5))
                k = k * lax.rsqrt(jnp.sum(k * k, axis=-1, keepdims=True) + NORM_EPS)
                if tb < cc:
                    q = jnp.where(live, q, 0.0)
                    k = jnp.where(live, k, 0.0)
                    v = jnp.where(live, v, 0.0)
                beta = beta_t[:, h:h + 1]
                kb = k * beta
                qs.append(q)
                ks.append(k)
                kbs.append(kb)
                decays.append(jnp.exp(jnp.where(
                    incl, gc[:, gl] - gc_t[G_HEADS + h:G_HEADS + h + 1, 0:cc], -jnp.inf)))
                rhss.append(jnp.concatenate([v * beta, kb * e_gc[:, gl]], axis=1))
                e_gcs.append(e_gc[:, gl])
                e_rests.append(e_rest[:, gl])
                e_ends.append(e_end[:, gl])

        kq = [_bdot(jnp.concatenate([kbs[i], qs[i]], axis=0), ks[i], dn_nt) for i in idx]
        a_low = [jnp.where(strict, kq[i][:cc] * decays[i], 0.0) for i in idx]
        qk = [kq[i][cc:] * decays[i] for i in idx]
        inv = [jnp.where(eye, 1.0, -a_low[i]) for i in idx]
        pw = a_low
        span = 2
        while span < cc:
            pw = [_bdot(pw[i], pw[i]) for i in idx]
            inv = [inv[i] + _bdot(pw[i], inv[i]) for i in idx]
            span *= 2
        rhss = [_bdot(inv[i], rhss[i]) for i in idx]
        s_old = [s_ref[b0 + i // G_HEADS, i % G_HEADS] for i in idx]
        ws = [_bdot(jnp.concatenate([rhss[i][:, G_DV:], qs[i] * e_gcs[i]], axis=0), s_old[i]) for i in idx]
        v_new = [rhss[i][:, :G_DV] - ws[i][:cc] for i in idx]
        for i in idx:
            s_ref[b0 + i // G_HEADS, i % G_HEADS] = (
                s_old[i] * e_ends[i] + _bdot(ks[i] * e_rests[i], v_new[i], dn_tn))
        o = [ws[i][cc:] + _bdot(qk[i], v_new[i]) for i in idx]
        for i in idx:
            bi, h = b0 + i // G_HEADS, i % G_HEADS
            sl = slice(h * G_DV, (h + 1) * G_DV)
            o_h = o[i] * lax.rsqrt(jnp.mean(o[i] * o[i], axis=-1, keepdims=True) + NORM_EPS) * nw_ref[...]
            zz = z_ref[seq_rows(bi) + (sl,)].astype(F32)
            o_ref[seq_rows(bi) + (sl,)] = o_h[:tb] * (zz * jax.nn.sigmoid(zz))

    @pl.when(n == nc - 1)
    def _():
        sout_ref[...] = s_ref[...]


def _gdn(u, ba, conv0, s0, conv_w, alog_t, dtb_t, norm_w, *, batch, seq, bpb):
    cc = G_CHUNK if seq >= G_CHUNK else SUBLANE
    tb = min(seq, cc)
    nc = seq // tb
    assert nc * tb == seq and (nc == 1 or tb == cc) and batch % bpb == 0
    flat = nc == 1
    if flat:
        assert (bpb * tb) % SUBLANE == 0
        uv, bav, o_rows = u, ba, batch * seq

        def rows(width, col):
            return pl.BlockSpec((bpb * tb, width), lambda b, n: (b, col))
    else:
        uv, bav, o_rows = u.reshape(batch, seq, UB_WIDTH), ba.reshape(batch, seq, LANE), batch

        def rows(width, col):
            return pl.BlockSpec((bpb, tb, width), lambda b, n: (b, n, col))
    o_shape = (o_rows, G_QK) if flat else (batch, seq, G_QK)
    conv0p = jnp.concatenate(
        [jnp.zeros((batch, SUBLANE - (G_CONV - 1), G_CONV_CH), F32), conv0.astype(F32)], axis=1)
    const2 = lambda b, n: (0, 0)
    o, s_new = pl.pallas_call(
        functools.partial(_gdn_kernel, tb=tb, cc=cc, nc=nc, bpb=bpb, flat=flat),
        grid=(batch // bpb, nc),
        in_specs=[
            rows(G_CONV_CH, QKVB_OFF // G_CONV_CH),
            rows(G_QK, Z_OFF // G_QK),
            rows(LANE, 0),
            pl.BlockSpec((G_CONV, G_CONV_CH), const2),
            pl.BlockSpec((1, LANE), const2),
            pl.BlockSpec((1, LANE), const2),
            pl.BlockSpec((1, G_DV), const2),
            pl.BlockSpec((bpb, SUBLANE, G_CONV_CH), lambda b, n: (b, 0, 0)),
            pl.BlockSpec((bpb, G_HEADS, G_DK, G_DV), lambda b, n: (b, 0, 0, 0)),
        ],
        out_specs=[
            rows(G_QK, 0),
            pl.BlockSpec((bpb, G_HEADS, G_DK, G_DV), lambda b, n: (b, 0, 0, 0)),
        ],
        out_shape=[
            jax.ShapeDtypeStruct(o_shape, F32),
            jax.ShapeDtypeStruct((batch, G_HEADS, G_DK, G_DV), F32),
        ],
        scratch_shapes=[
            pltpu.VMEM((bpb, SUBLANE + cc, G_CONV_CH), F32),
            pltpu.VMEM((bpb, G_HEADS, G_DK, G_DV), F32),
        ],
        compiler_params=_cparams(("parallel", "arbitrary")),
        name="gdn",
    )(uv, uv, bav, conv_w, alog_t, dtb_t, norm_w.reshape(1, G_DV), conv0p, s0)
    return o.reshape(batch * seq, G_QK), s_new


def _mix_kernel(x_ref, oa_ref, ob_ref, ga_ref, gb_ref, wa_ref, wb_ref, wo_ref, o_ref):
    pa = jnp.dot(oa_ref[...].astype(BF16), wa_ref[...], preferred_element_type=F32)
    pb = jnp.dot(ob_ref[...].astype(BF16), wb_ref[...], preferred_element_type=F32)
    merged = (jax.nn.sigmoid(ga_ref[...].astype(F32)) * pa
              + jax.nn.sigmoid(gb_ref[...].astype(F32)) * pb)
    o_ref[...] = x_ref[...] + jnp.dot(merged.astype(BF16), wo_ref[...], preferred_element_type=F32)


def _mix(x, oa, ob, u, wa, wb, wo, *, tm):
    n, d = x.shape
    const = lambda i: (0, 0)
    return pl.pallas_call(
        _mix_kernel,
        grid=(n // tm,),
        in_specs=[
            pl.BlockSpec((tm, d), lambda i: (i, 0)),
            pl.BlockSpec((tm, A_GW), lambda i: (i, 0)),
            pl.BlockSpec((tm, G_QK), lambda i: (i, 0)),
            pl.BlockSpec((tm, d), lambda i: (i, GATE_OFF // d)),
            pl.BlockSpec((tm, d), lambda i: (i, GATE_OFF // d + 1)),
            pl.BlockSpec((A_GW, d), const),
            pl.BlockSpec((G_QK, d), const),
            pl.BlockSpec((d, d), const),
        ],
        out_specs=pl.BlockSpec((tm, d), lambda i: (i, 0)),
        out_shape=jax.ShapeDtypeStruct((n, d), F32),
        compiler_params=_cparams(("parallel",)),
        name="mix",
    )(x, oa, ob, u, u, wa, wb, wo)


def _w_in_sources():
    aw = len(A_GROUPS) * A_GW
    src_q, src_k, src_v = 0, aw, 2 * aw
    src_qkvb = 3 * aw
    src_z = src_qkvb + G_CONV_CH
    src_ba = src_z + G_QK
    src_gate = src_ba + 2 * G_HEADS
    blocks = [src_qkvb + W_CB * t for t in range(G_CONV_CH // W_CB)]
    blocks += [src_z + W_CB * t for t in range(G_QK // W_CB)]
    blocks += [src_gate + W_CB * t for t in range(2 * D_MODEL // W_CB)]
    for g in range(len(A_GROUPS)):
        blocks += [src_k + g * A_GW, src_v + g * A_GW]
    blocks += [src_q + W_CB * t for t in range(aw // W_CB)]
    assert len(blocks) * W_CB == U_WIDTH and A_GW == W_CB
    assert all(b % SUBLANE == 0 for b in blocks)
    return blocks, src_ba


def _permute_cast_kernel(src_ref, w_ref, o_ref):
    del src_ref
    o_ref[...] = w_ref[...].astype(BF16)


def _permute_w_in(wt):
    d = wt.shape[1]
    blocks, _ = _w_in_sources()
    return pl.pallas_call(
        _permute_cast_kernel,
        grid_spec=pltpu.PrefetchScalarGridSpec(
            num_scalar_prefetch=1,
            grid=(len(blocks),),
            in_specs=[pl.BlockSpec((pl.Element(W_CB), pl.Element(d)),
                                   lambda j, src: (pl.multiple_of(src[j], SUBLANE), 0))],
            out_specs=pl.BlockSpec((W_CB, d), lambda j, src: (j, 0)),
        ),
        out_shape=jax.ShapeDtypeStruct((U_WIDTH, d), BF16),
        compiler_params=_cparams(("parallel",)),
        name="permute_w_in",
    )(jnp.asarray(blocks, jnp.int32), wt)


def _prep_weights(w_in, gdn_a_log, gdn_dt_bias):
    wt = w_in.T
    w_main = _permute_w_in(wt)
    src_ba = _w_in_sources()[1]
    n_ba = 2 * G_HEADS
    ba = wt[src_ba:src_ba + n_ba, :]
    ba_hi = ba.astype(BF16)
    ba_lo = (ba - ba_hi.astype(F32)).astype(BF16)
    w_ba = jnp.concatenate([ba_hi, ba_lo, jnp.zeros((MXU_COLS - 2 * n_ba, wt.shape[1]), BF16)], axis=0)
    pad = (G_HEADS, LANE - n_ba)
    alog_t = jnp.pad(gdn_a_log.astype(F32), pad).reshape(1, LANE)
    dtb_t = jnp.pad(gdn_dt_bias.astype(F32), pad).reshape(1, LANE)
    return w_main, w_ba, alog_t, dtb_t


def _dense_tiles(n_tokens):
    if n_tokens >= 2048:
        return dict(tm=2048, tf=256, tn=768, tm_mix=1024)
    return dict(tm=n_tokens, tf=D_FF // 2, tn=1536, tm_mix=n_tokens)


def _layer(x, batch, seq, conv0, s0, attn_fn, p, *, gdn_bpb, b_dtype):
    t = _dense_tiles(x.shape[0])
    tm = t["tm"]
    gdn_bpb = min(gdn_bpb, batch)
    tail = b_dtype != F32
    x1 = _ffn(x, p["norm_ffn1"], p["w_ffn1_gu"], p["w_ffn1_down"], tm=tm, tf=t["tf"])
    ub, ua, ba, *rest = _inproj(x1, p["norm_mix"], p["w_main"], p["w_ba"],
                                tm=tm, tn=t["tn"], b_dtype=b_dtype, tail=tail)
    oa, rows = attn_fn(ua)
    ob, s_new = _gdn(ub, ba, conv0, s0, p["conv_w"], p["alog_t"], p["dtb_t"], p["gdn_norm"],
                     batch=batch, seq=seq, bpb=gdn_bpb)
    x2 = _mix(x1, oa.reshape(batch * seq, A_GW), ob.reshape(batch * seq, G_QK), ub,
              p["w_proj_a"], p["w_proj_b"], p["w_out"], tm=t["tm_mix"])
    y = _ffn(x2, p["norm_ffn2"], p["w_ffn2_gu"], p["w_ffn2_down"], p["norm_out"], tm=tm, tf=t["tf"])
    if tail:
        assert seq % tm == 0 and seq >= SUBLANE
        last = rest[0].reshape(batch, seq // tm, SUBLANE, G_CONV_CH)[:, -1]
        conv_new = last[:, SUBLANE - (G_CONV - 1):, :][None]
    else:
        qkvb = ub[:, QKVB_OFF:QKVB_OFF + G_CONV_CH].reshape(batch, seq, G_CONV_CH)
        conv_new = qkvb[:, seq - (G_CONV - 1):, :][None]
    return y.reshape(batch, seq, D_MODEL), rows, conv_new, s_new[None]


def kernel(x_prompt, x_sample, cache_kv_w128, cache_kv_w512, cache_kv_w2048, state_conv, state_ssm,
           norm_ffn1, w_ffn1_gu, w_ffn1_down, norm_mix, w_in, conv_w, gdn_a_log, gdn_dt_bias, gdn_norm,
           w_proj_a, w_proj_b, w_out, norm_ffn2, w_ffn2_gu, w_ffn2_down, norm_out):
    assert w_in.shape[0] == 1, "single layer"
    n_heads = len(A_GROUPS) * A_HPG
    slopes = jnp.exp2(-8.0 * jnp.arange(1, n_heads + 1, dtype=F32) / n_heads)
    w_main, w_ba, alog_t, dtb_t = _prep_weights(w_in[0], gdn_a_log[0], gdn_dt_bias[0])
    p = dict(
        norm_ffn1=norm_ffn1[0], w_ffn1_gu=w_ffn1_gu[0], w_ffn1_down=w_ffn1_down[0],
        norm_mix=norm_mix[0], w_main=w_main, w_ba=w_ba, conv_w=conv_w[0],
        alog_t=alog_t, dtb_t=dtb_t, gdn_norm=gdn_norm[0],
        w_proj_a=w_proj_a[0].astype(BF16), w_proj_b=w_proj_b[0].astype(BF16), w_out=w_out[0].astype(BF16),
        norm_ffn2=norm_ffn2[0], w_ffn2_gu=w_ffn2_gu[0], w_ffn2_down=w_ffn2_down[0],
        norm_out=norm_out,
    )
    bp, tp, d = x_prompt.shape
    bs, ts, _ = x_sample.shape

    conv0_p = jnp.zeros((bp, G_CONV - 1, G_CONV_CH), F32)
    ssm0_p = jnp.zeros((bp, G_HEADS, G_DK, G_DV), F32)
    yp, rows_p, conv_p, ssm_p = _layer(
        x_prompt.reshape(bp * tp, d), bp, tp, conv0_p, ssm0_p,
        functools.partial(_attn_prompt, slopes, batch=bp, seq=tp), p, gdn_bpb=4, b_dtype=BF16)

    caches = (cache_kv_w128[0], cache_kv_w512[0], cache_kv_w2048[0])
    ys, rows_s, conv_s, ssm_s = _layer(
        x_sample.reshape(bs * ts, d), bs, ts, state_conv[0], state_ssm[0],
        lambda u: _attn_sample(slopes, u, caches, batch=bs, dec_seq=ts), p, gdn_bpb=8, b_dtype=F32)

    return (yp, ys, rows_p[0], rows_p[1], rows_p[2], conv_p, ssm_p,
            rows_s[0], rows_s[1], rows_s[2], conv_s, ssm_s)
```

```python
import functools

import jax
import jax.numpy as jnp
from jax import lax
from jax.experimental import pallas as pl
from jax.experimental.pallas import tpu as pltpu

F32 = jnp.float32
BF16 = jnp.bfloat16
HIGHEST = lax.Precision.HIGHEST

D_MODEL = 1024
D_FF = 2816
NORM_EPS = 1e-6
A_GROUPS = ((128, 1), (512, 4), (2048, 16))
A_HPG = 4
A_E = 128
A_BLOCK = 128
A_GW = A_HPG * A_E
A_ST = 2048
G_HEADS = 8
G_DK = 128
G_DV = 128
G_QK = G_HEADS * G_DK
G_CONV_CH = 3 * G_QK
G_CONV = 4
G_CHUNK = 64

QKVB_OFF = 0
Z_OFF = 3072
GATE_OFF = 4096
UB_WIDTH = 6144
KV_OFF = 0
QA_OFF = 3072
UA_WIDTH = 4608
U_WIDTH = UB_WIDTH + UA_WIDTH
W_CB = 512

VMEM_LIMIT = 56 * 1024 * 1024
LANE = 128
SUBLANE = 8
MXU_COLS = 256
LOG2E = 1.4426950408889634


def _cparams(sem):
    return pltpu.CompilerParams(dimension_semantics=sem, vmem_limit_bytes=VMEM_LIMIT)


def _rms_rows(x, w):
    return x * lax.rsqrt(jnp.mean(x * x, axis=-1, keepdims=True) + NORM_EPS) * w


def _ffn_kernel(x_ref, nw_ref, wg_ref, wu_ref, wd_ref, *rest, n_ff, final_norm):
    if final_norm:
        onw_ref, o_ref, hb_ref = rest
    else:
        o_ref, hb_ref = rest
    j = pl.program_id(1)

    @pl.when(j == 0)
    def _():
        hb_ref[...] = _rms_rows(x_ref[...], nw_ref[...]).astype(BF16)
        o_ref[...] = jnp.zeros_like(o_ref)

    hb = hb_ref[...]
    g = jnp.dot(hb, wg_ref[...].astype(BF16), preferred_element_type=F32)
    u = jnp.dot(hb, wu_ref[...].astype(BF16), preferred_element_type=F32)
    a = (g * jax.nn.sigmoid(g) * u).astype(BF16)
    o_ref[...] += jnp.dot(a, wd_ref[...].astype(BF16), preferred_element_type=F32)

    @pl.when(j == n_ff - 1)
    def _():
        y = x_ref[...] + 0.5 * o_ref[...]
        if final_norm:
            y = _rms_rows(y, onw_ref[...])
        o_ref[...] = y


def _ffn(x, norm_w, w_gu, w_down, out_norm_w=None, *, tm, tf):
    n, d = x.shape
    n_ff = D_FF // tf
    final_norm = out_norm_w is not None
    in_specs = [
        pl.BlockSpec((tm, d), lambda i, j: (i, 0)),
        pl.BlockSpec((1, d), lambda i, j: (0, 0)),
        pl.BlockSpec((d, tf), lambda i, j: (0, j)),
        pl.BlockSpec((d, tf), lambda i, j: (0, j + n_ff)),
        pl.BlockSpec((tf, d), lambda i, j: (j, 0)),
    ]
    args = [x, norm_w.reshape(1, d), w_gu, w_gu, w_down]
    if final_norm:
        in_specs.append(pl.BlockSpec((1, d), lambda i, j: (0, 0)))
        args.append(out_norm_w.reshape(1, d))
    return pl.pallas_call(
        functools.partial(_ffn_kernel, n_ff=n_ff, final_norm=final_norm),
        grid=(n // tm, n_ff),
        in_specs=in_specs,
        out_specs=pl.BlockSpec((tm, d), lambda i, j: (i, 0)),
        out_shape=jax.ShapeDtypeStruct((n, d), F32),
        scratch_shapes=[pltpu.VMEM((tm, d), BF16)],
        compiler_params=_cparams(("parallel", "arbitrary")),
        name="ffn_final" if final_norm else "ffn",
    )(*args)


_DN_NT = (((1,), (1,)), ((), ()))


def _inproj_kernel(x_ref, nw_ref, w_ref, wba_ref, ub_ref, ua_ref, ba_ref, *rest, nb, nq, tail):
    if tail:
        tail_ref, hb_ref = rest
    else:
        (hb_ref,) = rest
    j = pl.program_id(1)

    @pl.when(j == 0)
    def _():
        h = _rms_rows(x_ref[...], nw_ref[...])
        hb = h.astype(BF16)
        hl = (h - hb.astype(F32)).astype(BF16)
        hb_ref[...] = hb
        n_ba = 2 * G_HEADS
        r_hi = lax.dot_general(hb, wba_ref[...], _DN_NT, preferred_element_type=F32)[:, :LANE]
        r_lo = lax.dot_general(hl, wba_ref[...], _DN_NT, preferred_element_type=F32)[:, :LANE]
        lane = lax.broadcasted_iota(jnp.int32, r_hi.shape, 1)
        ba_ref[...] = jnp.where(lane < n_ba, r_hi + pltpu.roll(r_hi, LANE - n_ba, 1) + r_lo, 0.0)

    def project(rows):
        return lax.dot_general(hb_ref[rows, :], w_ref[...], _DN_NT, preferred_element_type=F32)

    @pl.when(j < nb)
    def _():
        ub_ref[...] = project(slice(None)).astype(ub_ref.dtype)

    @pl.when(j >= nb)
    def _():
        ua_ref[...] = project(slice(None))

    if tail:
        @pl.when(j < nq)
        def _():
            tm = hb_ref.shape[0]
            tail_ref[...] = project(slice(tm - 2 * SUBLANE, tm))[SUBLANE:, :]


def _inproj(x, norm_w, w_main, w_ba, *, tm, tn, b_dtype, tail):
    n, d = x.shape
    nb, nq = UB_WIDTH // tn, G_CONV_CH // tn
    assert nb * tn == UB_WIDTH and nq * tn == G_CONV_CH and U_WIDTH % tn == 0 and QKVB_OFF == 0
    out_specs = [
        pl.BlockSpec((tm, tn), lambda i, j: (i, jnp.minimum(j, nb - 1))),
        pl.BlockSpec((tm, tn), lambda i, j: (i, jnp.maximum(j - nb, 0))),
        pl.BlockSpec((tm, LANE), lambda i, j: (i, 0)),
    ]
    out_shape = [
        jax.ShapeDtypeStruct((n, UB_WIDTH), b_dtype),
        jax.ShapeDtypeStruct((n, UA_WIDTH), F32),
        jax.ShapeDtypeStruct((n, LANE), F32),
    ]
    if tail:
        out_specs.append(pl.BlockSpec((SUBLANE, tn), lambda i, j: (i, jnp.minimum(j, nq - 1))))
        out_shape.append(jax.ShapeDtypeStruct((n // tm * SUBLANE, G_CONV_CH), F32))
    return pl.pallas_call(
        functools.partial(_inproj_kernel, nb=nb, nq=nq, tail=tail),
        grid=(n // tm, U_WIDTH // tn),
        in_specs=[
            pl.BlockSpec((tm, d), lambda i, j: (i, 0)),
            pl.BlockSpec((1, d), lambda i, j: (0, 0)),
            pl.BlockSpec((tn, d), lambda i, j: (j, 0)),
            pl.BlockSpec((MXU_COLS, d), lambda i, j: (0, 0)),
        ],
        out_specs=out_specs,
        out_shape=out_shape,
        scratch_shapes=[pltpu.VMEM((tm, d), BF16)],
        compiler_params=_cparams(("parallel", "arbitrary")),
        name="inproj",
    )(x, norm_w.reshape(1, d), w_main, w_ba)


def _rows(start, size, stride):
    return pl.ds(start, size) if stride == 1 else pl.ds(start, size, stride=stride)


def _attn_prompt_kernel(slopes_ref, *refs, n_st, keeps):
    n_g = len(A_GROUPS)
    ins, o_ref = refs[:5 * n_g], refs[5 * n_g]
    kv_refs, scr = refs[5 * n_g + 1:6 * n_g + 1], refs[6 * n_g + 1:]
    st = pl.program_id(1)
    hh = pl.program_id(2)

    @pl.when(st == n_st - 1)
    def _():
        for g in range(n_g):
            keep = keeps[g]
            kc_ref, vc_ref = ins[5 * g + 1], ins[5 * g + 2]
            kv_refs[g][pl.ds(hh, keep, stride=SUBLANE), :] = kc_ref[A_ST - keep:, :]
            kv_refs[g][pl.ds(A_HPG + hh, keep, stride=SUBLANE), :] = vc_ref[A_ST - keep:, :]

    scale = A_E ** -0.5
    qi = lax.broadcasted_iota(jnp.int32, (A_BLOCK, 2 * A_BLOCK), 0)
    kj = lax.broadcasted_iota(jnp.int32, (A_BLOCK, 2 * A_BLOCK), 1)
    delta = A_BLOCK + qi - kj
    band = jnp.logical_and(delta >= 0, delta <= A_BLOCK)
    band_first = jnp.logical_and(band, jnp.logical_or(kj >= A_BLOCK, st > 0))
    dn_nt = (((1,), (1,)), ((), ()))
    for g, (win, dil) in enumerate(A_GROUPS):
        q_ref, kc_ref, vc_ref, kp_ref, vp_ref = ins[5 * g:5 * g + 5]
        og_ref, lg_ref = scr[2 * g:2 * g + 2]
        slope = slopes_ref[g * A_HPG + hh]
        bias = -slope * (dil * delta).astype(F32) * LOG2E
        bias_mid = jnp.where(band, bias, -jnp.inf)
        bias_first = jnp.where(band_first, bias, -jnp.inf)

        def two_blocks(prev_ref, cur_ref, r, j):
            if j == 0:
                return jnp.concatenate([prev_ref[_rows(r, A_BLOCK, dil), :],
                                        cur_ref[_rows(r, A_BLOCK, dil), :]], axis=0)
            return cur_ref[_rows(r + (j - 1) * A_BLOCK * dil, 2 * A_BLOCK, dil), :]

        for r in range(dil):
            for j in range(A_ST // (A_BLOCK * dil)):
                row0 = r + j * A_BLOCK * dil
                q = (q_ref[_rows(row0, A_BLOCK, dil), :] * (scale * LOG2E)).astype(BF16)
                k2 = two_blocks(kp_ref, kc_ref, r, j).astype(BF16)
                v2 = two_blocks(vp_ref, vc_ref, r, j).astype(BF16)
                s = lax.dot_general(q, k2, dn_nt, preferred_element_type=F32)
                s = s + (bias_first if j == 0 else bias_mid)
                m = jnp.max(s, axis=-1, keepdims=True)
                p = jnp.exp2(s - m)
                den = jnp.sum(p, axis=-1, keepdims=True)
                acc = jnp.dot(p.astype(BF16), v2, preferred_element_type=F32)
                og_ref[_rows(row0, A_BLOCK, dil), :] = acc / den
                lg_ref[_rows(row0, A_BLOCK, dil), :] = jnp.broadcast_to(m + jnp.log2(den), (A_BLOCK, A_E))
    chunk = 2 * A_BLOCK
    for c in range(A_ST // chunk):
        rs = slice(c * chunk, (c + 1) * chunk)
        ls = [scr[2 * g + 1][rs, :] for g in range(n_g)]
        mx = functools.reduce(jnp.maximum, ls)
        ws = [jnp.exp2(l - mx) for l in ls]
        num = sum(w * scr[2 * g][rs, :] for g, w in enumerate(ws))
        o_ref[rs, :] = num / sum(ws)


def _attn_prompt(slopes, u, *, batch, seq):
    assert seq % A_ST == 0
    u3 = u.reshape(batch, seq, UA_WIDTH)
    in_specs = [pl.BlockSpec(memory_space=pltpu.SMEM)]
    args = [slopes]
    scratch = []
    keeps = tuple(min(win, seq) for win, _ in A_GROUPS)
    assert max(keeps) <= A_ST
    for g, (win, dil) in enumerate(A_GROUPS):
        assert win == A_BLOCK * dil and A_ST % win == 0
        qcb = (QA_OFF + g * A_GW) // A_E
        kcb = (KV_OFF + g * 2 * A_GW) // A_E
        vcb = kcb + A_HPG
        per = A_ST // win

        def cur(cb):
            return pl.BlockSpec((None, A_ST, A_E), lambda b, st, hh, cb=cb: (b, st, cb + hh))

        def prv(cb, win=win, per=per):
            return pl.BlockSpec((None, win, A_E),
                                lambda b, st, hh, cb=cb: (b, jnp.maximum(st * per - 1, 0), cb + hh))

        in_specs += [cur(qcb), cur(kcb), cur(vcb), prv(kcb), prv(vcb)]
        args += [u3] * 5
        scratch += [pltpu.VMEM((A_ST, A_E), F32), pltpu.VMEM((A_ST, A_E), F32)]
    n_st = seq // A_ST
    out_specs = [pl.BlockSpec((None, A_ST, A_E), lambda b, st, hh: (b, st, hh))]
    out_shape = [jax.ShapeDtypeStruct((batch, seq, A_GW), F32)]
    for keep in keeps:
        out_specs.append(pl.BlockSpec((None, keep * SUBLANE, A_E), lambda b, st, hh: (b, 0, 0),
                                      pipeline_mode=pl.Buffered(1)))
        out_shape.append(jax.ShapeDtypeStruct((batch, keep * SUBLANE, A_E), F32))
    outs = pl.pallas_call(
        functools.partial(_attn_prompt_kernel, n_st=n_st, keeps=keeps),
        grid=(batch, n_st, A_HPG),
        in_specs=in_specs,
        out_specs=out_specs,
        out_shape=out_shape,
        scratch_shapes=scratch,
        compiler_params=_cparams(("parallel", "arbitrary", "arbitrary")),
        name="attn_prompt",
    )(*args)
    rows = [kv.reshape(1, batch, keep, 2, A_HPG, A_E) for kv, keep in zip(outs[1:], keeps)]
    return outs[0], rows


def _pick_rows(rows):
    sub = lax.broadcasted_iota(jnp.int32, (SUBLANE, A_E), 0)
    out = jnp.zeros((SUBLANE, A_E), F32)
    for i, r in enumerate(rows):
        out = jnp.where(sub == i, jnp.broadcast_to(r, (SUBLANE, A_E)), out)
    return out


def _attn_sample_kernel(slopes_ref, q_ref, kv_ref, c0_ref, c1_ref, c2_ref, o_ref, *kvo_refs, dec_seq, nseq):
    scale = A_E ** -0.5
    caches = (c0_ref, c1_ref, c2_ref)
    for g, kvo_ref in enumerate(kvo_refs):
        for t in range(nseq * dec_seq):
            kvo_ref[t * SUBLANE:(t + 1) * SUBLANE, :] = _pick_rows(
                [kv_ref[t:t + 1, g * 2 * A_GW + c * A_E:g * 2 * A_GW + (c + 1) * A_E] for c in range(2 * A_HPG)])
    n_keys = A_BLOCK * SUBLANE
    sub = lax.broadcasted_iota(jnp.int32, (SUBLANE, n_keys), 0)
    lane = lax.broadcasted_iota(jnp.int32, (SUBLANE, n_keys), 1)
    tile_row = lane & (SUBLANE - 1)
    m_idx = lane >> 3
    own_k = tile_row == sub
    sub1 = lax.broadcasted_iota(jnp.int32, (SUBLANE, 1), 0)
    dn_nt = (((1,), (1,)), ((), ()))
    slope8s = []
    for g in range(len(A_GROUPS)):
        slope8 = jnp.zeros((SUBLANE, 1), F32)
        for h in range(A_HPG):
            slope8 = jnp.where(sub1 == h, slopes_ref[g * A_HPG + h], slope8)
        slope8s.append(slope8)

    def keys(bb, s, g):
        res = 0 if A_GROUPS[g][1] == 1 else s
        return caches[g][bb, :, res, :, :].reshape(n_keys, A_E).astype(BF16)

    units = [(bb, s, g) for bb in range(nseq) for s in range(dec_seq) for g in range(len(A_GROUPS))]
    q8s = [_pick_rows([q_ref[bb * dec_seq + s:bb * dec_seq + s + 1, g * A_GW + h * A_E:g * A_GW + (h + 1) * A_E]
                       for h in range(A_HPG)]) for bb, s, g in units]
    scs = [lax.dot_general(q8.astype(BF16), keys(*u), dn_nt, preferred_element_type=F32) * scale
           for q8, u in zip(q8s, units)]
    ms, dens, pvs, news = [], [], [], []
    for (bb, s, g), q8, sc in zip(units, q8s, scs):
        win, dil = A_GROUPS[g]
        row = bb * dec_seq + s
        slope8 = slope8s[g]
        if dil == 1:
            dist = (win + s - m_idx).astype(F32)
            valid = jnp.logical_and(own_k, m_idx >= s)
            new_rows = [(bb * dec_seq + t, float(s - t)) for t in range(s + 1)]
        else:
            dist = (dil * (win // dil - m_idx)).astype(F32)
            valid = own_k
            new_rows = [(row, 0.0)]
        sc = jnp.where(valid, sc - slope8 * dist, -jnp.inf)
        m = jnp.max(sc, axis=-1, keepdims=True)
        koff = g * 2 * A_GW
        new = []
        for nrow, ndist in new_rows:
            k8 = _pick_rows([kv_ref[nrow:nrow + 1, koff + h * A_E:koff + (h + 1) * A_E]
                             for h in range(A_HPG)])
            v8 = _pick_rows([kv_ref[nrow:nrow + 1, koff + A_GW + h * A_E:koff + A_GW + (h + 1) * A_E]
                             for h in range(A_HPG)])
            s_n = jnp.sum(k8 * q8, axis=-1, keepdims=True) * scale - slope8 * ndist
            m = jnp.maximum(m, s_n)
            new.append((s_n, v8))
        p = jnp.exp(sc - m)
        ms.append(m)
        dens.append(jnp.sum(p, axis=-1, keepdims=True))
        pvs.append(pltpu.roll(p, A_HPG, 1).astype(BF16))
        news.append(new)
    accs = [jnp.dot(pv, keys(*u), preferred_element_type=F32) for pv, u in zip(pvs, units)]

    out_tiles = [jnp.zeros((nseq * dec_seq, A_E), F32) for _ in range(A_HPG)]
    out_sub = lax.broadcasted_iota(jnp.int32, (nseq * dec_seq, A_E), 0)
    n_g = len(A_GROUPS)
    for ui in range(0, len(units), n_g):
        bb, s, _ = units[ui]
        outs, lses = [], []
        for m, den, acc, new in zip(ms[ui:ui + n_g], dens[ui:ui + n_g], accs[ui:ui + n_g], news[ui:ui + n_g]):
            for s_n, v8 in new:
                p_n = jnp.exp(s_n - m)
                den = den + p_n
                acc = acc + p_n * v8
            outs.append(acc / den)
            lses.append(m + jnp.log(den))
        mx = functools.reduce(jnp.maximum, lses)
        ws = [jnp.exp(l - mx) for l in lses]
        o8 = sum(w * o for w, o in zip(ws, outs)) / sum(ws)
        for h in range(A_HPG):
            out_tiles[h] = jnp.where(out_sub == bb * dec_seq + s,
                                     jnp.broadcast_to(o8[h:h + 1, :], (nseq * dec_seq, A_E)), out_tiles[h])
    for h in range(A_HPG):
        o_ref[:, h * A_E:(h + 1) * A_E] = out_tiles[h]


def _attn_sample(slopes, u, caches, *, batch, dec_seq):
    nseq = SUBLANE // dec_seq
    assert nseq * dec_seq == SUBLANE and batch % nseq == 0
    views, specs = [], []
    for (win, dil), cache in zip(A_GROUPS, caches):
        wb = cache.shape[1]
        assert wb == win and wb // dil == A_BLOCK and (dil == 1 or dec_seq <= dil)
        views.append(cache.reshape(batch, wb // dil, dil, SUBLANE, A_E))
        res = 1 if dil == 1 else dec_seq
        specs.append(pl.BlockSpec((nseq, A_BLOCK, res, SUBLANE, A_E), lambda i: (i, 0, 0, 0, 0)))
    n_q = len(A_GROUPS) * A_GW
    rows = nseq * dec_seq
    outs = pl.pallas_call(
        functools.partial(_attn_sample_kernel, dec_seq=dec_seq, nseq=nseq),
        grid=(batch // nseq,),
        in_specs=[
            pl.BlockSpec(memory_space=pltpu.SMEM),
            pl.BlockSpec((rows, n_q), lambda i: (i, QA_OFF // n_q)),
            pl.BlockSpec((rows, 2 * n_q), lambda i: (i, KV_OFF // (2 * n_q))),
        ] + specs,
        out_specs=[pl.BlockSpec((rows, A_GW), lambda i: (i, 0))]
        + [pl.BlockSpec((rows * SUBLANE, A_E), lambda i: (i, 0))] * len(A_GROUPS),
        out_shape=[jax.ShapeDtypeStruct((batch * dec_seq, A_GW), F32)]
        + [jax.ShapeDtypeStruct((batch * dec_seq * SUBLANE, A_E), F32)] * len(A_GROUPS),
        compiler_params=_cparams(("parallel",)),
        name="attn_sample",
    )(slopes, u, u, *views)
    return outs[0], [kv.reshape(1, batch, dec_seq, 2, A_HPG, A_E) for kv in outs[1:]]


def _bdot(a, b, dn=(((1,), (0,)), ((), ()))):
    return lax.dot_general(a.astype(BF16), b.astype(BF16), dn, preferred_element_type=F32)


def _gdn_kernel(qkv_ref, z_ref, ba_ref, cw_ref, alog_ref, dtb_ref, nw_ref, conv0_ref, s0_ref,
                o_ref, sout_ref, cb_ref, s_ref, *, tb, cc, nc, bpb, flat):
    n = pl.program_id(1)

    def seq_rows(bi):
        return (slice(bi * tb, (bi + 1) * tb),) if flat else (bi, slice(None))

    @pl.when(n == 0)
    def _():
        s_ref[...] = s0_ref[...]
        cb_ref[:, 0:SUBLANE, :] = conv0_ref[...]

    rows = lax.broadcasted_iota(jnp.int32, (cc, 1), 0)
    live = rows < tb
    ii = lax.broadcasted_iota(jnp.int32, (cc, cc), 0)
    jj = lax.broadcasted_iota(jnp.int32, (cc, cc), 1)
    incl = ii >= jj
    strict = ii > jj
    tri = incl.astype(F32)
    cw = cw_ref[...]
    first = SUBLANE - (G_CONV - 1)
    dn_nt = (((1,), (1,)), ((), ()))
    dn_tn = (((0,), (0,)), ((), ()))
    eye = ii == jj

    qs, ks, kbs, decays, rhss, e_gcs, e_rests, e_ends = [], [], [], [], [], [], [], []
    idx = range(bpb * G_HEADS)
    for bi in range(bpb):
        x = qkv_ref[seq_rows(bi)].astype(F32)
        cb_ref[bi, SUBLANE:SUBLANE + tb, :] = x
        if tb < cc:
            cb_ref[bi, SUBLANE + tb:SUBLANE + cc, :] = jnp.zeros((cc - tb, G_CONV_CH), F32)
        y = cb_ref[bi, SUBLANE:SUBLANE + cc, :] * cw[G_CONV - 1:G_CONV, :]
        for j in range(G_CONV - 2, -1, -1):
            y = y + cb_ref[bi, first + j:first + j + cc, :] * cw[j:j + 1, :]
        if nc > 1:
            cb_ref[bi, 0:SUBLANE, :] = x[tb - SUBLANE:tb, :]
        y = y * jax.nn.sigmoid(y)

        ba = ba_ref[seq_rows(bi)]
        if tb < cc:
            ba = jnp.concatenate([ba, jnp.zeros((cc - tb, LANE), F32)], axis=0)
        beta_t = jnp.where(live, jax.nn.sigmoid(ba), 0.0)
        xs = ba + dtb_ref[...]
        softplus = jnp.maximum(xs, 0.0) + jnp.log(1.0 + jnp.exp(-jnp.abs(xs)))
        g_t = jnp.where(live, -jnp.exp(alog_ref[...]) * softplus, 0.0)
        gc = jnp.dot(tri, g_t, precision=HIGHEST, preferred_element_type=F32)
        gc_t = jnp.concatenate([gc, jnp.zeros((LANE - cc, LANE), F32)], axis=0).T
        gc_last = gc[cc - 1:cc, :]
        e_gc = jnp.exp(gc)
        e_rest = jnp.exp(gc_last - gc)
        e_end = jnp.exp(gc_last)
        for h in range(G_HEADS):
            sl = slice(h * G_DK, (h + 1) * G_DK)
            gl = slice(G_HEADS + h, G_HEADS + h + 1)
            q = y[:, sl]
            k = y[:, G_QK + h * G_DK:G_QK + (h + 1) * G_DK]
            v = y[:, 2 * G_QK + h * G_DV:2 * G_QK + (h + 1) * G_DV]
            q = q * (lax.rsqrt(jnp.sum(q * q, axis=-1, keepdims=True) + NORM_EPS) * (G_DK ** -0.5))
            k = k * lax.rsqrt(jnp.sum(k * k, axis=-1, keepdims=True) + NORM_EPS)
            if tb < cc:
                q = jnp.where(live, q, 0.0)
                k = jnp.where(live, k, 0.0)
                v = jnp.where(live, v, 0.0)
            beta = beta_t[:, h:h + 1]
            kb = k * beta
            qs.append(q)
            ks.append(k)
            kbs.append(kb)
            decays.append(jnp.exp(jnp.where(
                incl, gc[:, gl] - gc_t[G_HEADS + h:G_HEADS + h + 1, 0:cc], -jnp.inf)))
            rhss.append(jnp.concatenate([v * beta, kb * e_gc[:, gl]], axis=1))
            e_gcs.append(e_gc[:, gl])
            e_rests.append(e_rest[:, gl])
            e_ends.append(e_end[:, gl])

    kq = [_bdot(jnp.concatenate([kbs[i], qs[i]], axis=0), ks[i], dn_nt) for i in idx]
    a_low = [jnp.where(strict, kq[i][:cc] * decays[i], 0.0) for i in idx]
    qk = [kq[i][cc:] * decays[i] for i in idx]
    inv = [jnp.where(eye, 1.0, -a_low[i]) for i in idx]
    pw = a_low
    span = 2
    while span < cc:
        pw = [_bdot(pw[i], pw[i]) for i in idx]
        inv = [inv[i] + _bdot(pw[i], inv[i]) for i in idx]
        span *= 2
    rhss = [_bdot(inv[i], rhss[i]) for i in idx]
    s_old = [s_ref[i // G_HEADS, i % G_HEADS] for i in idx]
    ws = [_bdot(jnp.concatenate([rhss[i][:, G_DV:], qs[i] * e_gcs[i]], axis=0), s_old[i]) for i in idx]
    v_new = [rhss[i][:, :G_DV] - ws[i][:cc] for i in idx]
    for i in idx:
        s_ref[i // G_HEADS, i % G_HEADS] = (
            s_old[i] * e_ends[i] + _bdot(ks[i] * e_rests[i], v_new[i], dn_tn))
    o = [ws[i][cc:] + _bdot(qk[i], v_new[i]) for i in idx]
    for i in idx:
        bi, h = i // G_HEADS, i % G_HEADS
        sl = slice(h * G_DV, (h + 1) * G_DV)
        o_h = o[i] * lax.rsqrt(jnp.mean(o[i] * o[i], axis=-1, keepdims=True) + NORM_EPS) * nw_ref[...]
        zz = z_ref[seq_rows(bi) + (sl,)].astype(F32)
        o_ref[seq_rows(bi) + (sl,)] = o_h[:tb] * (zz * jax.nn.sigmoid(zz))

    @pl.when(n == nc - 1)
    def _():
        sout_ref[...] = s_ref[...]


def _gdn(u, ba, conv0, s0, conv_w, alog_t, dtb_t, norm_w, *, batch, seq, bpb):
    cc = G_CHUNK if seq >= G_CHUNK else SUBLANE
    tb = min(seq, cc)
    nc = seq // tb
    assert nc * tb == seq and (nc == 1 or tb == cc) and batch % bpb == 0
    flat = nc == 1
    if flat:
        assert (bpb * tb) % SUBLANE == 0
        uv, bav, o_rows = u, ba, batch * seq

        def rows(width, col):
            return pl.BlockSpec((bpb * tb, width), lambda b, n: (b, col))
    else:
        uv, bav, o_rows = u.reshape(batch, seq, UB_WIDTH), ba.reshape(batch, seq, LANE), batch

        def rows(width, col):
            return pl.BlockSpec((bpb, tb, width), lambda b, n: (b, n, col))
    o_shape = (o_rows, G_QK) if flat else (batch, seq, G_QK)
    conv0p = jnp.concatenate(
        [jnp.zeros((batch, SUBLANE - (G_CONV - 1), G_CONV_CH), F32), conv0.astype(F32)], axis=1)
    const2 = lambda b, n: (0, 0)
    o, s_new = pl.pallas_call(
        functools.partial(_gdn_kernel, tb=tb, cc=cc, nc=nc, bpb=bpb, flat=flat),
        grid=(batch // bpb, nc),
        in_specs=[
            rows(G_CONV_CH, QKVB_OFF // G_CONV_CH),
            rows(G_QK, Z_OFF // G_QK),
            rows(LANE, 0),
            pl.BlockSpec((G_CONV, G_CONV_CH), const2),
            pl.BlockSpec((1, LANE), const2),
            pl.BlockSpec((1, LANE), const2),
            pl.BlockSpec((1, G_DV), const2),
            pl.BlockSpec((bpb, SUBLANE, G_CONV_CH), lambda b, n: (b, 0, 0)),
            pl.BlockSpec((bpb, G_HEADS, G_DK, G_DV), lambda b, n: (b, 0, 0, 0)),
        ],
        out_specs=[
            rows(G_QK, 0),
            pl.BlockSpec((bpb, G_HEADS, G_DK, G_DV), lambda b, n: (b, 0, 0, 0)),
        ],
        out_shape=[
            jax.ShapeDtypeStruct(o_shape, F32),
            jax.ShapeDtypeStruct((batch, G_HEADS, G_DK, G_DV), F32),
        ],
        scratch_shapes=[
            pltpu.VMEM((bpb, SUBLANE + cc, G_CONV_CH), F32),
            pltpu.VMEM((bpb, G_HEADS, G_DK, G_DV), F32),
        ],
        compiler_params=_cparams(("parallel", "arbitrary")),
        name="gdn",
    )(uv, uv, bav, conv_w, alog_t, dtb_t, norm_w.reshape(1, G_DV), conv0p, s0)
    return o.reshape(batch * seq, G_QK), s_new


def _mix_kernel(x_ref, oa_ref, ob_ref, ga_ref, gb_ref, wa_ref, wb_ref, wo_ref, o_ref):
    pa = jnp.dot(oa_ref[...].astype(BF16), wa_ref[...], preferred_element_type=F32)
    pb = jnp.dot(ob_ref[...].astype(BF16), wb_ref[...], preferred_element_type=F32)
    merged = (jax.nn.sigmoid(ga_ref[...].astype(F32)) * pa
              + jax.nn.sigmoid(gb_ref[...].astype(F32)) * pb)
    o_ref[...] = x_ref[...] + jnp.dot(merged.astype(BF16), wo_ref[...], preferred_element_type=F32)


def _mix(x, oa, ob, u, wa, wb, wo, *, tm):
    n, d = x.shape
    const = lambda i: (0, 0)
    return pl.pallas_call(
        _mix_kernel,
        grid=(n // tm,),
        in_specs=[
            pl.BlockSpec((tm, d), lambda i: (i, 0)),
            pl.BlockSpec((tm, A_GW), lambda i: (i, 0)),
            pl.BlockSpec((tm, G_QK), lambda i: (i, 0)),
            pl.BlockSpec((tm, d), lambda i: (i, GATE_OFF // d)),
            pl.BlockSpec((tm, d), lambda i: (i, GATE_OFF // d + 1)),
            pl.BlockSpec((A_GW, d), const),
            pl.BlockSpec((G_QK, d), const),
            pl.BlockSpec((d, d), const),
        ],
        out_specs=pl.BlockSpec((tm, d), lambda i: (i, 0)),
        out_shape=jax.ShapeDtypeStruct((n, d), F32),
        compiler_params=_cparams(("parallel",)),
        name="mix",
    )(x, oa, ob, u, u, wa, wb, wo)


def _w_in_sources():
    aw = len(A_GROUPS) * A_GW
    src_q, src_k, src_v = 0, aw, 2 * aw
    src_qkvb = 3 * aw
    src_z = src_qkvb + G_CONV_CH
    src_ba = src_z + G_QK
    src_gate = src_ba + 2 * G_HEADS
    blocks = [src_qkvb + W_CB * t for t in range(G_CONV_CH // W_CB)]
    blocks += [src_z + W_CB * t for t in range(G_QK // W_CB)]
    blocks += [src_gate + W_CB * t for t in range(2 * D_MODEL // W_CB)]
    for g in range(len(A_GROUPS)):
        blocks += [src_k + g * A_GW, src_v + g * A_GW]
    blocks += [src_q + W_CB * t for t in range(aw // W_CB)]
    assert len(blocks) * W_CB == U_WIDTH and A_GW == W_CB
    assert all(b % SUBLANE == 0 for b in blocks)
    return blocks, src_ba


def _permute_cast_kernel(src_ref, w_ref, o_ref):
    del src_ref
    o_ref[...] = w_ref[...].astype(BF16)


def _permute_w_in(wt):
    d = wt.shape[1]
    blocks, _ = _w_in_sources()
    return pl.pallas_call(
        _permute_cast_kernel,
        grid_spec=pltpu.PrefetchScalarGridSpec(
            num_scalar_prefetch=1,
            grid=(len(blocks),),
            in_specs=[pl.BlockSpec((pl.Element(W_CB), pl.Element(d)),
                                   lambda j, src: (pl.multiple_of(src[j], SUBLANE), 0))],
            out_specs=pl.BlockSpec((W_CB, d), lambda j, src: (j, 0)),
        ),
        out_shape=jax.ShapeDtypeStruct((U_WIDTH, d), BF16),
        compiler_params=_cparams(("parallel",)),
        name="permute_w_in",
    )(jnp.asarray(blocks, jnp.int32), wt)


def _prep_weights(w_in, gdn_a_log, gdn_dt_bias):
    wt = w_in.T
    w_main = _permute_w_in(wt)
    src_ba = _w_in_sources()[1]
    n_ba = 2 * G_HEADS
    ba = wt[src_ba:src_ba + n_ba, :]
    ba_hi = ba.astype(BF16)
    ba_lo = (ba - ba_hi.astype(F32)).astype(BF16)
    w_ba = jnp.concatenate([ba_hi, ba_lo, jnp.zeros((MXU_COLS - 2 * n_ba, wt.shape[1]), BF16)], axis=0)
    pad = (G_HEADS, LANE - n_ba)
    alog_t = jnp.pad(gdn_a_log.astype(F32), pad).reshape(1, LANE)
    dtb_t = jnp.pad(gdn_dt_bias.astype(F32), pad).reshape(1, LANE)
    return w_main, w_ba, alog_t, dtb_t


def _dense_tiles(n_tokens):
    if n_tokens >= 2048:
        return dict(tm=2048, tf=256, tn=768, tm_mix=1024)
    return dict(tm=n_tokens, tf=256, tn=1536, tm_mix=n_tokens)


def _layer(x, batch, seq, conv0, s0, attn_fn, p, *, gdn_bpb, b_dtype):
    t = _dense_tiles(x.shape[0])
    tm = t["tm"]
    gdn_bpb = min(gdn_bpb, batch)
    tail = b_dtype != F32
    x1 = _ffn(x, p["norm_ffn1"], p["w_ffn1_gu"], p["w_ffn1_down"], tm=tm, tf=t["tf"])
    ub, ua, ba, *rest = _inproj(x1, p["norm_mix"], p["w_main"], p["w_ba"],
                                tm=tm, tn=t["tn"], b_dtype=b_dtype, tail=tail)
    oa, rows = attn_fn(ua)
    ob, s_new = _gdn(ub, ba, conv0, s0, p["conv_w"], p["alog_t"], p["dtb_t"], p["gdn_norm"],
                     batch=batch, seq=seq, bpb=gdn_bpb)
    x2 = _mix(x1, oa.reshape(batch * seq, A_GW), ob.reshape(batch * seq, G_QK), ub,
              p["w_proj_a"], p["w_proj_b"], p["w_out"], tm=t["tm_mix"])
    y = _ffn(x2, p["norm_ffn2"], p["w_ffn2_gu"], p["w_ffn2_down"], p["norm_out"], tm=tm, tf=t["tf"])
    if tail:
        assert seq % tm == 0 and seq >= SUBLANE
        last = rest[0].reshape(batch, seq // tm, SUBLANE, G_CONV_CH)[:, -1]
        conv_new = last[:, SUBLANE - (G_CONV - 1):, :][None]
    else:
        qkvb = ub[:, QKVB_OFF:QKVB_OFF + G_CONV_CH].reshape(batch, seq, G_CONV_CH)
        conv_new = qkvb[:, seq - (G_CONV - 1):, :][None]
    return y.reshape(batch, seq, D_MODEL), rows, conv_new, s_new[None]


def kernel(x_prompt, x_sample, cache_kv_w128, cache_kv_w512, cache_kv_w2048, state_conv, state_ssm,
           norm_ffn1, w_ffn1_gu, w_ffn1_down, norm_mix, w_in, conv_w, gdn_a_log, gdn_dt_bias, gdn_norm,
           w_proj_a, w_proj_b, w_out, norm_ffn2, w_ffn2_gu, w_ffn2_down, norm_out):
    assert w_in.shape[0] == 1, "single layer"
    n_heads = len(A_GROUPS) * A_HPG
    slopes = jnp.exp2(-8.0 * jnp.arange(1, n_heads + 1, dtype=F32) / n_heads)
    w_main, w_ba, alog_t, dtb_t = _prep_weights(w_in[0], gdn_a_log[0], gdn_dt_bias[0])
    p = dict(
        norm_ffn1=norm_ffn1[0], w_ffn1_gu=w_ffn1_gu[0], w_ffn1_down=w_ffn1_down[0],
        norm_mix=norm_mix[0], w_main=w_main, w_ba=w_ba, conv_w=conv_w[0],
        alog_t=alog_t, dtb_t=dtb_t, gdn_norm=gdn_norm[0],
        w_proj_a=w_proj_a[0].astype(BF16), w_proj_b=w_proj_b[0].astype(BF16), w_out=w_out[0].astype(BF16),
        norm_ffn2=norm_ffn2[0], w_ffn2_gu=w_ffn2_gu[0], w_ffn2_down=w_ffn2_down[0],
        norm_out=norm_out,
    )
    bp, tp, d = x_prompt.shape
    bs, ts, _ = x_sample.shape

    conv0_p = jnp.zeros((bp, G_CONV - 1, G_CONV_CH), F32)
    ssm0_p = jnp.zeros((bp, G_HEADS, G_DK, G_DV), F32)
    yp, rows_p, conv_p, ssm_p = _layer(
        x_prompt.reshape(bp * tp, d), bp, tp, conv0_p, ssm0_p,
        functools.partial(_attn_prompt, slopes, batch=bp, seq=tp), p, gdn_bpb=4, b_dtype=BF16)

    caches = (cache_kv_w128[0], cache_kv_w512[0], cache_kv_w2048[0])
    ys, rows_s, conv_s, ssm_s = _layer(
        x_sample.reshape(bs * ts, d), bs, ts, state_conv[0], state_ssm[0],
        lambda u: _attn_sample(slopes, u, caches, batch=bs, dec_seq=ts), p, gdn_bpb=8, b_dtype=F32)

    return (yp, ys, rows_p[0], rows_p[1], rows_p[2], conv_p, ssm_p,
            rows_s[0], rows_s[1], rows_s[2], conv_s, ssm_s)
```

```python
import functools

import jax
import jax.numpy as jnp
from jax import lax
from jax.experimental import pallas as pl
from jax.experimental.pallas import tpu as pltpu

F32 = jnp.float32
BF16 = jnp.bfloat16
HIGHEST = lax.Precision.HIGHEST

D_MODEL = 1024
D_FF = 2816
NORM_EPS = 1e-6
A_GROUPS = ((128, 1), (512, 4), (2048, 16))
A_HPG = 4
A_E = 128
A_BLOCK = 128
A_GW = A_HPG * A_E
A_ST = 2048
A_SAMPLE_SEQS = 4
G_HEADS = 8
G_DK = 128
G_DV = 128
G_QK = G_HEADS * G_DK
G_CONV_CH = 3 * G_QK
G_CONV = 4
G_CHUNK = 64

QKVB_OFF = 0
Z_OFF = 3072
GATE_OFF = 4096
UB_WIDTH = 6144
KV_OFF = 0
QA_OFF = 3072
UA_WIDTH = 4608
U_WIDTH = UB_WIDTH + UA_WIDTH
W_CB = 512

VMEM_LIMIT = 56 * 1024 * 1024
LANE = 128
SUBLANE = 8
MXU_COLS = 256
LOG2E = 1.4426950408889634


def _cparams(sem):
    return pltpu.CompilerParams(dimension_semantics=sem, vmem_limit_bytes=VMEM_LIMIT)


def _rms_rows(x, w):
    return x * lax.rsqrt(jnp.mean(x * x, axis=-1, keepdims=True) + NORM_EPS) * w


def _ffn_kernel(x_ref, nw_ref, wg_ref, wu_ref, wd_ref, *rest, n_ff, final_norm):
    if final_norm:
        onw_ref, o_ref, hb_ref = rest
    else:
        o_ref, hb_ref = rest
    j = pl.program_id(1)

    @pl.when(j == 0)
    def _():
        hb_ref[...] = _rms_rows(x_ref[...], nw_ref[...]).astype(BF16)
        o_ref[...] = jnp.zeros_like(o_ref)

    hb = hb_ref[...]
    g = jnp.dot(hb, wg_ref[...].astype(BF16), preferred_element_type=F32)
    u = jnp.dot(hb, wu_ref[...].astype(BF16), preferred_element_type=F32)
    a = (g * jax.nn.sigmoid(g) * u).astype(BF16)
    o_ref[...] += jnp.dot(a, wd_ref[...].astype(BF16), preferred_element_type=F32)

    @pl.when(j == n_ff - 1)
    def _():
        y = x_ref[...] + 0.5 * o_ref[...]
        if final_norm:
            y = _rms_rows(y, onw_ref[...])
        o_ref[...] = y


def _ffn(x, norm_w, w_gu, w_down, out_norm_w=None, *, tm, tf):
    n, d = x.shape
    n_ff = D_FF // tf
    final_norm = out_norm_w is not None
    in_specs = [
        pl.BlockSpec((tm, d), lambda i, j: (i, 0)),
        pl.BlockSpec((1, d), lambda i, j: (0, 0)),
        pl.BlockSpec((d, tf), lambda i, j: (0, j)),
        pl.BlockSpec((d, tf), lambda i, j: (0, j + n_ff)),
        pl.BlockSpec((tf, d), lambda i, j: (j, 0)),
    ]
    args = [x, norm_w.reshape(1, d), w_gu, w_gu, w_down]
    if final_norm:
        in_specs.append(pl.BlockSpec((1, d), lambda i, j: (0, 0)))
        args.append(out_norm_w.reshape(1, d))
    return pl.pallas_call(
        functools.partial(_ffn_kernel, n_ff=n_ff, final_norm=final_norm),
        grid=(n // tm, n_ff),
        in_specs=in_specs,
        out_specs=pl.BlockSpec((tm, d), lambda i, j: (i, 0)),
        out_shape=jax.ShapeDtypeStruct((n, d), F32),
        scratch_shapes=[pltpu.VMEM((tm, d), BF16)],
        compiler_params=_cparams(("parallel", "arbitrary")),
        name="ffn_final" if final_norm else "ffn",
    )(*args)


_DN_NT = (((1,), (1,)), ((), ()))


def _inproj_kernel(x_ref, nw_ref, w_ref, wba_ref, ub_ref, ua_ref, ba_ref, *rest, nb, nq, tail):
    if tail:
        tail_ref, hb_ref = rest
    else:
        (hb_ref,) = rest
    j = pl.program_id(1)

    @pl.when(j == 0)
    def _():
        h = _rms_rows(x_ref[...], nw_ref[...])
        hb = h.astype(BF16)
        hl = (h - hb.astype(F32)).astype(BF16)
        hb_ref[...] = hb
        n_ba = 2 * G_HEADS
        r_hi = lax.dot_general(hb, wba_ref[...], _DN_NT, preferred_element_type=F32)[:, :LANE]
        r_lo = lax.dot_general(hl, wba_ref[...], _DN_NT, preferred_element_type=F32)[:, :LANE]
        lane = lax.broadcasted_iota(jnp.int32, r_hi.shape, 1)
        ba_ref[...] = jnp.where(lane < n_ba, r_hi + pltpu.roll(r_hi, LANE - n_ba, 1) + r_lo, 0.0)

    def project(rows):
        return lax.dot_general(hb_ref[rows, :], w_ref[...], _DN_NT, preferred_element_type=F32)

    @pl.when(j < nb)
    def _():
        ub_ref[...] = project(slice(None)).astype(ub_ref.dtype)

    @pl.when(j >= nb)
    def _():
        ua_ref[...] = project(slice(None))

    if tail:
        @pl.when(j < nq)
        def _():
            tm = hb_ref.shape[0]
            tail_ref[...] = project(slice(tm - 2 * SUBLANE, tm))[SUBLANE:, :]


def _inproj(x, norm_w, w_main, w_ba, *, tm, tn, b_dtype, tail):
    n, d = x.shape
    nb, nq = UB_WIDTH // tn, G_CONV_CH // tn
    assert nb * tn == UB_WIDTH and nq * tn == G_CONV_CH and U_WIDTH % tn == 0 and QKVB_OFF == 0
    out_specs = [
        pl.BlockSpec((tm, tn), lambda i, j: (i, jnp.minimum(j, nb - 1))),
        pl.BlockSpec((tm, tn), lambda i, j: (i, jnp.maximum(j - nb, 0))),
        pl.BlockSpec((tm, LANE), lambda i, j: (i, 0)),
    ]
    out_shape = [
        jax.ShapeDtypeStruct((n, UB_WIDTH), b_dtype),
        jax.ShapeDtypeStruct((n, UA_WIDTH), F32),
        jax.ShapeDtypeStruct((n, LANE), F32),
    ]
    if tail:
        out_specs.append(pl.BlockSpec((SUBLANE, tn), lambda i, j: (i, jnp.minimum(j, nq - 1))))
        out_shape.append(jax.ShapeDtypeStruct((n // tm * SUBLANE, G_CONV_CH), F32))
    return pl.pallas_call(
        functools.partial(_inproj_kernel, nb=nb, nq=nq, tail=tail),
        grid=(n // tm, U_WIDTH // tn),
        in_specs=[
            pl.BlockSpec((tm, d), lambda i, j: (i, 0)),
            pl.BlockSpec((1, d), lambda i, j: (0, 0)),
            pl.BlockSpec((tn, d), lambda i, j: (j, 0)),
            pl.BlockSpec((MXU_COLS, d), lambda i, j: (0, 0)),
        ],
        out_specs=out_specs,
        out_shape=out_shape,
        scratch_shapes=[pltpu.VMEM((tm, d), BF16)],
        compiler_params=_cparams(("parallel", "arbitrary")),
        name="inproj",
    )(x, norm_w.reshape(1, d), w_main, w_ba)


def _rows(start, size, stride):
    return pl.ds(start, size) if stride == 1 else pl.ds(start, size, stride=stride)


def _attn_prompt_kernel(slopes_ref, *refs, n_st, keeps):
    n_g = len(A_GROUPS)
    ins, o_ref = refs[:5 * n_g], refs[5 * n_g]
    kv_refs, scr = refs[5 * n_g + 1:6 * n_g + 1], refs[6 * n_g + 1:]
    st = pl.program_id(1)
    hh = pl.program_id(2)

    @pl.when(st == n_st - 1)
    def _():
        for g in range(n_g):
            keep = keeps[g]
            kc_ref, vc_ref = ins[5 * g + 1], ins[5 * g + 2]
            kv_refs[g][pl.ds(hh, keep, stride=SUBLANE), :] = kc_ref[A_ST - keep:, :]
            kv_refs[g][pl.ds(A_HPG + hh, keep, stride=SUBLANE), :] = vc_ref[A_ST - keep:, :]

    scale = A_E ** -0.5
    qi = lax.broadcasted_iota(jnp.int32, (A_BLOCK, 2 * A_BLOCK), 0)
    kj = lax.broadcasted_iota(jnp.int32, (A_BLOCK, 2 * A_BLOCK), 1)
    delta = A_BLOCK + qi - kj
    band = jnp.logical_and(delta >= 0, delta <= A_BLOCK)
    band_first = jnp.logical_and(band, jnp.logical_or(kj >= A_BLOCK, st > 0))
    dn_nt = (((1,), (1,)), ((), ()))
    for g, (win, dil) in enumerate(A_GROUPS):
        q_ref, kc_ref, vc_ref, kp_ref, vp_ref = ins[5 * g:5 * g + 5]
        og_ref, lg_ref = scr[2 * g:2 * g + 2]
        slope = slopes_ref[g * A_HPG + hh]
        bias = -slope * (dil * delta).astype(F32) * LOG2E
        bias_mid = jnp.where(band, bias, -jnp.inf)
        bias_first = jnp.where(band_first, bias, -jnp.inf)

        def two_blocks(prev_ref, cur_ref, r, j):
            if j == 0:
                return jnp.concatenate([prev_ref[_rows(r, A_BLOCK, dil), :],
                                        cur_ref[_rows(r, A_BLOCK, dil), :]], axis=0)
            return cur_ref[_rows(r + (j - 1) * A_BLOCK * dil, 2 * A_BLOCK, dil), :]

        for r in range(dil):
            for j in range(A_ST // (A_BLOCK * dil)):
                row0 = r + j * A_BLOCK * dil
                q = (q_ref[_rows(row0, A_BLOCK, dil), :] * (scale * LOG2E)).astype(BF16)
                k2 = two_blocks(kp_ref, kc_ref, r, j).astype(BF16)
                v2 = two_blocks(vp_ref, vc_ref, r, j).astype(BF16)
                s = lax.dot_general(q, k2, dn_nt, preferred_element_type=F32)
                s = s + (bias_first if j == 0 else bias_mid)
                m = jnp.max(s, axis=-1, keepdims=True)
                p = jnp.exp2(s - m)
                den = jnp.sum(p, axis=-1, keepdims=True)
                acc = jnp.dot(p.astype(BF16), v2, preferred_element_type=F32)
                og_ref[_rows(row0, A_BLOCK, dil), :] = acc / den
                lg_ref[_rows(row0, A_BLOCK, dil), :] = jnp.broadcast_to(m + jnp.log2(den), (A_BLOCK, A_E))
    chunk = 2 * A_BLOCK
    for c in range(A_ST // chunk):
        rs = slice(c * chunk, (c + 1) * chunk)
        ls = [scr[2 * g + 1][rs, :] for g in range(n_g)]
        mx = functools.reduce(jnp.maximum, ls)
        ws = [jnp.exp2(l - mx) for l in ls]
        num = sum(w * scr[2 * g][rs, :] for g, w in enumerate(ws))
        o_ref[rs, :] = num / sum(ws)


def _attn_prompt(slopes, u, *, batch, seq):
    assert seq % A_ST == 0
    u3 = u.reshape(batch, seq, UA_WIDTH)
    in_specs = [pl.BlockSpec(memory_space=pltpu.SMEM)]
    args = [slopes]
    scratch = []
    keeps = tuple(min(win, seq) for win, _ in A_GROUPS)
    assert max(keeps) <= A_ST
    for g, (win, dil) in enumerate(A_GROUPS):
        assert win == A_BLOCK * dil and A_ST % win == 0
        qcb = (QA_OFF + g * A_GW) // A_E
        kcb = (KV_OFF + g * 2 * A_GW) // A_E
        vcb = kcb + A_HPG
        per = A_ST // win

        def cur(cb):
            return pl.BlockSpec((None, A_ST, A_E), lambda b, st, hh, cb=cb: (b, st, cb + hh))

        def prv(cb, win=win, per=per):
            return pl.BlockSpec((None, win, A_E),
                                lambda b, st, hh, cb=cb: (b, jnp.maximum(st * per - 1, 0), cb + hh))

        in_specs += [cur(qcb), cur(kcb), cur(vcb), prv(kcb), prv(vcb)]
        args += [u3] * 5
        scratch += [pltpu.VMEM((A_ST, A_E), F32), pltpu.VMEM((A_ST, A_E), F32)]
    n_st = seq // A_ST
    out_specs = [pl.BlockSpec((None, A_ST, A_E), lambda b, st, hh: (b, st, hh))]
    out_shape = [jax.ShapeDtypeStruct((batch, seq, A_GW), F32)]
    for keep in keeps:
        out_specs.append(pl.BlockSpec((None, keep * SUBLANE, A_E), lambda b, st, hh: (b, 0, 0),
                                      pipeline_mode=pl.Buffered(1)))
        out_shape.append(jax.ShapeDtypeStruct((batch, keep * SUBLANE, A_E), F32))
    outs = pl.pallas_call(
        functools.partial(_attn_prompt_kernel, n_st=n_st, keeps=keeps),
        grid=(batch, n_st, A_HPG),
        in_specs=in_specs,
        out_specs=out_specs,
        out_shape=out_shape,
        scratch_shapes=scratch,
        compiler_params=_cparams(("parallel", "arbitrary", "arbitrary")),
        name="attn_prompt",
    )(*args)
    rows = [kv.reshape(1, batch, keep, 2, A_HPG, A_E) for kv, keep in zip(outs[1:], keeps)]
    return outs[0], rows


def _pick_rows(rows):
    sub = lax.broadcasted_iota(jnp.int32, (SUBLANE, A_E), 0)
    out = jnp.zeros((SUBLANE, A_E), F32)
    for i, r in enumerate(rows):
        out = jnp.where(sub == i, jnp.broadcast_to(r, (SUBLANE, A_E)), out)
    return out


def _attn_sample_kernel(slopes_ref, q_ref, kv_ref, c0_ref, c1_ref, c2_ref, o_ref, *kvo_refs, dec_seq, nseq):
    scale = A_E ** -0.5
    caches = (c0_ref, c1_ref, c2_ref)
    for g, kvo_ref in enumerate(kvo_refs):
        for t in range(nseq * dec_seq):
            kvo_ref[t * SUBLANE:(t + 1) * SUBLANE, :] = _pick_rows(
                [kv_ref[t:t + 1, g * 2 * A_GW + c * A_E:g * 2 * A_GW + (c + 1) * A_E] for c in range(2 * A_HPG)])
    n_keys = A_BLOCK * SUBLANE
    sub = lax.broadcasted_iota(jnp.int32, (SUBLANE, n_keys), 0)
    lane = lax.broadcasted_iota(jnp.int32, (SUBLANE, n_keys), 1)
    tile_row = lane & (SUBLANE - 1)
    m_idx = lane >> 3
    own_k = tile_row == sub
    sub1 = lax.broadcasted_iota(jnp.int32, (SUBLANE, 1), 0)
    dn_nt = (((1,), (1,)), ((), ()))
    slope8s = []
    for g in range(len(A_GROUPS)):
        slope8 = jnp.zeros((SUBLANE, 1), F32)
        for h in range(A_HPG):
            slope8 = jnp.where(sub1 == h, slopes_ref[g * A_HPG + h], slope8)
        slope8s.append(slope8)

    def keys(bb, s, g):
        res = 0 if A_GROUPS[g][1] == 1 else s
        return caches[g][bb, :, res, :, :].reshape(n_keys, A_E).astype(BF16)

    units = [(bb, s, g) for bb in range(nseq) for s in range(dec_seq) for g in range(len(A_GROUPS))]
    q8s = [_pick_rows([q_ref[bb * dec_seq + s:bb * dec_seq + s + 1, g * A_GW + h * A_E:g * A_GW + (h + 1) * A_E]
                       for h in range(A_HPG)]) for bb, s, g in units]
    scs = [lax.dot_general(q8.astype(BF16), keys(*u), dn_nt, preferred_element_type=F32) * scale
           for q8, u in zip(q8s, units)]
    ms, dens, pvs, news = [], [], [], []
    for (bb, s, g), q8, sc in zip(units, q8s, scs):
        win, dil = A_GROUPS[g]
        row = bb * dec_seq + s
        slope8 = slope8s[g]
        if dil == 1:
            dist = (win + s - m_idx).astype(F32)
            valid = jnp.logical_and(own_k, m_idx >= s)
            new_rows = [(bb * dec_seq + t, float(s - t)) for t in range(s + 1)]
        else:
            dist = (dil * (win // dil - m_idx)).astype(F32)
            valid = own_k
            new_rows = [(row, 0.0)]
        sc = jnp.where(valid, sc - slope8 * dist, -jnp.inf)
        m = jnp.max(sc, axis=-1, keepdims=True)
        koff = g * 2 * A_GW
        new = []
        for nrow, ndist in new_rows:
            k8 = _pick_rows([kv_ref[nrow:nrow + 1, koff + h * A_E:koff + (h + 1) * A_E]
                             for h in range(A_HPG)])
            v8 = _pick_rows([kv_ref[nrow:nrow + 1, koff + A_GW + h * A_E:koff + A_GW + (h + 1) * A_E]
                             for h in range(A_HPG)])
            s_n = jnp.sum(k8 * q8, axis=-1, keepdims=True) * scale - slope8 * ndist
            m = jnp.maximum(m, s_n)
            new.append((s_n, v8))
        p = jnp.exp(sc - m)
        ms.append(m)
        dens.append(jnp.sum(p, axis=-1, keepdims=True))
        pvs.append(pltpu.roll(p, A_HPG, 1).astype(BF16))
        news.append(new)
    accs = [jnp.dot(pv, keys(*u), preferred_element_type=F32) for pv, u in zip(pvs, units)]

    out_tiles = [jnp.zeros((nseq * dec_seq, A_E), F32) for _ in range(A_HPG)]
    out_sub = lax.broadcasted_iota(jnp.int32, (nseq * dec_seq, A_E), 0)
    n_g = len(A_GROUPS)
    for ui in range(0, len(units), n_g):
        bb, s, _ = units[ui]
        outs, lses = [], []
        for m, den, acc, new in zip(ms[ui:ui + n_g], dens[ui:ui + n_g], accs[ui:ui + n_g], news[ui:ui + n_g]):
            for s_n, v8 in new:
                p_n = jnp.exp(s_n - m)
                den = den + p_n
                acc = acc + p_n * v8
            outs.append(acc / den)
            lses.append(m + jnp.log(den))
        mx = functools.reduce(jnp.maximum, lses)
        ws = [jnp.exp(l - mx) for l in lses]
        o8 = sum(w * o for w, o in zip(ws, outs)) / sum(ws)
        for h in range(A_HPG):
            out_tiles[h] = jnp.where(out_sub == bb * dec_seq + s,
                                     jnp.broadcast_to(o8[h:h + 1, :], (nseq * dec_seq, A_E)), out_tiles[h])
    for h in range(A_HPG):
        o_ref[:, h * A_E:(h + 1) * A_E] = out_tiles[h]


def _attn_sample(slopes, u, caches, *, batch, dec_seq):
    nseq = min(A_SAMPLE_SEQS, batch)
    assert (nseq * dec_seq) % SUBLANE == 0 and batch % nseq == 0
    views, specs = [], []
    for (win, dil), cache in zip(A_GROUPS, caches):
        wb = cache.shape[1]
        assert wb == win and wb // dil == A_BLOCK and (dil == 1 or dec_seq <= dil)
        views.append(cache.reshape(batch, wb // dil, dil, SUBLANE, A_E))
        res = 1 if dil == 1 else dec_seq
        specs.append(pl.BlockSpec((nseq, A_BLOCK, res, SUBLANE, A_E), lambda i: (i, 0, 0, 0, 0)))
    n_q = len(A_GROUPS) * A_GW
    rows = nseq * dec_seq
    outs = pl.pallas_call(
        functools.partial(_attn_sample_kernel, dec_seq=dec_seq, nseq=nseq),
        grid=(batch // nseq,),
        in_specs=[
            pl.BlockSpec(memory_space=pltpu.SMEM),
            pl.BlockSpec((rows, n_q), lambda i: (i, QA_OFF // n_q)),
            pl.BlockSpec((rows, 2 * n_q), lambda i: (i, KV_OFF // (2 * n_q))),
        ] + specs,
        out_specs=[pl.BlockSpec((rows, A_GW), lambda i: (i, 0))]
        + [pl.BlockSpec((rows * SUBLANE, A_E), lambda i: (i, 0))] * len(A_GROUPS),
        out_shape=[jax.ShapeDtypeStruct((batch * dec_seq, A_GW), F32)]
        + [jax.ShapeDtypeStruct((batch * dec_seq * SUBLANE, A_E), F32)] * len(A_GROUPS),
        compiler_params=_cparams(("parallel",)),
        name="attn_sample",
    )(slopes, u, u, *views)
    return outs[0], [kv.reshape(1, batch, dec_seq, 2, A_HPG, A_E) for kv in outs[1:]]


def _bdot(a, b, dn=(((1,), (0,)), ((), ()))):
    return lax.dot_general(a.astype(BF16), b.astype(BF16), dn, preferred_element_type=F32)


def _gdn_kernel(qkv_ref, z_ref, ba_ref, cw_ref, alog_ref, dtb_ref, nw_ref, conv0_ref, s0_ref,
                o_ref, sout_ref, cb_ref, s_ref, *, tb, cc, nc, bpb, flat):
    n = pl.program_id(1)

    def seq_rows(bi):
        return (slice(bi * tb, (bi + 1) * tb),) if flat else (bi, slice(None))

    @pl.when(n == 0)
    def _():
        s_ref[...] = s0_ref[...]
        cb_ref[:, 0:SUBLANE, :] = conv0_ref[...]

    rows = lax.broadcasted_iota(jnp.int32, (cc, 1), 0)
    live = rows < tb
    ii = lax.broadcasted_iota(jnp.int32, (cc, cc), 0)
    jj = lax.broadcasted_iota(jnp.int32, (cc, cc), 1)
    incl = ii >= jj
    strict = ii > jj
    tri = incl.astype(F32)
    cw = cw_ref[...]
    first = SUBLANE - (G_CONV - 1)
    dn_nt = (((1,), (1,)), ((), ()))
    dn_tn = (((0,), (0,)), ((), ()))
    eye = ii == jj

    qs, ks, kbs, decays, rhss, e_gcs, e_rests, e_ends = [], [], [], [], [], [], [], []
    idx = range(bpb * G_HEADS)
    for bi in range(bpb):
        x = qkv_ref[seq_rows(bi)].astype(F32)
        cb_ref[bi, SUBLANE:SUBLANE + tb, :] = x
        if tb < cc:
            cb_ref[bi, SUBLANE + tb:SUBLANE + cc, :] = jnp.zeros((cc - tb, G_CONV_CH), F32)
        y = cb_ref[bi, SUBLANE:SUBLANE + cc, :] * cw[G_CONV - 1:G_CONV, :]
        for j in range(G_CONV - 2, -1, -1):
            y = y + cb_ref[bi, first + j:first + j + cc, :] * cw[j:j + 1, :]
        if nc > 1:
            cb_ref[bi, 0:SUBLANE, :] = x[tb - SUBLANE:tb, :]
        y = y * jax.nn.sigmoid(y)

        ba = ba_ref[seq_rows(bi)]
        if tb < cc:
            ba = jnp.concatenate([ba, jnp.zeros((cc - tb, LANE), F32)], axis=0)
        beta_t = jnp.where(live, jax.nn.sigmoid(ba), 0.0)
        xs = ba + dtb_ref[...]
        softplus = jnp.maximum(xs, 0.0) + jnp.log(1.0 + jnp.exp(-jnp.abs(xs)))
        g_t = jnp.where(live, -jnp.exp(alog_ref[...]) * softplus, 0.0)
        gc = jnp.dot(tri, g_t, precision=HIGHEST, preferred_element_type=F32)
        gc_t = jnp.concatenate([gc, jnp.zeros((LANE - cc, LANE), F32)], axis=0).T
        gc_last = gc[cc - 1:cc, :]
        e_gc = jnp.exp(gc)
        e_rest = jnp.exp(gc_last - gc)
        e_end = jnp.exp(gc_last)
        for h in range(G_HEADS):
            sl = slice(h * G_DK, (h + 1) * G_DK)
            gl = slice(G_HEADS + h, G_HEADS + h + 1)
            q = y[:, sl]
            k = y[:, G_QK + h * G_DK:G_QK + (h + 1) * G_DK]
            v = y[:, 2 * G_QK + h * G_DV:2 * G_QK + (h + 1) * G_DV]
            q = q * (lax.rsqrt(jnp.sum(q * q, axis=-1, keepdims=True) + NORM_EPS) * (G_DK ** -0.5))
            k = k * lax.rsqrt(jnp.sum(k * k, axis=-1, keepdims=True) + NORM_EPS)
            if tb < cc:
                q = jnp.where(live, q, 0.0)
                k = jnp.where(live, k, 0.0)
                v = jnp.where(live, v, 0.0)
            beta = beta_t[:, h:h + 1]
            kb = k * beta
            qs.append(q)
            ks.append(k)
            kbs.append(kb)
            decays.append(jnp.exp(jnp.where(
                incl, gc[:, gl] - gc_t[G_HEADS + h:G_HEADS + h + 1, 0:cc], -jnp.inf)))
            rhss.append(jnp.concatenate([v * beta, kb * e_gc[:, gl]], axis=1))
            e_gcs.append(e_gc[:, gl])
            e_rests.append(e_rest[:, gl])
            e_ends.append(e_end[:, gl])

    kq = [_bdot(jnp.concatenate([kbs[i], qs[i]], axis=0), ks[i], dn_nt) for i in idx]
    a_low = [jnp.where(strict, kq[i][:cc] * decays[i], 0.0) for i in idx]
    qk = [kq[i][cc:] * decays[i] for i in idx]
    inv = [jnp.where(eye, 1.0, -a_low[i]) for i in idx]
    pw = a_low
    span = 2
    while span < cc:
        pw = [_bdot(pw[i], pw[i]) for i in idx]
        inv = [inv[i] + _bdot(pw[i], inv[i]) for i in idx]
        span *= 2
    rhss = [_bdot(inv[i], rhss[i]) for i in idx]
    s_old = [s_ref[i // G_HEADS, i % G_HEADS] for i in idx]
    ws = [_bdot(jnp.concatenate([rhss[i][:, G_DV:], qs[i] * e_gcs[i]], axis=0), s_old[i]) for i in idx]
    v_new = [rhss[i][:, :G_DV] - ws[i][:cc] for i in idx]
    for i in idx:
        s_ref[i // G_HEADS, i % G_HEADS] = (
            s_old[i] * e_ends[i] + _bdot(ks[i] * e_rests[i], v_new[i], dn_tn))
    o = [ws[i][cc:] + _bdot(qk[i], v_new[i]) for i in idx]
    for i in idx:
        bi, h = i // G_HEADS, i % G_HEADS
        sl = slice(h * G_DV, (h + 1) * G_DV)
        o_h = o[i] * lax.rsqrt(jnp.mean(o[i] * o[i], axis=-1, keepdims=True) + NORM_EPS) * nw_ref[...]
        zz = z_ref[seq_rows(bi) + (sl,)].astype(F32)
        o_ref[seq_rows(bi) + (sl,)] = o_h[:tb] * (zz * jax.nn.sigmoid(zz))

    @pl.when(n == nc - 1)
    def _():
        sout_ref[...] = s_ref[...]


def _gdn(u, ba, conv0, s0, conv_w, alog_t, dtb_t, norm_w, *, batch, seq, bpb):
    cc = G_CHUNK if seq >= G_CHUNK else SUBLANE
    tb = min(seq, cc)
    nc = seq // tb
    assert nc * tb == seq and (nc == 1 or tb == cc) and batch % bpb == 0
    flat = nc == 1
    if flat:
        assert (bpb * tb) % SUBLANE == 0
        uv, bav, o_rows = u, ba, batch * seq

        def rows(width, col):
            return pl.BlockSpec((bpb * tb, width), lambda b, n: (b, col))
    else:
        uv, bav, o_rows = u.reshape(batch, seq, UB_WIDTH), ba.reshape(batch, seq, LANE), batch

        def rows(width, col):
            return pl.BlockSpec((bpb, tb, width), lambda b, n: (b, n, col))
    o_shape = (o_rows, G_QK) if flat else (batch, seq, G_QK)
    conv0p = jnp.concatenate(
        [jnp.zeros((batch, SUBLANE - (G_CONV - 1), G_CONV_CH), F32), conv0.astype(F32)], axis=1)
    const2 = lambda b, n: (0, 0)
    o, s_new = pl.pallas_call(
        functools.partial(_gdn_kernel, tb=tb, cc=cc, nc=nc, bpb=bpb, flat=flat),
        grid=(batch // bpb, nc),
        in_specs=[
            rows(G_CONV_CH, QKVB_OFF // G_CONV_CH),
            rows(G_QK, Z_OFF // G_QK),
            rows(LANE, 0),
            pl.BlockSpec((G_CONV, G_CONV_CH), const2),
            pl.BlockSpec((1, LANE), const2),
            pl.BlockSpec((1, LANE), const2),
            pl.BlockSpec((1, G_DV), const2),
            pl.BlockSpec((bpb, SUBLANE, G_CONV_CH), lambda b, n: (b, 0, 0)),
            pl.BlockSpec((bpb, G_HEADS, G_DK, G_DV), lambda b, n: (b, 0, 0, 0)),
        ],
        out_specs=[
            rows(G_QK, 0),
            pl.BlockSpec((bpb, G_HEADS, G_DK, G_DV), lambda b, n: (b, 0, 0, 0)),
        ],
        out_shape=[
            jax.ShapeDtypeStruct(o_shape, F32),
            jax.ShapeDtypeStruct((batch, G_HEADS, G_DK, G_DV), F32),
        ],
        scratch_shapes=[
            pltpu.VMEM((bpb, SUBLANE + cc, G_CONV_CH), F32),
            pltpu.VMEM((bpb, G_HEADS, G_DK, G_DV), F32),
        ],
        compiler_params=_cparams(("parallel", "arbitrary")),
        name="gdn",
    )(uv, uv, bav, conv_w, alog_t, dtb_t, norm_w.reshape(1, G_DV), conv0p, s0)
    return o.reshape(batch * seq, G_QK), s_new


def _mix_kernel(x_ref, oa_ref, ob_ref, ga_ref, gb_ref, wa_ref, wb_ref, wo_ref, o_ref):
    pa = jnp.dot(oa_ref[...].astype(BF16), wa_ref[...], preferred_element_type=F32)
    pb = jnp.dot(ob_ref[...].astype(BF16), wb_ref[...], preferred_element_type=F32)
    merged = (jax.nn.sigmoid(ga_ref[...].astype(F32)) * pa
              + jax.nn.sigmoid(gb_ref[...].astype(F32)) * pb)
    o_ref[...] = x_ref[...] + jnp.dot(merged.astype(BF16), wo_ref[...], preferred_element_type=F32)


def _mix(x, oa, ob, u, wa, wb, wo, *, tm):
    n, d = x.shape
    const = lambda i: (0, 0)
    return pl.pallas_call(
        _mix_kernel,
        grid=(n // tm,),
        in_specs=[
            pl.BlockSpec((tm, d), lambda i: (i, 0)),
            pl.BlockSpec((tm, A_GW), lambda i: (i, 0)),
            pl.BlockSpec((tm, G_QK), lambda i: (i, 0)),
            pl.BlockSpec((tm, d), lambda i: (i, GATE_OFF // d)),
            pl.BlockSpec((tm, d), lambda i: (i, GATE_OFF // d + 1)),
            pl.BlockSpec((A_GW, d), const),
            pl.BlockSpec((G_QK, d), const),
            pl.BlockSpec((d, d), const),
        ],
        out_specs=pl.BlockSpec((tm, d), lambda i: (i, 0)),
        out_shape=jax.ShapeDtypeStruct((n, d), F32),
        compiler_params=_cparams(("parallel",)),
        name="mix",
    )(x, oa, ob, u, u, wa, wb, wo)


def _w_in_sources():
    aw = len(A_GROUPS) * A_GW
    src_q, src_k, src_v = 0, aw, 2 * aw
    src_qkvb = 3 * aw
    src_z = src_qkvb + G_CONV_CH
    src_ba = src_z + G_QK
    src_gate = src_ba + 2 * G_HEADS
    blocks = [src_qkvb + W_CB * t for t in range(G_CONV_CH // W_CB)]
    blocks += [src_z + W_CB * t for t in range(G_QK // W_CB)]
    blocks += [src_gate + W_CB * t for t in range(2 * D_MODEL // W_CB)]
    for g in range(len(A_GROUPS)):
        blocks += [src_k + g * A_GW, src_v + g * A_GW]
    blocks += [src_q + W_CB * t for t in range(aw // W_CB)]
    assert len(blocks) * W_CB == U_WIDTH and A_GW == W_CB
    assert all(b % SUBLANE == 0 for b in blocks)
    return blocks, src_ba


def _permute_cast_kernel(src_ref, w_ref, o_ref):
    del src_ref
    o_ref[...] = w_ref[...].astype(BF16)


def _permute_w_in(wt):
    d = wt.shape[1]
    blocks, _ = _w_in_sources()
    return pl.pallas_call(
        _permute_cast_kernel,
        grid_spec=pltpu.PrefetchScalarGridSpec(
            num_scalar_prefetch=1,
            grid=(len(blocks),),
            in_specs=[pl.BlockSpec((pl.Element(W_CB), pl.Element(d)),
                                   lambda j, src: (pl.multiple_of(src[j], SUBLANE), 0))],
            out_specs=pl.BlockSpec((W_CB, d), lambda j, src: (j, 0)),
        ),
        out_shape=jax.ShapeDtypeStruct((U_WIDTH, d), BF16),
        compiler_params=_cparams(("parallel",)),
        name="permute_w_in",
    )(jnp.asarray(blocks, jnp.int32), wt)


def _prep_weights(w_in, gdn_a_log, gdn_dt_bias):
    wt = w_in.T
    w_main = _permute_w_in(wt)
    src_ba = _w_in_sources()[1]
    n_ba = 2 * G_HEADS
    ba = wt[src_ba:src_ba + n_ba, :]
    ba_hi = ba.astype(BF16)
    ba_lo = (ba - ba_hi.astype(F32)).astype(BF16)
    w_ba = jnp.concatenate([ba_hi, ba_lo, jnp.zeros((MXU_COLS - 2 * n_ba, wt.shape[1]), BF16)], axis=0)
    pad = (G_HEADS, LANE - n_ba)
    alog_t = jnp.pad(gdn_a_log.astype(F32), pad).reshape(1, LANE)
    dtb_t = jnp.pad(gdn_dt_bias.astype(F32), pad).reshape(1, LANE)
    return w_main, w_ba, alog_t, dtb_t


def _dense_tiles(n_tokens):
    if n_tokens >= 2048:
        return dict(tm=2048, tf=256, tn=768, tm_mix=1024)
    return dict(tm=n_tokens, tf=256, tn=1536, tm_mix=n_tokens)


def _layer(x, batch, seq, conv0, s0, attn_fn, p, *, gdn_bpb, b_dtype):
    t = _dense_tiles(x.shape[0])
    tm = t["tm"]
    gdn_bpb = min(gdn_bpb, batch)
    tail = b_dtype != F32
    x1 = _ffn(x, p["norm_ffn1"], p["w_ffn1_gu"], p["w_ffn1_down"], tm=tm, tf=t["tf"])
    ub, ua, ba, *rest = _inproj(x1, p["norm_mix"], p["w_main"], p["w_ba"],
                                tm=tm, tn=t["tn"], b_dtype=b_dtype, tail=tail)
    oa, rows = attn_fn(ua)
    ob, s_new = _gdn(ub, ba, conv0, s0, p["conv_w"], p["alog_t"], p["dtb_t"], p["gdn_norm"],
                     batch=batch, seq=seq, bpb=gdn_bpb)
    x2 = _mix(x1, oa.reshape(batch * seq, A_GW), ob.reshape(batch * seq, G_QK), ub,
              p["w_proj_a"], p["w_proj_b"], p["w_out"], tm=t["tm_mix"])
    y = _ffn(x2, p["norm_ffn2"], p["w_ffn2_gu"], p["w_ffn2_down"], p["norm_out"], tm=tm, tf=t["tf"])
    if tail:
        assert seq % tm == 0 and seq >= SUBLANE
        last = rest[0].reshape(batch, seq // tm, SUBLANE, G_CONV_CH)[:, -1]
        conv_new = last[:, SUBLANE - (G_CONV - 1):, :][None]
    else:
        qkvb = ub[:, QKVB_OFF:QKVB_OFF + G_CONV_CH].reshape(batch, seq, G_CONV_CH)
        conv_new = qkvb[:, seq - (G_CONV - 1):, :][None]
    return y.reshape(batch, seq, D_MODEL), rows, conv_new, s_new[None]


def kernel(x_prompt, x_sample, cache_kv_w128, cache_kv_w512, cache_kv_w2048, state_conv, state_ssm,
           norm_ffn1, w_ffn1_gu, w_ffn1_down, norm_mix, w_in, conv_w, gdn_a_log, gdn_dt_bias, gdn_norm,
           w_proj_a, w_proj_b, w_out, norm_ffn2, w_ffn2_gu, w_ffn2_down, norm_out):
    assert w_in.shape[0] == 1, "single layer"
    n_heads = len(A_GROUPS) * A_HPG
    slopes = jnp.exp2(-8.0 * jnp.arange(1, n_heads + 1, dtype=F32) / n_heads)
    w_main, w_ba, alog_t, dtb_t = _prep_weights(w_in[0], gdn_a_log[0], gdn_dt_bias[0])
    p = dict(
        norm_ffn1=norm_ffn1[0], w_ffn1_gu=w_ffn1_gu[0], w_ffn1_down=w_ffn1_down[0],
        norm_mix=norm_mix[0], w_main=w_main, w_ba=w_ba, conv_w=conv_w[0],
        alog_t=alog_t, dtb_t=dtb_t, gdn_norm=gdn_norm[0],
        w_proj_a=w_proj_a[0].astype(BF16), w_proj_b=w_proj_b[0].astype(BF16), w_out=w_out[0].astype(BF16),
        norm_ffn2=norm_ffn2[0], w_ffn2_gu=w_ffn2_gu[0], w_ffn2_down=w_ffn2_down[0],
        norm_out=norm_out,
    )
    bp, tp, d = x_prompt.shape
    bs, ts, _ = x_sample.shape

    conv0_p = jnp.zeros((bp, G_CONV - 1, G_CONV_CH), F32)
    ssm0_p = jnp.zeros((bp, G_HEADS, G_DK, G_DV), F32)
    yp, rows_p, conv_p, ssm_p = _layer(
        x_prompt.reshape(bp * tp, d), bp, tp, conv0_p, ssm0_p,
        functools.partial(_attn_prompt, slopes, batch=bp, seq=tp), p, gdn_bpb=4, b_dtype=BF16)

    caches = (cache_kv_w128[0], cache_kv_w512[0], cache_kv_w2048[0])
    ys, rows_s, conv_s, ssm_s = _layer(
        x_sample.reshape(bs * ts, d), bs, ts, state_conv[0], state_ssm[0],
        lambda u: _attn_sample(slopes, u, caches, batch=bs, dec_seq=ts), p, gdn_bpb=16, b_dtype=F32)

    return (yp, ys, rows_p[0], rows_p[1], rows_p[2], conv_p, ssm_p,
            rows_s[0], rows_s[1], rows_s[2], conv_s, ssm_s)
```

```python
import functools

import jax
import jax.numpy as jnp
from jax import lax
from jax.experimental import pallas as pl
from jax.experimental.pallas import tpu as pltpu

F32 = jnp.float32
BF16 = jnp.bfloat16
HIGHEST = lax.Precision.HIGHEST

D_MODEL = 1024
D_FF = 2816
NORM_EPS = 1e-6
A_GROUPS = ((128, 1), (512, 4), (2048, 16))
A_HPG = 4
A_E = 128
A_BLOCK = 128
A_GW = A_HPG * A_E
A_ST = 2048
A_SAMPLE_SEQS = 4
G_HEADS = 8
G_DK = 128
G_DV = 128
G_QK = G_HEADS * G_DK
G_CONV_CH = 3 * G_QK
G_CONV = 4
G_CHUNK = 64

QKVB_OFF = 0
Z_OFF = 3072
GATE_OFF = 4096
UB_WIDTH = 6144
KV_OFF = 0
QA_OFF = 3072
UA_WIDTH = 4608
U_WIDTH = UB_WIDTH + UA_WIDTH
W_CB = 512

VMEM_LIMIT = 56 * 1024 * 1024
LANE = 128
SUBLANE = 8
MXU_COLS = 256
LOG2E = 1.4426950408889634


def _cparams(sem):
    return pltpu.CompilerParams(dimension_semantics=sem, vmem_limit_bytes=VMEM_LIMIT)


def _rms_rows(x, w):
    return x * lax.rsqrt(jnp.mean(x * x, axis=-1, keepdims=True) + NORM_EPS) * w


def _ffn_kernel(x_ref, nw_ref, wg_ref, wu_ref, wd_ref, *rest, n_ff, final_norm):
    if final_norm:
        onw_ref, o_ref, hb_ref = rest
    else:
        o_ref, hb_ref = rest
    j = pl.program_id(1)

    @pl.when(j == 0)
    def _():
        hb_ref[...] = _rms_rows(x_ref[...], nw_ref[...]).astype(BF16)
        o_ref[...] = jnp.zeros_like(o_ref)

    hb = hb_ref[...]
    g = jnp.dot(hb, wg_ref[...].astype(BF16), preferred_element_type=F32)
    u = jnp.dot(hb, wu_ref[...].astype(BF16), preferred_element_type=F32)
    a = (g * jax.nn.sigmoid(g) * u).astype(BF16)
    o_ref[...] += jnp.dot(a, wd_ref[...].astype(BF16), preferred_element_type=F32)

    @pl.when(j == n_ff - 1)
    def _():
        y = x_ref[...] + 0.5 * o_ref[...]
        if final_norm:
            y = _rms_rows(y, onw_ref[...])
        o_ref[...] = y


def _ffn(x, norm_w, w_gu, w_down, out_norm_w=None, *, tm, tf):
    n, d = x.shape
    n_ff = D_FF // tf
    final_norm = out_norm_w is not None
    in_specs = [
        pl.BlockSpec((tm, d), lambda i, j: (i, 0)),
        pl.BlockSpec((1, d), lambda i, j: (0, 0)),
        pl.BlockSpec((d, tf), lambda i, j: (0, j)),
        pl.BlockSpec((d, tf), lambda i, j: (0, j + n_ff)),
        pl.BlockSpec((tf, d), lambda i, j: (j, 0)),
    ]
    args = [x, norm_w.reshape(1, d), w_gu, w_gu, w_down]
    if final_norm:
        in_specs.append(pl.BlockSpec((1, d), lambda i, j: (0, 0)))
        args.append(out_norm_w.reshape(1, d))
    return pl.pallas_call(
        functools.partial(_ffn_kernel, n_ff=n_ff, final_norm=final_norm),
        grid=(n // tm, n_ff),
        in_specs=in_specs,
        out_specs=pl.BlockSpec((tm, d), lambda i, j: (i, 0)),
        out_shape=jax.ShapeDtypeStruct((n, d), F32),
        scratch_shapes=[pltpu.VMEM((tm, d), BF16)],
        compiler_params=_cparams(("parallel", "arbitrary")),
        name="ffn_final" if final_norm else "ffn",
    )(*args)


_DN_NT = (((1,), (1,)), ((), ()))


def _inproj_kernel(x_ref, nw_ref, w_ref, wba_ref, ub_ref, ua_ref, ba_ref, *rest, nb, nq, tail):
    if tail:
        tail_ref, hb_ref = rest
    else:
        (hb_ref,) = rest
    j = pl.program_id(1)

    @pl.when(j == 0)
    def _():
        h = _rms_rows(x_ref[...], nw_ref[...])
        hb = h.astype(BF16)
        hl = (h - hb.astype(F32)).astype(BF16)
        hb_ref[...] = hb
        n_ba = 2 * G_HEADS
        r_hi = lax.dot_general(hb, wba_ref[...], _DN_NT, preferred_element_type=F32)[:, :LANE]
        r_lo = lax.dot_general(hl, wba_ref[...], _DN_NT, preferred_element_type=F32)[:, :LANE]
        lane = lax.broadcasted_iota(jnp.int32, r_hi.shape, 1)
        ba_ref[...] = jnp.where(lane < n_ba, r_hi + pltpu.roll(r_hi, LANE - n_ba, 1) + r_lo, 0.0)

    def project(rows):
        return lax.dot_general(hb_ref[rows, :], w_ref[...], _DN_NT, preferred_element_type=F32)

    @pl.when(j < nb)
    def _():
        ub_ref[...] = project(slice(None)).astype(ub_ref.dtype)

    @pl.when(j >= nb)
    def _():
        ua_ref[...] = project(slice(None))

    if tail:
        @pl.when(j < nq)
        def _():
            tm = hb_ref.shape[0]
            tail_ref[...] = project(slice(tm - 2 * SUBLANE, tm))[SUBLANE:, :]


def _inproj(x, norm_w, w_main, w_ba, *, tm, tn, b_dtype, tail):
    n, d = x.shape
    nb, nq = UB_WIDTH // tn, G_CONV_CH // tn
    assert nb * tn == UB_WIDTH and nq * tn == G_CONV_CH and U_WIDTH % tn == 0 and QKVB_OFF == 0
    out_specs = [
        pl.BlockSpec((tm, tn), lambda i, j: (i, jnp.minimum(j, nb - 1))),
        pl.BlockSpec((tm, tn), lambda i, j: (i, jnp.maximum(j - nb, 0))),
        pl.BlockSpec((tm, LANE), lambda i, j: (i, 0)),
    ]
    out_shape = [
        jax.ShapeDtypeStruct((n, UB_WIDTH), b_dtype),
        jax.ShapeDtypeStruct((n, UA_WIDTH), F32),
        jax.ShapeDtypeStruct((n, LANE), F32),
    ]
    if tail:
        out_specs.append(pl.BlockSpec((SUBLANE, tn), lambda i, j: (i, jnp.minimum(j, nq - 1))))
        out_shape.append(jax.ShapeDtypeStruct((n // tm * SUBLANE, G_CONV_CH), F32))
    return pl.pallas_call(
        functools.partial(_inproj_kernel, nb=nb, nq=nq, tail=tail),
        grid=(n // tm, U_WIDTH // tn),
        in_specs=[
            pl.BlockSpec((tm, d), lambda i, j: (i, 0)),
            pl.BlockSpec((1, d), lambda i, j: (0, 0)),
            pl.BlockSpec((tn, d), lambda i, j: (j, 0)),
            pl.BlockSpec((MXU_COLS, d), lambda i, j: (0, 0)),
        ],
        out_specs=out_specs,
        out_shape=out_shape,
        scratch_shapes=[pltpu.VMEM((tm, d), BF16)],
        compiler_params=_cparams(("parallel", "arbitrary")),
        name="inproj",
    )(x, norm_w.reshape(1, d), w_main, w_ba)


def _rows(start, size, stride):
    return pl.ds(start, size) if stride == 1 else pl.ds(start, size, stride=stride)


def _attn_prompt_kernel(slopes_ref, *refs, n_st, keeps):
    n_g = len(A_GROUPS)
    ins, o_ref = refs[:5 * n_g], refs[5 * n_g]
    kv_refs, scr = refs[5 * n_g + 1:6 * n_g + 1], refs[6 * n_g + 1:]
    st = pl.program_id(1)
    hh = pl.program_id(2)

    @pl.when(st == n_st - 1)
    def _():
        for g in range(n_g):
            keep = keeps[g]
            kc_ref, vc_ref = ins[5 * g + 1], ins[5 * g + 2]
            kv_refs[g][pl.ds(hh, keep, stride=SUBLANE), :] = kc_ref[A_ST - keep:, :]
            kv_refs[g][pl.ds(A_HPG + hh, keep, stride=SUBLANE), :] = vc_ref[A_ST - keep:, :]

    scale = A_E ** -0.5
    qi = lax.broadcasted_iota(jnp.int32, (A_BLOCK, 2 * A_BLOCK), 0)
    kj = lax.broadcasted_iota(jnp.int32, (A_BLOCK, 2 * A_BLOCK), 1)
    delta = A_BLOCK + qi - kj
    band = jnp.logical_and(delta >= 0, delta <= A_BLOCK)
    band_first = jnp.logical_and(band, jnp.logical_or(kj >= A_BLOCK, st > 0))
    dn_nt = (((1,), (1,)), ((), ()))
    for g, (win, dil) in enumerate(A_GROUPS):
        q_ref, kc_ref, vc_ref, kp_ref, vp_ref = ins[5 * g:5 * g + 5]
        og_ref, lg_ref = scr[2 * g:2 * g + 2]
        slope = slopes_ref[g * A_HPG + hh]
        bias = -slope * (dil * delta).astype(F32) * LOG2E
        bias_mid = jnp.where(band, bias, -jnp.inf)
        bias_first = jnp.where(band_first, bias, -jnp.inf)

        for r in range(dil):
            k_prev = kp_ref[_rows(r, A_BLOCK, dil), :].astype(BF16)
            v_prev = vp_ref[_rows(r, A_BLOCK, dil), :].astype(BF16)
            for j in range(A_ST // (A_BLOCK * dil)):
                row0 = r + j * A_BLOCK * dil
                k_cur = kc_ref[_rows(row0, A_BLOCK, dil), :].astype(BF16)
                v_cur = vc_ref[_rows(row0, A_BLOCK, dil), :].astype(BF16)
                k2 = jnp.concatenate([k_prev, k_cur], axis=0)
                v2 = jnp.concatenate([v_prev, v_cur], axis=0)
                q = (q_ref[_rows(row0, A_BLOCK, dil), :] * (scale * LOG2E)).astype(BF16)
                s = lax.dot_general(q, k2, dn_nt, preferred_element_type=F32)
                s = s + (bias_first if j == 0 else bias_mid)
                m = jnp.max(s, axis=-1, keepdims=True)
                p = jnp.exp2(s - m)
                den = jnp.sum(p, axis=-1, keepdims=True)
                acc = jnp.dot(p.astype(BF16), v2, preferred_element_type=F32)
                og_ref[_rows(row0, A_BLOCK, dil), :] = acc / den
                lg_ref[_rows(row0, A_BLOCK, dil), :] = jnp.broadcast_to(m + jnp.log2(den), (A_BLOCK, A_E))
                k_prev, v_prev = k_cur, v_cur
    chunk = 2 * A_BLOCK
    for c in range(A_ST // chunk):
        rs = slice(c * chunk, (c + 1) * chunk)
        ls = [scr[2 * g + 1][rs, :] for g in range(n_g)]
        mx = functools.reduce(jnp.maximum, ls)
        ws = [jnp.exp2(l - mx) for l in ls]
        num = sum(w * scr[2 * g][rs, :] for g, w in enumerate(ws))
        o_ref[rs, :] = num / sum(ws)


def _attn_prompt(slopes, u, *, batch, seq):
    assert seq % A_ST == 0
    u3 = u.reshape(batch, seq, UA_WIDTH)
    in_specs = [pl.BlockSpec(memory_space=pltpu.SMEM)]
    args = [slopes]
    scratch = []
    keeps = tuple(min(win, seq) for win, _ in A_GROUPS)
    assert max(keeps) <= A_ST
    for g, (win, dil) in enumerate(A_GROUPS):
        assert win == A_BLOCK * dil and A_ST % win == 0
        qcb = (QA_OFF + g * A_GW) // A_E
        kcb = (KV_OFF + g * 2 * A_GW) // A_E
        vcb = kcb + A_HPG
        per = A_ST // win

        def cur(cb):
            return pl.BlockSpec((None, A_ST, A_E), lambda b, st, hh, cb=cb: (b, st, cb + hh))

        def prv(cb, win=win, per=per):
            return pl.BlockSpec((None, win, A_E),
                                lambda b, st, hh, cb=cb: (b, jnp.maximum(st * per - 1, 0), cb + hh))

        in_specs += [cur(qcb), cur(kcb), cur(vcb), prv(kcb), prv(vcb)]
        args += [u3] * 5
        scratch += [pltpu.VMEM((A_ST, A_E), F32), pltpu.VMEM((A_ST, A_E), F32)]
    n_st = seq // A_ST
    out_specs = [pl.BlockSpec((None, A_ST, A_E), lambda b, st, hh: (b, st, hh))]
    out_shape = [jax.ShapeDtypeStruct((batch, seq, A_GW), F32)]
    for keep in keeps:
        out_specs.append(pl.BlockSpec((None, keep * SUBLANE, A_E), lambda b, st, hh: (b, 0, 0),
                                      pipeline_mode=pl.Buffered(1)))
        out_shape.append(jax.ShapeDtypeStruct((batch, keep * SUBLANE, A_E), F32))
    outs = pl.pallas_call(
        functools.partial(_attn_prompt_kernel, n_st=n_st, keeps=keeps),
        grid=(batch, n_st, A_HPG),
        in_specs=in_specs,
        out_specs=out_specs,
        out_shape=out_shape,
        scratch_shapes=scratch,
        compiler_params=_cparams(("parallel", "arbitrary", "arbitrary")),
        name="attn_prompt",
    )(*args)
    rows = [kv.reshape(1, batch, keep, 2, A_HPG, A_E) for kv, keep in zip(outs[1:], keeps)]
    return outs[0], rows


def _pick_rows(rows):
    sub = lax.broadcasted_iota(jnp.int32, (SUBLANE, A_E), 0)
    out = jnp.zeros((SUBLANE, A_E), F32)
    for i, r in enumerate(rows):
        out = jnp.where(sub == i, jnp.broadcast_to(r, (SUBLANE, A_E)), out)
    return out


def _attn_sample_kernel(slopes_ref, q_ref, kv_ref, c0_ref, c1_ref, c2_ref, o_ref, *kvo_refs, dec_seq, nseq):
    scale = A_E ** -0.5
    caches = (c0_ref, c1_ref, c2_ref)
    for g, kvo_ref in enumerate(kvo_refs):
        for t in range(nseq * dec_seq):
            kvo_ref[t * SUBLANE:(t + 1) * SUBLANE, :] = _pick_rows(
                [kv_ref[t:t + 1, g * 2 * A_GW + c * A_E:g * 2 * A_GW + (c + 1) * A_E] for c in range(2 * A_HPG)])
    n_keys = A_BLOCK * SUBLANE
    sub = lax.broadcasted_iota(jnp.int32, (SUBLANE, n_keys), 0)
    lane = lax.broadcasted_iota(jnp.int32, (SUBLANE, n_keys), 1)
    tile_row = lane & (SUBLANE - 1)
    m_idx = lane >> 3
    own_k = tile_row == sub
    sub1 = lax.broadcasted_iota(jnp.int32, (SUBLANE, 1), 0)
    dn_nt = (((1,), (1,)), ((), ()))
    slope8s = []
    for g in range(len(A_GROUPS)):
        slope8 = jnp.zeros((SUBLANE, 1), F32)
        for h in range(A_HPG):
            slope8 = jnp.where(sub1 == h, slopes_ref[g * A_HPG + h], slope8)
        slope8s.append(slope8)

    def keys(bb, s, g):
        res = 0 if A_GROUPS[g][1] == 1 else s
        return caches[g][bb, :, res, :, :].reshape(n_keys, A_E).astype(BF16)

    units = [(bb, s, g) for bb in range(nseq) for s in range(dec_seq) for g in range(len(A_GROUPS))]
    q8s = [_pick_rows([q_ref[bb * dec_seq + s:bb * dec_seq + s + 1, g * A_GW + h * A_E:g * A_GW + (h + 1) * A_E]
                       for h in range(A_HPG)]) for bb, s, g in units]
    scs = [lax.dot_general(q8.astype(BF16), keys(*u), dn_nt, preferred_element_type=F32) * scale
           for q8, u in zip(q8s, units)]
    ms, dens, pvs, news = [], [], [], []
    for (bb, s, g), q8, sc in zip(units, q8s, scs):
        win, dil = A_GROUPS[g]
        row = bb * dec_seq + s
        slope8 = slope8s[g]
        if dil == 1:
            dist = (win + s - m_idx).astype(F32)
            valid = jnp.logical_and(own_k, m_idx >= s)
            new_rows = [(bb * dec_seq + t, float(s - t)) for t in range(s + 1)]
        else:
            dist = (dil * (win // dil - m_idx)).astype(F32)
            valid = own_k
            new_rows = [(row, 0.0)]
        sc = jnp.where(valid, sc - slope8 * dist, -jnp.inf)
        m = jnp.max(sc, axis=-1, keepdims=True)
        koff = g * 2 * A_GW
        new = []
        for nrow, ndist in new_rows:
            k8 = _pick_rows([kv_ref[nrow:nrow + 1, koff + h * A_E:koff + (h + 1) * A_E]
                             for h in range(A_HPG)])
            v8 = _pick_rows([kv_ref[nrow:nrow + 1, koff + A_GW + h * A_E:koff + A_GW + (h + 1) * A_E]
                             for h in range(A_HPG)])
            s_n = jnp.sum(k8 * q8, axis=-1, keepdims=True) * scale - slope8 * ndist
            m = jnp.maximum(m, s_n)
            new.append((s_n, v8))
        p = jnp.exp(sc - m)
        ms.append(m)
        dens.append(jnp.sum(p, axis=-1, keepdims=True))
        pvs.append(pltpu.roll(p, A_HPG, 1).astype(BF16))
        news.append(new)
    accs = [jnp.dot(pv, keys(*u), preferred_element_type=F32) for pv, u in zip(pvs, units)]

    out_tiles = [jnp.zeros((nseq * dec_seq, A_E), F32) for _ in range(A_HPG)]
    out_sub = lax.broadcasted_iota(jnp.int32, (nseq * dec_seq, A_E), 0)
    n_g = len(A_GROUPS)
    for ui in range(0, len(units), n_g):
        bb, s, _ = units[ui]
        outs, lses = [], []
        for m, den, acc, new in zip(ms[ui:ui + n_g], dens[ui:ui + n_g], accs[ui:ui + n_g], news[ui:ui + n_g]):
            for s_n, v8 in new:
                p_n = jnp.exp(s_n - m)
                den = den + p_n
                acc = acc + p_n * v8
            outs.append(acc / den)
            lses.append(m + jnp.log(den))
        mx = functools.reduce(jnp.maximum, lses)
        ws = [jnp.exp(l - mx) for l in lses]
        o8 = sum(w * o for w, o in zip(ws, outs)) / sum(ws)
        for h in range(A_HPG):
            out_tiles[h] = jnp.where(out_sub == bb * dec_seq + s,
                                     jnp.broadcast_to(o8[h:h + 1, :], (nseq * dec_seq, A_E)), out_tiles[h])
    for h in range(A_HPG):
        o_ref[:, h * A_E:(h + 1) * A_E] = out_tiles[h]


def _attn_sample(slopes, u, caches, *, batch, dec_seq):
    nseq = min(A_SAMPLE_SEQS, batch)
    assert (nseq * dec_seq) % SUBLANE == 0 and batch % nseq == 0
    views, specs = [], []
    for (win, dil), cache in zip(A_GROUPS, caches):
        wb = cache.shape[1]
        assert wb == win and wb // dil == A_BLOCK and (dil == 1 or dec_seq <= dil)
        views.append(cache.reshape(batch, wb // dil, dil, SUBLANE, A_E))
        res = 1 if dil == 1 else dec_seq
        specs.append(pl.BlockSpec((nseq, A_BLOCK, res, SUBLANE, A_E), lambda i: (i, 0, 0, 0, 0)))
    n_q = len(A_GROUPS) * A_GW
    rows = nseq * dec_seq
    outs = pl.pallas_call(
        functools.partial(_attn_sample_kernel, dec_seq=dec_seq, nseq=nseq),
        grid=(batch // nseq,),
        in_specs=[
            pl.BlockSpec(memory_space=pltpu.SMEM),
            pl.BlockSpec((rows, n_q), lambda i: (i, QA_OFF // n_q)),
            pl.BlockSpec((rows, 2 * n_q), lambda i: (i, KV_OFF // (2 * n_q))),
        ] + specs,
        out_specs=[pl.BlockSpec((rows, A_GW), lambda i: (i, 0))]
        + [pl.BlockSpec((rows * SUBLANE, A_E), lambda i: (i, 0))] * len(A_GROUPS),
        out_shape=[jax.ShapeDtypeStruct((batch * dec_seq, A_GW), F32)]
        + [jax.ShapeDtypeStruct((batch * dec_seq * SUBLANE, A_E), F32)] * len(A_GROUPS),
        compiler_params=_cparams(("parallel",)),
        name="attn_sample",
    )(slopes, u, u, *views)
    return outs[0], [kv.reshape(1, batch, dec_seq, 2, A_HPG, A_E) for kv in outs[1:]]


def _bdot(a, b, dn=(((1,), (0,)), ((), ()))):
    return lax.dot_general(a.astype(BF16), b.astype(BF16), dn, preferred_element_type=F32)


def _gdn_kernel(qkv_ref, z_ref, ba_ref, cw_ref, alog_ref, dtb_ref, nw_ref, conv0_ref, s0_ref,
                o_ref, sout_ref, cb_ref, s_ref, *, tb, cc, nc, bpb, flat):
    n = pl.program_id(1)

    def seq_rows(bi):
        return (slice(bi * tb, (bi + 1) * tb),) if flat else (bi, slice(None))

    @pl.when(n == 0)
    def _():
        s_ref[...] = s0_ref[...]
        cb_ref[:, 0:SUBLANE, :] = conv0_ref[...]

    rows = lax.broadcasted_iota(jnp.int32, (cc, 1), 0)
    live = rows < tb
    ii = lax.broadcasted_iota(jnp.int32, (cc, cc), 0)
    jj = lax.broadcasted_iota(jnp.int32, (cc, cc), 1)
    incl = ii >= jj
    strict = ii > jj
    tri = incl.astype(F32)
    cw = cw_ref[...]
    first = SUBLANE - (G_CONV - 1)
    dn_nt = (((1,), (1,)), ((), ()))
    dn_tn = (((0,), (0,)), ((), ()))
    eye = ii == jj

    qs, ks, kbs, decays, rhss, e_gcs, e_rests, e_ends = [], [], [], [], [], [], [], []
    idx = range(bpb * G_HEADS)
    for bi in range(bpb):
        x = qkv_ref[seq_rows(bi)].astype(F32)
        cb_ref[bi, SUBLANE:SUBLANE + tb, :] = x
        if tb < cc:
            cb_ref[bi, SUBLANE + tb:SUBLANE + cc, :] = jnp.zeros((cc - tb, G_CONV_CH), F32)
        y = cb_ref[bi, SUBLANE:SUBLANE + cc, :] * cw[G_CONV - 1:G_CONV, :]
        for j in range(G_CONV - 2, -1, -1):
            y = y + cb_ref[bi, first + j:first + j + cc, :] * cw[j:j + 1, :]
        if nc > 1:
            cb_ref[bi, 0:SUBLANE, :] = x[tb - SUBLANE:tb, :]
        y = y * jax.nn.sigmoid(y)

        ba = ba_ref[seq_rows(bi)]
        if tb < cc:
            ba = jnp.concatenate([ba, jnp.zeros((cc - tb, LANE), F32)], axis=0)
        beta_t = jnp.where(live, jax.nn.sigmoid(ba), 0.0)
        xs = ba + dtb_ref[...]
        softplus = jnp.maximum(xs, 0.0) + jnp.log(1.0 + jnp.exp(-jnp.abs(xs)))
        g_t = jnp.where(live, -jnp.exp(alog_ref[...]) * softplus, 0.0)
        gc = jnp.dot(tri, g_t, precision=HIGHEST, preferred_element_type=F32)
        gc_t = jnp.concatenate([gc, jnp.zeros((LANE - cc, LANE), F32)], axis=0).T
        gc_last = gc[cc - 1:cc, :]
        e_gc = jnp.exp(gc)
        e_rest = jnp.exp(gc_last - gc)
        e_end = jnp.exp(gc_last)
        for h in range(G_HEADS):
            sl = slice(h * G_DK, (h + 1) * G_DK)
            gl = slice(G_HEADS + h, G_HEADS + h + 1)
            q = y[:, sl]
            k = y[:, G_QK + h * G_DK:G_QK + (h + 1) * G_DK]
            v = y[:, 2 * G_QK + h * G_DV:2 * G_QK + (h + 1) * G_DV]
            q = q * (lax.rsqrt(jnp.sum(q * q, axis=-1, keepdims=True) + NORM_EPS) * (G_DK ** -0.5))
            k = k * lax.rsqrt(jnp.sum(k * k, axis=-1, keepdims=True) + NORM_EPS)
            if tb < cc:
                q = jnp.where(live, q, 0.0)
                k = jnp.where(live, k, 0.0)
                v = jnp.where(live, v, 0.0)
            beta = beta_t[:, h:h + 1]
            kb = k * beta
            qs.append(q)
            ks.append(k)
            kbs.append(kb)
            decays.append(jnp.exp(jnp.where(
                incl, gc[:, gl] - gc_t[G_HEADS + h:G_HEADS + h + 1, 0:cc], -jnp.inf)))
            rhss.append(jnp.concatenate([v * beta, kb * e_gc[:, gl]], axis=1))
            e_gcs.append(e_gc[:, gl])
            e_rests.append(e_rest[:, gl])
            e_ends.append(e_end[:, gl])

    kq = [_bdot(jnp.concatenate([kbs[i], qs[i]], axis=0), ks[i], dn_nt) for i in idx]
    a_low = [jnp.where(strict, kq[i][:cc] * decays[i], 0.0) for i in idx]
    qk = [kq[i][cc:] * decays[i] for i in idx]
    inv = [jnp.where(eye, 1.0, -a_low[i]) for i in idx]
    pw = a_low
    span = 2
    while span < cc:
        pw = [_bdot(pw[i], pw[i]) for i in idx]
        inv = [inv[i] + _bdot(pw[i], inv[i]) for i in idx]
        span *= 2
    rhss = [_bdot(inv[i], rhss[i]) for i in idx]
    s_old = [s_ref[i // G_HEADS, i % G_HEADS] for i in idx]
    ws = [_bdot(jnp.concatenate([rhss[i][:, G_DV:], qs[i] * e_gcs[i]], axis=0), s_old[i]) for i in idx]
    v_new = [rhss[i][:, :G_DV] - ws[i][:cc] for i in idx]
    for i in idx:
        s_ref[i // G_HEADS, i % G_HEADS] = (
            s_old[i] * e_ends[i] + _bdot(ks[i] * e_rests[i], v_new[i], dn_tn))
    o = [ws[i][cc:] + _bdot(qk[i], v_new[i]) for i in idx]
    for i in idx:
        bi, h = i // G_HEADS, i % G_HEADS
        sl = slice(h * G_DV, (h + 1) * G_DV)
        o_h = o[i] * lax.rsqrt(jnp.mean(o[i] * o[i], axis=-1, keepdims=True) + NORM_EPS) * nw_ref[...]
        zz = z_ref[seq_rows(bi) + (sl,)].astype(F32)
        o_ref[seq_rows(bi) + (sl,)] = o_h[:tb] * (zz * jax.nn.sigmoid(zz))

    @pl.when(n == nc - 1)
    def _():
        sout_ref[...] = s_ref[...]


def _gdn(u, ba, conv0, s0, conv_w, alog_t, dtb_t, norm_w, *, batch, seq, bpb):
    cc = G_CHUNK if seq >= G_CHUNK else SUBLANE
    tb = min(seq, cc)
    nc = seq // tb
    assert nc * tb == seq and (nc == 1 or tb == cc) and batch % bpb == 0
    flat = nc == 1
    if flat:
        assert (bpb * tb) % SUBLANE == 0
        uv, bav, o_rows = u, ba, batch * seq

        def rows(width, col):
            return pl.BlockSpec((bpb * tb, width), lambda b, n: (b, col))
    else:
        uv, bav, o_rows = u.reshape(batch, seq, UB_WIDTH), ba.reshape(batch, seq, LANE), batch

        def rows(width, col):
            return pl.BlockSpec((bpb, tb, width), lambda b, n: (b, n, col))
    o_shape = (o_rows, G_QK) if flat else (batch, seq, G_QK)
    conv0p = jnp.concatenate(
        [jnp.zeros((batch, SUBLANE - (G_CONV - 1), G_CONV_CH), F32), conv0.astype(F32)], axis=1)
    const2 = lambda b, n: (0, 0)
    o, s_new = pl.pallas_call(
        functools.partial(_gdn_kernel, tb=tb, cc=cc, nc=nc, bpb=bpb, flat=flat),
        grid=(batch // bpb, nc),
        in_specs=[
            rows(G_CONV_CH, QKVB_OFF // G_CONV_CH),
            rows(G_QK, Z_OFF // G_QK),
            rows(LANE, 0),
            pl.BlockSpec((G_CONV, G_CONV_CH), const2),
            pl.BlockSpec((1, LANE), const2),
            pl.BlockSpec((1, LANE), const2),
            pl.BlockSpec((1, G_DV), const2),
            pl.BlockSpec((bpb, SUBLANE, G_CONV_CH), lambda b, n: (b, 0, 0)),
            pl.BlockSpec((bpb, G_HEADS, G_DK, G_DV), lambda b, n: (b, 0, 0, 0)),
        ],
        out_specs=[
            rows(G_QK, 0),
            pl.BlockSpec((bpb, G_HEADS, G_DK, G_DV), lambda b, n: (b, 0, 0, 0)),
        ],
        out_shape=[
            jax.ShapeDtypeStruct(o_shape, F32),
            jax.ShapeDtypeStruct((batch, G_HEADS, G_DK, G_DV), F32),
        ],
        scratch_shapes=[
            pltpu.VMEM((bpb, SUBLANE + cc, G_CONV_CH), F32),
            pltpu.VMEM((bpb, G_HEADS, G_DK, G_DV), F32),
        ],
        compiler_params=_cparams(("parallel", "arbitrary")),
        name="gdn",
    )(uv, uv, bav, conv_w, alog_t, dtb_t, norm_w.reshape(1, G_DV), conv0p, s0)
    return o.reshape(batch * seq, G_QK), s_new


def _mix_kernel(x_ref, oa_ref, ob_ref, ga_ref, gb_ref, wa_ref, wb_ref, wo_ref, o_ref):
    pa = jnp.dot(oa_ref[...].astype(BF16), wa_ref[...], preferred_element_type=F32)
    pb = jnp.dot(ob_ref[...].astype(BF16), wb_ref[...], preferred_element_type=F32)
    merged = (jax.nn.sigmoid(ga_ref[...].astype(F32)) * pa
              + jax.nn.sigmoid(gb_ref[...].astype(F32)) * pb)
    o_ref[...] = x_ref[...] + jnp.dot(merged.astype(BF16), wo_ref[...], preferred_element_type=F32)


def _mix(x, oa, ob, u, wa, wb, wo, *, tm):
    n, d = x.shape
    const = lambda i: (0, 0)
    return pl.pallas_call(
        _mix_kernel,
        grid=(n // tm,),
        in_specs=[
            pl.BlockSpec((tm, d), lambda i: (i, 0)),
            pl.BlockSpec((tm, A_GW), lambda i: (i, 0)),
            pl.BlockSpec((tm, G_QK), lambda i: (i, 0)),
            pl.BlockSpec((tm, d), lambda i: (i, GATE_OFF // d)),
            pl.BlockSpec((tm, d), lambda i: (i, GATE_OFF // d + 1)),
            pl.BlockSpec((A_GW, d), const),
            pl.BlockSpec((G_QK, d), const),
            pl.BlockSpec((d, d), const),
        ],
        out_specs=pl.BlockSpec((tm, d), lambda i: (i, 0)),
        out_shape=jax.ShapeDtypeStruct((n, d), F32),
        compiler_params=_cparams(("parallel",)),
        name="mix",
    )(x, oa, ob, u, u, wa, wb, wo)


def _w_in_sources():
    aw = len(A_GROUPS) * A_GW
    src_q, src_k, src_v = 0, aw, 2 * aw
    src_qkvb = 3 * aw
    src_z = src_qkvb + G_CONV_CH
    src_ba = src_z + G_QK
    src_gate = src_ba + 2 * G_HEADS
    blocks = [src_qkvb + W_CB * t for t in range(G_CONV_CH // W_CB)]
    blocks += [src_z + W_CB * t for t in range(G_QK // W_CB)]
    blocks += [src_gate + W_CB * t for t in range(2 * D_MODEL // W_CB)]
    for g in range(len(A_GROUPS)):
        blocks += [src_k + g * A_GW, src_v + g * A_GW]
    blocks += [src_q + W_CB * t for t in range(aw // W_CB)]
    assert len(blocks) * W_CB == U_WIDTH and A_GW == W_CB
    assert all(b % SUBLANE == 0 for b in blocks)
    return blocks, src_ba


def _permute_cast_kernel(src_ref, w_ref, o_ref):
    del src_ref
    o_ref[...] = w_ref[...].astype(BF16)


def _permute_w_in(wt):
    d = wt.shape[1]
    blocks, _ = _w_in_sources()
    return pl.pallas_call(
        _permute_cast_kernel,
        grid_spec=pltpu.PrefetchScalarGridSpec(
            num_scalar_prefetch=1,
            grid=(len(blocks),),
            in_specs=[pl.BlockSpec((pl.Element(W_CB), pl.Element(d)),
                                   lambda j, src: (pl.multiple_of(src[j], SUBLANE), 0))],
            out_specs=pl.BlockSpec((W_CB, d), lambda j, src: (j, 0)),
        ),
        out_shape=jax.ShapeDtypeStruct((U_WIDTH, d), BF16),
        compiler_params=_cparams(("parallel",)),
        name="permute_w_in",
    )(jnp.asarray(blocks, jnp.int32), wt)


def _prep_weights(w_in, gdn_a_log, gdn_dt_bias):
    wt = w_in.T
    w_main = _permute_w_in(wt)
    src_ba = _w_in_sources()[1]
    n_ba = 2 * G_HEADS
    ba = wt[src_ba:src_ba + n_ba, :]
    ba_hi = ba.astype(BF16)
    ba_lo = (ba - ba_hi.astype(F32)).astype(BF16)
    w_ba = jnp.concatenate([ba_hi, ba_lo, jnp.zeros((MXU_COLS - 2 * n_ba, wt.shape[1]), BF16)], axis=0)
    pad = (G_HEADS, LANE - n_ba)
    alog_t = jnp.pad(gdn_a_log.astype(F32), pad).reshape(1, LANE)
    dtb_t = jnp.pad(gdn_dt_bias.astype(F32), pad).reshape(1, LANE)
    return w_main, w_ba, alog_t, dtb_t


def _dense_tiles(n_tokens):
    if n_tokens >= 2048:
        return dict(tm=2048, tf=256, tn=768, tm_mix=1024)
    return dict(tm=n_tokens, tf=256, tn=1536, tm_mix=n_tokens)


def _layer(x, batch, seq, conv0, s0, attn_fn, p, *, gdn_bpb, b_dtype):
    t = _dense_tiles(x.shape[0])
    tm = t["tm"]
    gdn_bpb = min(gdn_bpb, batch)
    tail = b_dtype != F32
    x1 = _ffn(x, p["norm_ffn1"], p["w_ffn1_gu"], p["w_ffn1_down"], tm=tm, tf=t["tf"])
    ub, ua, ba, *rest = _inproj(x1, p["norm_mix"], p["w_main"], p["w_ba"],
                                tm=tm, tn=t["tn"], b_dtype=b_dtype, tail=tail)
    oa, rows = attn_fn(ua)
    ob, s_new = _gdn(ub, ba, conv0, s0, p["conv_w"], p["alog_t"], p["dtb_t"], p["gdn_norm"],
                     batch=batch, seq=seq, bpb=gdn_bpb)
    x2 = _mix(x1, oa.reshape(batch * seq, A_GW), ob.reshape(batch * seq, G_QK), ub,
              p["w_proj_a"], p["w_proj_b"], p["w_out"], tm=t["tm_mix"])
    y = _ffn(x2, p["norm_ffn2"], p["w_ffn2_gu"], p["w_ffn2_down"], p["norm_out"], tm=tm, tf=t["tf"])
    if tail:
        assert seq % tm == 0 and seq >= SUBLANE
        last = rest[0].reshape(batch, seq // tm, SUBLANE, G_CONV_CH)[:, -1]
        conv_new = last[:, SUBLANE - (G_CONV - 1):, :][None]
    else:
        qkvb = ub[:, QKVB_OFF:QKVB_OFF + G_CONV_CH].reshape(batch, seq, G_CONV_CH)
        conv_new = qkvb[:, seq - (G_CONV - 1):, :][None]
    return y.reshape(batch, seq, D_MODEL), rows, conv_new, s_new[None]


def kernel(x_prompt, x_sample, cache_kv_w128, cache_kv_w512, cache_kv_w2048, state_conv, state_ssm,
           norm_ffn1, w_ffn1_gu, w_ffn1_down, norm_mix, w_in, conv_w, gdn_a_log, gdn_dt_bias, gdn_norm,
           w_proj_a, w_proj_b, w_out, norm_ffn2, w_ffn2_gu, w_ffn2_down, norm_out):
    assert w_in.shape[0] == 1, "single layer"
    n_heads = len(A_GROUPS) * A_HPG
    slopes = jnp.exp2(-8.0 * jnp.arange(1, n_heads + 1, dtype=F32) / n_heads)
    w_main, w_ba, alog_t, dtb_t = _prep_weights(w_in[0], gdn_a_log[0], gdn_dt_bias[0])
    p = dict(
        norm_ffn1=norm_ffn1[0], w_ffn1_gu=w_ffn1_gu[0], w_ffn1_down=w_ffn1_down[0],
        norm_mix=norm_mix[0], w_main=w_main, w_ba=w_ba, conv_w=conv_w[0],
        alog_t=alog_t, dtb_t=dtb_t, gdn_norm=gdn_norm[0],
        w_proj_a=w_proj_a[0].astype(BF16), w_proj_b=w_proj_b[0].astype(BF16), w_out=w_out[0].astype(BF16),
        norm_ffn2=norm_ffn2[0], w_ffn2_gu=w_ffn2_gu[0], w_ffn2_down=w_ffn2_down[0],
        norm_out=norm_out,
    )
    bp, tp, d = x_prompt.shape
    bs, ts, _ = x_sample.shape

    conv0_p = jnp.zeros((bp, G_CONV - 1, G_CONV_CH), F32)
    ssm0_p = jnp.zeros((bp, G_HEADS, G_DK, G_DV), F32)
    yp, rows_p, conv_p, ssm_p = _layer(
        x_prompt.reshape(bp * tp, d), bp, tp, conv0_p, ssm0_p,
        functools.partial(_attn_prompt, slopes, batch=bp, seq=tp), p, gdn_bpb=4, b_dtype=BF16)

    caches = (cache_kv_w128[0], cache_kv_w512[0], cache_kv_w2048[0])
    ys, rows_s, conv_s, ssm_s = _layer(
        x_sample.reshape(bs * ts, d), bs, ts, state_conv[0], state_ssm[0],
        lambda u: _attn_sample(slopes, u, caches, batch=bs, dec_seq=ts), p, gdn_bpb=16, b_dtype=F32)

    return (yp, ys, rows_p[0], rows_p[1], rows_p[2], conv_p, ssm_p,
            rows_s[0], rows_s[1], rows_s[2], conv_s, ssm_s)
```

```python
import functools

import jax
import jax.numpy as jnp
from jax import lax
from jax.experimental import pallas as pl
from jax.experimental.pallas import tpu as pltpu

F32 = jnp.float32
BF16 = jnp.bfloat16
HIGHEST = lax.Precision.HIGHEST

D_MODEL = 1024
D_FF = 2816
NORM_EPS = 1e-6
A_GROUPS = ((128, 1), (512, 4), (2048, 16))
A_HPG = 4
A_E = 128
A_BLOCK = 128
A_GW = A_HPG * A_E
A_ST = 2048
A_SAMPLE_SEQS = 4
G_HEADS = 8
G_DK = 128
G_DV = 128
G_QK = G_HEADS * G_DK
G_CONV_CH = 3 * G_QK
G_CONV = 4
G_CHUNK = 64

QKVB_OFF = 0
Z_OFF = 3072
GATE_OFF = 4096
UB_WIDTH = 6144
KV_OFF = 0
QA_OFF = 3072
UA_WIDTH = 4608
U_WIDTH = UB_WIDTH + UA_WIDTH
W_CB = 512

VMEM_LIMIT = 56 * 1024 * 1024
LANE = 128
SUBLANE = 8
MXU_COLS = 256
LOG2E = 1.4426950408889634


def _cparams(sem):
    return pltpu.CompilerParams(dimension_semantics=sem, vmem_limit_bytes=VMEM_LIMIT)


def _rms_rows(x, w):
    return x * lax.rsqrt(jnp.mean(x * x, axis=-1, keepdims=True) + NORM_EPS) * w


def _ffn_kernel(x_ref, nw_ref, wg_ref, wu_ref, wd_ref, *rest, n_ff, final_norm):
    if final_norm:
        onw_ref, o_ref, hb_ref = rest
    else:
        o_ref, hb_ref = rest
    j = pl.program_id(1)

    @pl.when(j == 0)
    def _():
        hb_ref[...] = _rms_rows(x_ref[...], nw_ref[...]).astype(BF16)
        o_ref[...] = jnp.zeros_like(o_ref)

    hb = hb_ref[...]
    g = jnp.dot(hb, wg_ref[...].astype(BF16), preferred_element_type=F32)
    u = jnp.dot(hb, wu_ref[...].astype(BF16), preferred_element_type=F32)
    a = (g * jax.nn.sigmoid(g) * u).astype(BF16)
    o_ref[...] += jnp.dot(a, wd_ref[...].astype(BF16), preferred_element_type=F32)

    @pl.when(j == n_ff - 1)
    def _():
        y = x_ref[...] + 0.5 * o_ref[...]
        if final_norm:
            y = _rms_rows(y, onw_ref[...])
        o_ref[...] = y


def _ffn(x, norm_w, w_gu, w_down, out_norm_w=None, *, tm, tf):
    n, d = x.shape
    n_ff = D_FF // tf
    final_norm = out_norm_w is not None
    in_specs = [
        pl.BlockSpec((tm, d), lambda i, j: (i, 0)),
        pl.BlockSpec((1, d), lambda i, j: (0, 0)),
        pl.BlockSpec((d, tf), lambda i, j: (0, j)),
        pl.BlockSpec((d, tf), lambda i, j: (0, j + n_ff)),
        pl.BlockSpec((tf, d), lambda i, j: (j, 0)),
    ]
    args = [x, norm_w.reshape(1, d), w_gu, w_gu, w_down]
    if final_norm:
        in_specs.append(pl.BlockSpec((1, d), lambda i, j: (0, 0)))
        args.append(out_norm_w.reshape(1, d))
    return pl.pallas_call(
        functools.partial(_ffn_kernel, n_ff=n_ff, final_norm=final_norm),
        grid=(n // tm, n_ff),
        in_specs=in_specs,
        out_specs=pl.BlockSpec((tm, d), lambda i, j: (i, 0)),
        out_shape=jax.ShapeDtypeStruct((n, d), F32),
        scratch_shapes=[pltpu.VMEM((tm, d), BF16)],
        compiler_params=_cparams(("parallel", "arbitrary")),
        name="ffn_final" if final_norm else "ffn",
    )(*args)


_DN_NT = (((1,), (1,)), ((), ()))


def _inproj_kernel(x_ref, nw_ref, w_ref, wba_ref, ub_ref, ua_ref, ba_ref, *rest, nb, nq, tail):
    if tail:
        tail_ref, hb_ref = rest
    else:
        (hb_ref,) = rest
    j = pl.program_id(1)

    @pl.when(j == 0)
    def _():
        h = _rms_rows(x_ref[...], nw_ref[...])
        hb = h.astype(BF16)
        hl = (h - hb.astype(F32)).astype(BF16)
        hb_ref[...] = hb
        n_ba = 2 * G_HEADS
        r_hi = lax.dot_general(hb, wba_ref[...], _DN_NT, preferred_element_type=F32)[:, :LANE]
        r_lo = lax.dot_general(hl, wba_ref[...], _DN_NT, preferred_element_type=F32)[:, :LANE]
        lane = lax.broadcasted_iota(jnp.int32, r_hi.shape, 1)
        ba_ref[...] = jnp.where(lane < n_ba, r_hi + pltpu.roll(r_hi, LANE - n_ba, 1) + r_lo, 0.0)

    def project(rows):
        return lax.dot_general(hb_ref[rows, :], w_ref[...], _DN_NT, preferred_element_type=F32)

    @pl.when(j < nb)
    def _():
        ub_ref[...] = project(slice(None)).astype(ub_ref.dtype)

    @pl.when(j >= nb)
    def _():
        ua_ref[...] = project(slice(None))

    if tail:
        @pl.when(j < nq)
        def _():
            tm = hb_ref.shape[0]
            tail_ref[...] = project(slice(tm - 2 * SUBLANE, tm))[SUBLANE:, :]


def _inproj(x, norm_w, w_main, w_ba, *, tm, tn, b_dtype, tail):
    n, d = x.shape
    nb, nq = UB_WIDTH // tn, G_CONV_CH // tn
    assert nb * tn == UB_WIDTH and nq * tn == G_CONV_CH and U_WIDTH % tn == 0 and QKVB_OFF == 0
    out_specs = [
        pl.BlockSpec((tm, tn), lambda i, j: (i, jnp.minimum(j, nb - 1))),
        pl.BlockSpec((tm, tn), lambda i, j: (i, jnp.maximum(j - nb, 0))),
        pl.BlockSpec((tm, LANE), lambda i, j: (i, 0)),
    ]
    out_shape = [
        jax.ShapeDtypeStruct((n, UB_WIDTH), b_dtype),
        jax.ShapeDtypeStruct((n, UA_WIDTH), F32),
        jax.ShapeDtypeStruct((n, LANE), F32),
    ]
    if tail:
        out_specs.append(pl.BlockSpec((SUBLANE, tn), lambda i, j: (i, jnp.minimum(j, nq - 1))))
        out_shape.append(jax.ShapeDtypeStruct((n // tm * SUBLANE, G_CONV_CH), F32))
    return pl.pallas_call(
        functools.partial(_inproj_kernel, nb=nb, nq=nq, tail=tail),
        grid=(n // tm, U_WIDTH // tn),
        in_specs=[
            pl.BlockSpec((tm, d), lambda i, j: (i, 0)),
            pl.BlockSpec((1, d), lambda i, j: (0, 0)),
            pl.BlockSpec((tn, d), lambda i, j: (j, 0)),
            pl.BlockSpec((MXU_COLS, d), lambda i, j: (0, 0)),
        ],
        out_specs=out_specs,
        out_shape=out_shape,
        scratch_shapes=[pltpu.VMEM((tm, d), BF16)],
        compiler_params=_cparams(("parallel", "arbitrary")),
        name="inproj",
    )(x, norm_w.reshape(1, d), w_main, w_ba)


def _rows(start, size, stride):
    return pl.ds(start, size) if stride == 1 else pl.ds(start, size, stride=stride)


def _attn_prompt_kernel(slopes_ref, *refs, n_st, keeps):
    n_g = len(A_GROUPS)
    ins, o_ref = refs[:5 * n_g], refs[5 * n_g]
    kv_refs, scr = refs[5 * n_g + 1:6 * n_g + 1], refs[6 * n_g + 1:]
    st = pl.program_id(1)
    hh = pl.program_id(2)

    @pl.when(st == n_st - 1)
    def _():
        for g in range(n_g):
            keep = keeps[g]
            kc_ref, vc_ref = ins[5 * g + 1], ins[5 * g + 2]
            kv_refs[g][pl.ds(hh, keep, stride=SUBLANE), :] = kc_ref[A_ST - keep:, :]
            kv_refs[g][pl.ds(A_HPG + hh, keep, stride=SUBLANE), :] = vc_ref[A_ST - keep:, :]

    scale = A_E ** -0.5
    qi = lax.broadcasted_iota(jnp.int32, (A_BLOCK, 2 * A_BLOCK), 0)
    kj = lax.broadcasted_iota(jnp.int32, (A_BLOCK, 2 * A_BLOCK), 1)
    delta = A_BLOCK + qi - kj
    band = jnp.logical_and(delta >= 0, delta <= A_BLOCK)
    band_first = jnp.logical_and(band, jnp.logical_or(kj >= A_BLOCK, st > 0))
    dn_nt = (((1,), (1,)), ((), ()))
    assert A_GROUPS[0][1] == 1
    for g, (win, dil) in reversed(list(enumerate(A_GROUPS))):
        q_ref, kc_ref, vc_ref, kp_ref, vp_ref = ins[5 * g:5 * g + 5]
        og_ref, lg_ref = scr[2 * g:2 * g + 2]
        slope = slopes_ref[g * A_HPG + hh]
        bias = -slope * (dil * delta).astype(F32) * LOG2E
        bias_mid = jnp.where(band, bias, -jnp.inf)
        bias_first = jnp.where(band_first, bias, -jnp.inf)

        for r in range(dil):
            k_prev = kp_ref[_rows(r, A_BLOCK, dil), :].astype(BF16)
            v_prev = vp_ref[_rows(r, A_BLOCK, dil), :].astype(BF16)
            for j in range(A_ST // (A_BLOCK * dil)):
                row0 = r + j * A_BLOCK * dil
                k_cur = kc_ref[_rows(row0, A_BLOCK, dil), :].astype(BF16)
                v_cur = vc_ref[_rows(row0, A_BLOCK, dil), :].astype(BF16)
                k2 = jnp.concatenate([k_prev, k_cur], axis=0)
                v2 = jnp.concatenate([v_prev, v_cur], axis=0)
                q = (q_ref[_rows(row0, A_BLOCK, dil), :] * (scale * LOG2E)).astype(BF16)
                s = lax.dot_general(q, k2, dn_nt, preferred_element_type=F32)
                s = s + (bias_first if j == 0 else bias_mid)
                m = jnp.max(s, axis=-1, keepdims=True)
                p = jnp.exp2(s - m)
                den = jnp.sum(p, axis=-1, keepdims=True)
                acc = jnp.dot(p.astype(BF16), v2, preferred_element_type=F32)
                o_blk = acc / den
                l_blk = jnp.broadcast_to(m + jnp.log2(den), (A_BLOCK, A_E))
                k_prev, v_prev = k_cur, v_cur
                if g > 0:
                    og_ref[_rows(row0, A_BLOCK, dil), :] = o_blk
                    lg_ref[_rows(row0, A_BLOCK, dil), :] = l_blk
                    continue
                rs = slice(row0, row0 + A_BLOCK)
                ls = [l_blk] + [scr[2 * gg + 1][rs, :] for gg in range(1, n_g)]
                os_ = [o_blk] + [scr[2 * gg][rs, :] for gg in range(1, n_g)]
                mx = functools.reduce(jnp.maximum, ls)
                ws = [jnp.exp2(l - mx) for l in ls]
                num = sum(w * o for w, o in zip(ws, os_))
                o_ref[rs, :] = num / sum(ws)


def _attn_prompt(slopes, u, *, batch, seq):
    assert seq % A_ST == 0
    u3 = u.reshape(batch, seq, UA_WIDTH)
    in_specs = [pl.BlockSpec(memory_space=pltpu.SMEM)]
    args = [slopes]
    scratch = []
    keeps = tuple(min(win, seq) for win, _ in A_GROUPS)
    assert max(keeps) <= A_ST
    for g, (win, dil) in enumerate(A_GROUPS):
        assert win == A_BLOCK * dil and A_ST % win == 0
        qcb = (QA_OFF + g * A_GW) // A_E
        kcb = (KV_OFF + g * 2 * A_GW) // A_E
        vcb = kcb + A_HPG
        per = A_ST // win

        def cur(cb):
            return pl.BlockSpec((None, A_ST, A_E), lambda b, st, hh, cb=cb: (b, st, cb + hh))

        def prv(cb, win=win, per=per):
            return pl.BlockSpec((None, win, A_E),
                                lambda b, st, hh, cb=cb: (b, jnp.maximum(st * per - 1, 0), cb + hh))

        in_specs += [cur(qcb), cur(kcb), cur(vcb), prv(kcb), prv(vcb)]
        args += [u3] * 5
        scratch += [pltpu.VMEM((A_ST, A_E), F32), pltpu.VMEM((A_ST, A_E), F32)]
    n_st = seq // A_ST
    out_specs = [pl.BlockSpec((None, A_ST, A_E), lambda b, st, hh: (b, st, hh))]
    out_shape = [jax.ShapeDtypeStruct((batch, seq, A_GW), F32)]
    for keep in keeps:
        out_specs.append(pl.BlockSpec((None, keep * SUBLANE, A_E), lambda b, st, hh: (b, 0, 0),
                                      pipeline_mode=pl.Buffered(1)))
        out_shape.append(jax.ShapeDtypeStruct((batch, keep * SUBLANE, A_E), F32))
    outs = pl.pallas_call(
        functools.partial(_attn_prompt_kernel, n_st=n_st, keeps=keeps),
        grid=(batch, n_st, A_HPG),
        in_specs=in_specs,
        out_specs=out_specs,
        out_shape=out_shape,
        scratch_shapes=scratch,
        compiler_params=_cparams(("parallel", "arbitrary", "arbitrary")),
        name="attn_prompt",
    )(*args)
    rows = [kv.reshape(1, batch, keep, 2, A_HPG, A_E) for kv, keep in zip(outs[1:], keeps)]
    return outs[0], rows


def _pick_rows(rows):
    sub = lax.broadcasted_iota(jnp.int32, (SUBLANE, A_E), 0)
    out = jnp.zeros((SUBLANE, A_E), F32)
    for i, r in enumerate(rows):
        out = jnp.where(sub == i, jnp.broadcast_to(r, (SUBLANE, A_E)), out)
    return out


def _attn_sample_kernel(slopes_ref, q_ref, kv_ref, c0_ref, c1_ref, c2_ref, o_ref, *kvo_refs, dec_seq, nseq):
    scale = A_E ** -0.5
    caches = (c0_ref, c1_ref, c2_ref)
    for g, kvo_ref in enumerate(kvo_refs):
        for t in range(nseq * dec_seq):
            kvo_ref[t * SUBLANE:(t + 1) * SUBLANE, :] = _pick_rows(
                [kv_ref[t:t + 1, g * 2 * A_GW + c * A_E:g * 2 * A_GW + (c + 1) * A_E] for c in range(2 * A_HPG)])
    n_keys = A_BLOCK * SUBLANE
    sub = lax.broadcasted_iota(jnp.int32, (SUBLANE, n_keys), 0)
    lane = lax.broadcasted_iota(jnp.int32, (SUBLANE, n_keys), 1)
    tile_row = lane & (SUBLANE - 1)
    m_idx = lane >> 3
    own_k = tile_row == sub
    sub1 = lax.broadcasted_iota(jnp.int32, (SUBLANE, 1), 0)
    dn_nt = (((1,), (1,)), ((), ()))
    slope8s = []
    for g in range(len(A_GROUPS)):
        slope8 = jnp.zeros((SUBLANE, 1), F32)
        for h in range(A_HPG):
            slope8 = jnp.where(sub1 == h, slopes_ref[g * A_HPG + h], slope8)
        slope8s.append(slope8)

    def keys(bb, s, g):
        res = 0 if A_GROUPS[g][1] == 1 else s
        return caches[g][bb, :, res, :, :].reshape(n_keys, A_E).astype(BF16)

    units = [(bb, s, g) for bb in range(nseq) for s in range(dec_seq) for g in range(len(A_GROUPS))]
    q8s = [_pick_rows([q_ref[bb * dec_seq + s:bb * dec_seq + s + 1, g * A_GW + h * A_E:g * A_GW + (h + 1) * A_E]
                       for h in range(A_HPG)]) for bb, s, g in units]
    scs = [lax.dot_general(q8.astype(BF16), keys(*u), dn_nt, preferred_element_type=F32) * scale
           for q8, u in zip(q8s, units)]
    ms, dens, pvs, news = [], [], [], []
    for (bb, s, g), q8, sc in zip(units, q8s, scs):
        win, dil = A_GROUPS[g]
        row = bb * dec_seq + s
        slope8 = slope8s[g]
        if dil == 1:
            dist = (win + s - m_idx).astype(F32)
            valid = jnp.logical_and(own_k, m_idx >= s)
            new_rows = [(bb * dec_seq + t, float(s - t)) for t in range(s + 1)]
        else:
            dist = (dil * (win // dil - m_idx)).astype(F32)
            valid = own_k
            new_rows = [(row, 0.0)]
        sc = jnp.where(valid, sc - slope8 * dist, -jnp.inf)
        m = jnp.max(sc, axis=-1, keepdims=True)
        koff = g * 2 * A_GW
        new = []
        for nrow, ndist in new_rows:
            k8 = _pick_rows([kv_ref[nrow:nrow + 1, koff + h * A_E:koff + (h + 1) * A_E]
                             for h in range(A_HPG)])
            v8 = _pick_rows([kv_ref[nrow:nrow + 1, koff + A_GW + h * A_E:koff + A_GW + (h + 1) * A_E]
                             for h in range(A_HPG)])
            s_n = jnp.sum(k8 * q8, axis=-1, keepdims=True) * scale - slope8 * ndist
            m = jnp.maximum(m, s_n)
            new.append((s_n, v8))
        p = jnp.exp(sc - m)
        ms.append(m)
        dens.append(jnp.sum(p, axis=-1, keepdims=True))
        pvs.append(pltpu.roll(p, A_HPG, 1).astype(BF16))
        news.append(new)
    accs = [jnp.dot(pv, keys(*u), preferred_element_type=F32) for pv, u in zip(pvs, units)]

    out_tiles = [jnp.zeros((nseq * dec_seq, A_E), F32) for _ in range(A_HPG)]
    out_sub = lax.broadcasted_iota(jnp.int32, (nseq * dec_seq, A_E), 0)
    n_g = len(A_GROUPS)
    for ui in range(0, len(units), n_g):
        bb, s, _ = units[ui]
        outs, lses = [], []
        for m, den, acc, new in zip(ms[ui:ui + n_g], dens[ui:ui + n_g], accs[ui:ui + n_g], news[ui:ui + n_g]):
            for s_n, v8 in new:
                p_n = jnp.exp(s_n - m)
                den = den + p_n
                acc = acc + p_n * v8
            outs.append(acc / den)
            lses.append(m + jnp.log(den))
        mx = functools.reduce(jnp.maximum, lses)
        ws = [jnp.exp(l - mx) for l in lses]
        o8 = sum(w * o for w, o in zip(ws, outs)) / sum(ws)
        for h in range(A_HPG):
            out_tiles[h] = jnp.where(out_sub == bb * dec_seq + s,
                                     jnp.broadcast_to(o8[h:h + 1, :], (nseq * dec_seq, A_E)), out_tiles[h])
    for h in range(A_HPG):
        o_ref[:, h * A_E:(h + 1) * A_E] = out_tiles[h]


def _attn_sample(slopes, u, caches, *, batch, dec_seq):
    nseq = min(A_SAMPLE_SEQS, batch)
    assert (nseq * dec_seq) % SUBLANE == 0 and batch % nseq == 0
    views, specs = [], []
    for (win, dil), cache in zip(A_GROUPS, caches):
        wb = cache.shape[1]
        assert wb == win and wb // dil == A_BLOCK and (dil == 1 or dec_seq <= dil)
        views.append(cache.reshape(batch, wb // dil, dil, SUBLANE, A_E))
        res = 1 if dil == 1 else dec_seq
        specs.append(pl.BlockSpec((nseq, A_BLOCK, res, SUBLANE, A_E), lambda i: (i, 0, 0, 0, 0)))
    n_q = len(A_GROUPS) * A_GW
    rows = nseq * dec_seq
    outs = pl.pallas_call(
        functools.partial(_attn_sample_kernel, dec_seq=dec_seq, nseq=nseq),
        grid=(batch // nseq,),
        in_specs=[
            pl.BlockSpec(memory_space=pltpu.SMEM),
            pl.BlockSpec((rows, n_q), lambda i: (i, QA_OFF // n_q)),
            pl.BlockSpec((rows, 2 * n_q), lambda i: (i, KV_OFF // (2 * n_q))),
        ] + specs,
        out_specs=[pl.BlockSpec((rows, A_GW), lambda i: (i, 0))]
        + [pl.BlockSpec((rows * SUBLANE, A_E), lambda i: (i, 0))] * len(A_GROUPS),
        out_shape=[jax.ShapeDtypeStruct((batch * dec_seq, A_GW), F32)]
        + [jax.ShapeDtypeStruct((batch * dec_seq * SUBLANE, A_E), F32)] * len(A_GROUPS),
        compiler_params=_cparams(("parallel",)),
        name="attn_sample",
    )(slopes, u, u, *views)
    return outs[0], [kv.reshape(1, batch, dec_seq, 2, A_HPG, A_E) for kv in outs[1:]]


def _bdot(a, b, dn=(((1,), (0,)), ((), ()))):
    return lax.dot_general(a.astype(BF16), b.astype(BF16), dn, preferred_element_type=F32)


def _gdn_kernel(qkv_ref, z_ref, ba_ref, cw_ref, alog_ref, dtb_ref, nw_ref, conv0_ref, s0_ref,
                o_ref, sout_ref, cb_ref, s_ref, *, tb, cc, nc, bpb, flat):
    n = pl.program_id(1)

    def seq_rows(bi):
        return (slice(bi * tb, (bi + 1) * tb),) if flat else (bi, slice(None))

    @pl.when(n == 0)
    def _():
        s_ref[...] = s0_ref[...]
        cb_ref[:, 0:SUBLANE, :] = conv0_ref[...]

    rows = lax.broadcasted_iota(jnp.int32, (cc, 1), 0)
    live = rows < tb
    ii = lax.broadcasted_iota(jnp.int32, (cc, cc), 0)
    jj = lax.broadcasted_iota(jnp.int32, (cc, cc), 1)
    incl = ii >= jj
    strict = ii > jj
    tri = incl.astype(F32)
    cw = cw_ref[...]
    first = SUBLANE - (G_CONV - 1)
    dn_nt = (((1,), (1,)), ((), ()))
    dn_tn = (((0,), (0,)), ((), ()))
    eye = ii == jj

    qs, ks, kbs, decays, rhss, e_gcs, e_rests, e_ends = [], [], [], [], [], [], [], []
    idx = range(bpb * G_HEADS)
    for bi in range(bpb):
        x = qkv_ref[seq_rows(bi)].astype(F32)
        cb_ref[bi, SUBLANE:SUBLANE + tb, :] = x
        if tb < cc:
            cb_ref[bi, SUBLANE + tb:SUBLANE + cc, :] = jnp.zeros((cc - tb, G_CONV_CH), F32)
        y = cb_ref[bi, SUBLANE:SUBLANE + cc, :] * cw[G_CONV - 1:G_CONV, :]
        for j in range(G_CONV - 2, -1, -1):
            y = y + cb_ref[bi, first + j:first + j + cc, :] * cw[j:j + 1, :]
        if nc > 1:
            cb_ref[bi, 0:SUBLANE, :] = x[tb - SUBLANE:tb, :]
        y = y * jax.nn.sigmoid(y)

        ba = ba_ref[seq_rows(bi)]
        if tb < cc:
            ba = jnp.concatenate([ba, jnp.zeros((cc - tb, LANE), F32)], axis=0)
        beta_t = jnp.where(live, jax.nn.sigmoid(ba), 0.0)
        xs = ba + dtb_ref[...]
        softplus = jnp.maximum(xs, 0.0) + jnp.log(1.0 + jnp.exp(-jnp.abs(xs)))
        g_t = jnp.where(live, -jnp.exp(alog_ref[...]) * softplus, 0.0)
        gc = jnp.dot(tri, g_t, precision=HIGHEST, preferred_element_type=F32)
        gc_t = jnp.concatenate([gc, jnp.zeros((LANE - cc, LANE), F32)], axis=0).T
        gc_last = gc[cc - 1:cc, :]
        e_gc = jnp.exp(gc)
        e_rest = jnp.exp(gc_last - gc)
        e_end = jnp.exp(gc_last)
        for h in range(G_HEADS):
            sl = slice(h * G_DK, (h + 1) * G_DK)
            gl = slice(G_HEADS + h, G_HEADS + h + 1)
            q = y[:, sl]
            k = y[:, G_QK + h * G_DK:G_QK + (h + 1) * G_DK]
            v = y[:, 2 * G_QK + h * G_DV:2 * G_QK + (h + 1) * G_DV]
            q = q * (lax.rsqrt(jnp.sum(q * q, axis=-1, keepdims=True) + NORM_EPS) * (G_DK ** -0.5))
            k = k * lax.rsqrt(jnp.sum(k * k, axis=-1, keepdims=True) + NORM_EPS)
            if tb < cc:
                q = jnp.where(live, q, 0.0)
                k = jnp.where(live, k, 0.0)
                v = jnp.where(live, v, 0.0)
            beta = beta_t[:, h:h + 1]
            kb = k * beta
            qs.append(q)
            ks.append(k)
            kbs.append(kb)
            decays.append(jnp.exp(jnp.where(
                incl, gc[:, gl] - gc_t[G_HEADS + h:G_HEADS + h + 1, 0:cc], -jnp.inf)))
            rhss.append(jnp.concatenate([v * beta, kb * e_gc[:, gl]], axis=1))
            e_gcs.append(e_gc[:, gl])
            e_rests.append(e_rest[:, gl])
            e_ends.append(e_end[:, gl])

    kq = [_bdot(jnp.concatenate([kbs[i], qs[i]], axis=0), ks[i], dn_nt) for i in idx]
    a_low = [jnp.where(strict, kq[i][:cc] * decays[i], 0.0) for i in idx]
    qk = [kq[i][cc:] * decays[i] for i in idx]
    inv = [jnp.where(eye, 1.0, -a_low[i]) for i in idx]
    pw = a_low
    span = 2
    while span < cc:
        pw = [_bdot(pw[i], pw[i]) for i in idx]
        inv = [inv[i] + _bdot(pw[i], inv[i]) for i in idx]
        span *= 2
    rhss = [_bdot(inv[i], rhss[i]) for i in idx]
    s_old = [s_ref[i // G_HEADS, i % G_HEADS] for i in idx]
    ws = [_bdot(jnp.concatenate([rhss[i][:, G_DV:], qs[i] * e_gcs[i]], axis=0), s_old[i]) for i in idx]
    v_new = [rhss[i][:, :G_DV] - ws[i][:cc] for i in idx]
    for i in idx:
        s_ref[i // G_HEADS, i % G_HEADS] = (
            s_old[i] * e_ends[i] + _bdot(ks[i] * e_rests[i], v_new[i], dn_tn))
    o = [ws[i][cc:] + _bdot(qk[i], v_new[i]) for i in idx]
    for i in idx:
        bi, h = i // G_HEADS, i % G_HEADS
        sl = slice(h * G_DV, (h + 1) * G_DV)
        o_h = o[i] * lax.rsqrt(jnp.mean(o[i] * o[i], axis=-1, keepdims=True) + NORM_EPS) * nw_ref[...]
        zz = z_ref[seq_rows(bi) + (sl,)].astype(F32)
        o_ref[seq_rows(bi) + (sl,)] = o_h[:tb] * (zz * jax.nn.sigmoid(zz))

    @pl.when(n == nc - 1)
    def _():
        sout_ref[...] = s_ref[...]


def _gdn(u, ba, conv0, s0, conv_w, alog_t, dtb_t, norm_w, *, batch, seq, bpb):
    cc = G_CHUNK if seq >= G_CHUNK else SUBLANE
    tb = min(seq, cc)
    nc = seq // tb
    assert nc * tb == seq and (nc == 1 or tb == cc) and batch % bpb == 0
    flat = nc == 1
    if flat:
        assert (bpb * tb) % SUBLANE == 0
        uv, bav, o_rows = u, ba, batch * seq

        def rows(width, col):
            return pl.BlockSpec((bpb * tb, width), lambda b, n: (b, col))
    else:
        uv, bav, o_rows = u.reshape(batch, seq, UB_WIDTH), ba.reshape(batch, seq, LANE), batch

        def rows(width, col):
            return pl.BlockSpec((bpb, tb, width), lambda b, n: (b, n, col))
    o_shape = (o_rows, G_QK) if flat else (batch, seq, G_QK)
    conv0p = jnp.concatenate(
        [jnp.zeros((batch, SUBLANE - (G_CONV - 1), G_CONV_CH), F32), conv0.astype(F32)], axis=1)
    const2 = lambda b, n: (0, 0)
    o, s_new = pl.pallas_call(
        functools.partial(_gdn_kernel, tb=tb, cc=cc, nc=nc, bpb=bpb, flat=flat),
        grid=(batch // bpb, nc),
        in_specs=[
            rows(G_CONV_CH, QKVB_OFF // G_CONV_CH),
            rows(G_QK, Z_OFF // G_QK),
            rows(LANE, 0),
            pl.BlockSpec((G_CONV, G_CONV_CH), const2),
            pl.BlockSpec((1, LANE), const2),
            pl.BlockSpec((1, LANE), const2),
            pl.BlockSpec((1, G_DV), const2),
            pl.BlockSpec((bpb, SUBLANE, G_CONV_CH), lambda b, n: (b, 0, 0)),
            pl.BlockSpec((bpb, G_HEADS, G_DK, G_DV), lambda b, n: (b, 0, 0, 0)),
        ],
        out_specs=[
            rows(G_QK, 0),
            pl.BlockSpec((bpb, G_HEADS, G_DK, G_DV), lambda b, n: (b, 0, 0, 0)),
        ],
        out_shape=[
            jax.ShapeDtypeStruct(o_shape, F32),
            jax.ShapeDtypeStruct((batch, G_HEADS, G_DK, G_DV), F32),
        ],
        scratch_shapes=[
            pltpu.VMEM((bpb, SUBLANE + cc, G_CONV_CH), F32),
            pltpu.VMEM((bpb, G_HEADS, G_DK, G_DV), F32),
        ],
        compiler_params=_cparams(("parallel", "arbitrary")),
        name="gdn",
    )(uv, uv, bav, conv_w, alog_t, dtb_t, norm_w.reshape(1, G_DV), conv0p, s0)
    return o.reshape(batch * seq, G_QK), s_new


def _mix_kernel(x_ref, oa_ref, ob_ref, ga_ref, gb_ref, wa_ref, wb_ref, wo_ref, o_ref):
    pa = jnp.dot(oa_ref[...].astype(BF16), wa_ref[...], preferred_element_type=F32)
    pb = jnp.dot(ob_ref[...].astype(BF16), wb_ref[...], preferred_element_type=F32)
    merged = (jax.nn.sigmoid(ga_ref[...].astype(F32)) * pa
              + jax.nn.sigmoid(gb_ref[...].astype(F32)) * pb)
    o_ref[...] = x_ref[...] + jnp.dot(merged.astype(BF16), wo_ref[...], preferred_element_type=F32)


def _mix(x, oa, ob, u, wa, wb, wo, *, tm):
    n, d = x.shape
    const = lambda i: (0, 0)
    return pl.pallas_call(
        _mix_kernel,
        grid=(n // tm,),
        in_specs=[
            pl.BlockSpec((tm, d), lambda i: (i, 0)),
            pl.BlockSpec((tm, A_GW), lambda i: (i, 0)),
            pl.BlockSpec((tm, G_QK), lambda i: (i, 0)),
            pl.BlockSpec((tm, d), lambda i: (i, GATE_OFF // d)),
            pl.BlockSpec((tm, d), lambda i: (i, GATE_OFF // d + 1)),
            pl.BlockSpec((A_GW, d), const),
            pl.BlockSpec((G_QK, d), const),
            pl.BlockSpec((d, d), const),
        ],
        out_specs=pl.BlockSpec((tm, d), lambda i: (i, 0)),
        out_shape=jax.ShapeDtypeStruct((n, d), F32),
        compiler_params=_cparams(("parallel",)),
        name="mix",
    )(x, oa, ob, u, u, wa, wb, wo)


def _w_in_sources():
    aw = len(A_GROUPS) * A_GW
    src_q, src_k, src_v = 0, aw, 2 * aw
    src_qkvb = 3 * aw
    src_z = src_qkvb + G_CONV_CH
    src_ba = src_z + G_QK
    src_gate = src_ba + 2 * G_HEADS
    blocks = [src_qkvb + W_CB * t for t in range(G_CONV_CH // W_CB)]
    blocks += [src_z + W_CB * t for t in range(G_QK // W_CB)]
    blocks += [src_gate + W_CB * t for t in range(2 * D_MODEL // W_CB)]
    for g in range(len(A_GROUPS)):
        blocks += [src_k + g * A_GW, src_v + g * A_GW]
    blocks += [src_q + W_CB * t for t in range(aw // W_CB)]
    assert len(blocks) * W_CB == U_WIDTH and A_GW == W_CB
    assert all(b % SUBLANE == 0 for b in blocks)
    return blocks, src_ba


def _permute_cast_kernel(src_ref, w_ref, o_ref):
    del src_ref
    o_ref[...] = w_ref[...].astype(BF16)


def _permute_w_in(wt):
    d = wt.shape[1]
    blocks, _ = _w_in_sources()
    return pl.pallas_call(
        _permute_cast_kernel,
        grid_spec=pltpu.PrefetchScalarGridSpec(
            num_scalar_prefetch=1,
            grid=(len(blocks),),
            in_specs=[pl.BlockSpec((pl.Element(W_CB), pl.Element(d)),
                                   lambda j, src: (pl.multiple_of(src[j], SUBLANE), 0))],
            out_specs=pl.BlockSpec((W_CB, d), lambda j, src: (j, 0)),
        ),
        out_shape=jax.ShapeDtypeStruct((U_WIDTH, d), BF16),
        compiler_params=_cparams(("parallel",)),
        name="permute_w_in",
    )(jnp.asarray(blocks, jnp.int32), wt)


def _prep_weights(w_in, gdn_a_log, gdn_dt_bias):
    wt = w_in.T
    w_main = _permute_w_in(wt)
    src_ba = _w_in_sources()[1]
    n_ba = 2 * G_HEADS
    ba = wt[src_ba:src_ba + n_ba, :]
    ba_hi = ba.astype(BF16)
    ba_lo = (ba - ba_hi.astype(F32)).astype(BF16)
    w_ba = jnp.concatenate([ba_hi, ba_lo, jnp.zeros((MXU_COLS - 2 * n_ba, wt.shape[1]), BF16)], axis=0)
    pad = (G_HEADS, LANE - n_ba)
    alog_t = jnp.pad(gdn_a_log.astype(F32), pad).reshape(1, LANE)
    dtb_t = jnp.pad(gdn_dt_bias.astype(F32), pad).reshape(1, LANE)
    return w_main, w_ba, alog_t, dtb_t


def _dense_tiles(n_tokens):
    if n_tokens >= 2048:
        return dict(tm=2048, tf=256, tn=768, tm_mix=1024)
    return dict(tm=n_tokens, tf=256, tn=1536, tm_mix=n_tokens)


def _layer(x, batch, seq, conv0, s0, attn_fn, p, *, gdn_bpb, b_dtype):
    t = _dense_tiles(x.shape[0])
    tm = t["tm"]
    gdn_bpb = min(gdn_bpb, batch)
    tail = b_dtype != F32
    x1 = _ffn(x, p["norm_ffn1"], p["w_ffn1_gu"], p["w_ffn1_down"], tm=tm, tf=t["tf"])
    ub, ua, ba, *rest = _inproj(x1, p["norm_mix"], p["w_main"], p["w_ba"],
                                tm=tm, tn=t["tn"], b_dtype=b_dtype, tail=tail)
    oa, rows = attn_fn(ua)
    ob, s_new = _gdn(ub, ba, conv0, s0, p["conv_w"], p["alog_t"], p["dtb_t"], p["gdn_norm"],
                     batch=batch, seq=seq, bpb=gdn_bpb)
    x2 = _mix(x1, oa.reshape(batch * seq, A_GW), ob.reshape(batch * seq, G_QK), ub,
              p["w_proj_a"], p["w_proj_b"], p["w_out"], tm=t["tm_mix"])
    y = _ffn(x2, p["norm_ffn2"], p["w_ffn2_gu"], p["w_ffn2_down"], p["norm_out"], tm=tm, tf=t["tf"])
    if tail:
        assert seq % tm == 0 and seq >= SUBLANE
        last = rest[0].reshape(batch, seq // tm, SUBLANE, G_CONV_CH)[:, -1]
        conv_new = last[:, SUBLANE - (G_CONV - 1):, :][None]
    else:
        qkvb = ub[:, QKVB_OFF:QKVB_OFF + G_CONV_CH].reshape(batch, seq, G_CONV_CH)
        conv_new = qkvb[:, seq - (G_CONV - 1):, :][None]
    return y.reshape(batch, seq, D_MODEL), rows, conv_new, s_new[None]


def kernel(x_prompt, x_sample, cache_kv_w128, cache_kv_w512, cache_kv_w2048, state_conv, state_ssm,
           norm_ffn1, w_ffn1_gu, w_ffn1_down, norm_mix, w_in, conv_w, gdn_a_log, gdn_dt_bias, gdn_norm,
           w_proj_a, w_proj_b, w_out, norm_ffn2, w_ffn2_gu, w_ffn2_down, norm_out):
    assert w_in.shape[0] == 1, "single layer"
    n_heads = len(A_GROUPS) * A_HPG
    slopes = jnp.exp2(-8.0 * jnp.arange(1, n_heads + 1, dtype=F32) / n_heads)
    w_main, w_ba, alog_t, dtb_t = _prep_weights(w_in[0], gdn_a_log[0], gdn_dt_bias[0])
    p = dict(
        norm_ffn1=norm_ffn1[0], w_ffn1_gu=w_ffn1_gu[0], w_ffn1_down=w_ffn1_down[0],
        norm_mix=norm_mix[0], w_main=w_main, w_ba=w_ba, conv_w=conv_w[0],
        alog_t=alog_t, dtb_t=dtb_t, gdn_norm=gdn_norm[0],
        w_proj_a=w_proj_a[0].astype(BF16), w_proj_b=w_proj_b[0].astype(BF16), w_out=w_out[0].astype(BF16),
        norm_ffn2=norm_ffn2[0], w_ffn2_gu=w_ffn2_gu[0], w_ffn2_down=w_ffn2_down[0],
        norm_out=norm_out,
    )
    bp, tp, d = x_prompt.shape
    bs, ts, _ = x_sample.shape

    conv0_p = jnp.zeros((bp, G_CONV - 1, G_CONV_CH), F32)
    ssm0_p = jnp.zeros((bp, G_HEADS, G_DK, G_DV), F32)
    yp, rows_p, conv_p, ssm_p = _layer(
        x_prompt.reshape(bp * tp, d), bp, tp, conv0_p, ssm0_p,
        functools.partial(_attn_prompt, slopes, batch=bp, seq=tp), p, gdn_bpb=4, b_dtype=BF16)

    caches = (cache_kv_w128[0], cache_kv_w512[0], cache_kv_w2048[0])
    ys, rows_s, conv_s, ssm_s = _layer(
        x_sample.reshape(bs * ts, d), bs, ts, state_conv[0], state_ssm[0],
        lambda u: _attn_sample(slopes, u, caches, batch=bs, dec_seq=ts), p, gdn_bpb=16, b_dtype=F32)

    return (yp, ys, rows_p[0], rows_p[1], rows_p[2], conv_p, ssm_p,
            rows_s[0], rows_s[1], rows_s[2], conv_s, ssm_s)
```
